```python
import jax, jax.numpy as jnp
from jax import lax
import numpy as np

D_MODEL = 1024
BATCH = 8
SEQ = 2048
DEPTH = 1

MIX_WIDTH = D_MODEL
HEAD_DIM = 64
ATTN_WIDTH = MIX_WIDTH // 2
N_ATTN_HEADS = ATTN_WIDTH // HEAD_DIM
DILATED_BRANCHES = ((128, 1), (512, 4), (2048, 16))
Q_BLOCK = 128
ROPE_THETA = 10000.0
SSD_WIDTH = MIX_WIDTH - ATTN_WIDTH
SSD_HEAD_DIM = 64
N_SSD_HEADS = SSD_WIDTH // SSD_HEAD_DIM
N_SSD_GROUPS = 2
SSD_STATE = 128
CONV_WIDTH = 4
CHUNK = 128
D_FF = 4 * D_MODEL
EPS = 1e-6
CONV_CHANNELS = SSD_WIDTH + 2 * N_SSD_GROUPS * SSD_STATE
IN_PROJ_WIDTH = 3 * ATTN_WIDTH + SSD_WIDTH + CONV_CHANNELS + N_SSD_HEADS

kernel_name = "hybrid_dilated_attn_ssd_block"


def rms_norm(x, w):
    xf = x.astype(jnp.float32)
    y = xf * lax.rsqrt(jnp.mean(xf * xf, axis=-1, keepdims=True) + EPS)
    return (y * w.astype(jnp.float32)).astype(x.dtype)


def rope(x):
    s, d = x.shape[1], x.shape[-1]
    half = d // 2
    inv_freq = ROPE_THETA ** (-jnp.arange(half, dtype=jnp.float32) / half)
    ang = jnp.arange(s, dtype=jnp.float32)[:, None] * inv_freq[None, :]
    cos = jnp.cos(ang)[None, :, None, :]
    sin = jnp.sin(ang)[None, :, None, :]
    xf = x.astype(jnp.float32)
    x1, x2 = xf[..., :half], xf[..., half:]
    out = jnp.concatenate([x1 * cos - x2 * sin, x2 * cos + x1 * sin], axis=-1)
    return out.astype(x.dtype)


def dilated_attention(q, k, v):
    b, h, s, d = q.shape
    nb = s // Q_BLOCK
    scale = d ** -0.5
    q_blocks = q.reshape(b, h, nb, Q_BLOCK, d).transpose(2, 0, 1, 3, 4)
    t_blocks = jnp.arange(s, dtype=jnp.int32).reshape(nb, Q_BLOCK)

    def block(args):
        qb, tb = args
        outs, lses = [], []
        for window, dil in DILATED_BRANCHES:
            n_keys = window // dil + 1
            idx = tb[:, None] - dil * jnp.arange(n_keys, dtype=jnp.int32)[None, :]
            valid = idx >= 0
            flat = jnp.maximum(idx, 0).reshape(-1)
            kg = jnp.take(k, flat, axis=2).reshape(b, h, Q_BLOCK, n_keys, d)
            vg = jnp.take(v, flat, axis=2).reshape(b, h, Q_BLOCK, n_keys, d)
            sc = jnp.einsum("bhqd,bhqkd->bhqk", qb, kg).astype(jnp.float32) * scale
            sc = jnp.where(valid, sc, -jnp.inf)
            lse = jax.nn.logsumexp(sc, axis=-1)
            p = jnp.exp(sc - lse[..., None])
            o = jnp.einsum("bhqk,bhqkd->bhqd", p.astype(v.dtype), vg)
            outs.append(o.astype(jnp.float32))
            lses.append(lse)
        wts = jax.nn.softmax(jnp.stack(lses, axis=0), axis=0)
        o = jnp.sum(wts[..., None] * jnp.stack(outs, axis=0), axis=0)
        return o.astype(q.dtype)

    o = lax.map(block, (q_blocks, t_blocks))
    return o.transpose(1, 2, 0, 3, 4).reshape(b, h, s, d)


def causal_depthwise_conv(u, w, bias):
    s = u.shape[1]
    up = jnp.pad(u, ((0, 0), (CONV_WIDTH - 1, 0), (0, 0)))
    out = bias
    for tap in range(CONV_WIDTH):
        out = out + up[:, tap:tap + s, :] * w[tap]
    return out


def ssd_scan(x, dt, a, b_mat, c_mat):
    bsz, s, h, p = x.shape
    g, n = b_mat.shape[-2:]
    r = h // g
    nc = s // CHUNK
    xc = (x * dt[..., None]).reshape(bsz, nc, CHUNK, g, r, p)
    a_dt = (dt * a).reshape(bsz, nc, CHUNK, g, r)
    bc = b_mat.reshape(bsz, nc, CHUNK, g, n)
    cc = c_mat.reshape(bsz, nc, CHUNK, g, n)
    a_cs = jnp.cumsum(a_dt, axis=2)
    a_cs_t = jnp.moveaxis(a_cs, 2, -1)
    causal = jnp.tril(jnp.ones((CHUNK, CHUNK), dtype=bool))
    seg = jnp.exp(jnp.where(causal, a_cs_t[..., :, None] - a_cs_t[..., None, :], -jnp.inf))
    cb = jnp.einsum("bclgn,bcsgn->bcgls", cc, bc)
    y_diag = jnp.einsum("bcgls,bcgrls,bcsgrp->bclgrp", cb, seg, xc)
    decay_to_end = jnp.exp(a_cs[:, :, -1:] - a_cs)
    chunk_states = jnp.einsum("bclgn,bclgr,bclgrp->bcgrpn", bc, decay_to_end, xc)
    chunk_decay = jnp.exp(a_cs[:, :, -1])

    def step(state, inp):
        cs, dec = inp
        return dec[..., None, None] * state + cs, state

    init = jnp.zeros((bsz, g, r, p, n), dtype=x.dtype)
    _, states_in = lax.scan(step, init, (jnp.moveaxis(chunk_states, 1, 0),
                                         jnp.moveaxis(chunk_decay, 1, 0)))
    states_in = jnp.moveaxis(states_in, 0, 1)
    y_off = jnp.einsum("bclgn,bcgrpn,bclgr->bclgrp", cc, states_in, jnp.exp(a_cs))
    return (y_diag + y_off).reshape(bsz, s, h, p)


def setup_inputs(seed: int = 0) -> dict:
    key = jax.random.key(seed)
    ks = jax.random.split(key, 16)
    f32 = jnp.float32
    x = jax.random.normal(ks[0], (BATCH, SEQ, D_MODEL), f32)
    attn_norm_w = 1.0 + 0.02 * jax.random.normal(ks[1], (DEPTH, D_MODEL), f32)
    w_in = jax.random.normal(ks[2], (DEPTH, D_MODEL, IN_PROJ_WIDTH), f32) * D_MODEL ** -0.5
    q_norm_w = 1.0 + 0.02 * jax.random.normal(ks[3], (DEPTH, HEAD_DIM), f32)
    k_norm_w = 1.0 + 0.02 * jax.random.normal(ks[4], (DEPTH, HEAD_DIM), f32)
    conv_w = jax.random.normal(ks[5], (DEPTH, CONV_WIDTH, CONV_CHANNELS), f32) * CONV_WIDTH ** -0.5
    conv_b = 0.01 * jax.random.normal(ks[6], (DEPTH, CONV_CHANNELS), f32)
    dt0 = jnp.exp(jax.random.uniform(ks[7], (DEPTH, N_SSD_HEADS), f32,
                                     minval=math_log(0.001), maxval=math_log(0.1)))
    dt_bias = dt0 + jnp.log(-jnp.expm1(-dt0))
    a_log = jnp.log(jax.random.uniform(ks[8], (DEPTH, N_SSD_HEADS), f32, minval=1.0, maxval=16.0))
    d_skip = 1.0 + 0.1 * jax.random.normal(ks[9], (DEPTH, N_SSD_HEADS), f32)
    ssd_norm_w = 1.0 + 0.02 * jax.random.normal(ks[10], (DEPTH, SSD_WIDTH), f32)
    w_out = jax.random.normal(ks[11], (DEPTH, MIX_WIDTH, D_MODEL), f32) * MIX_WIDTH ** -0.5
    mlp_norm_w = 1.0 + 0.02 * jax.random.normal(ks[12], (DEPTH, D_MODEL), f32)
    w_up = jax.random.normal(ks[13], (DEPTH, D_MODEL, D_FF), f32) * D_MODEL ** -0.5
    w_down = jax.random.normal(ks[14], (DEPTH, D_FF, D_MODEL), f32) * D_FF ** -0.5
    return {"x": x, "attn_norm_w": attn_norm_w, "w_in": w_in, "q_norm_w": q_norm_w,
            "k_norm_w": k_norm_w, "conv_w": conv_w, "conv_b": conv_b, "dt_bias": dt_bias,
            "a_log": a_log, "d_skip": d_skip, "ssd_norm_w": ssd_norm_w, "w_out": w_out,
            "mlp_norm_w": mlp_norm_w, "w_up": w_up, "w_down": w_down}


def math_log(v):
    return float(np.log(v))


def reference(x, attn_norm_w, w_in, q_norm_w, k_norm_w, conv_w, conv_b, dt_bias,
              a_log, d_skip, ssd_norm_w, w_out, mlp_norm_w, w_up, w_down):
    b, s, _ = x.shape
    splits = np.cumsum([ATTN_WIDTH, ATTN_WIDTH, ATTN_WIDTH, SSD_WIDTH, CONV_CHANNELS]).tolist()
    for i in range(DEPTH):
        h = rms_norm(x, attn_norm_w[i])
        proj = h @ w_in[i]
        q, k, v, z, xbc, dt_raw = jnp.split(proj, splits, axis=-1)

        q = rope(rms_norm(q.reshape(b, s, N_ATTN_HEADS, HEAD_DIM), q_norm_w[i]))
        k = rope(rms_norm(k.reshape(b, s, N_ATTN_HEADS, HEAD_DIM), k_norm_w[i]))
        v = v.reshape(b, s, N_ATTN_HEADS, HEAD_DIM)
        o_attn = dilated_attention(q.transpose(0, 2, 1, 3), k.transpose(0, 2, 1, 3),
                                   v.transpose(0, 2, 1, 3))
        o_attn = o_attn.transpose(0, 2, 1, 3).reshape(b, s, ATTN_WIDTH)

        xbc = jax.nn.silu(causal_depthwise_conv(xbc, conv_w[i], conv_b[i]))
        xs, bm, cm = jnp.split(xbc, [SSD_WIDTH, SSD_WIDTH + N_SSD_GROUPS * SSD_STATE], axis=-1)
        xs = xs.astype(jnp.float32).reshape(b, s, N_SSD_HEADS, SSD_HEAD_DIM)
        bm = bm.astype(jnp.float32).reshape(b, s, N_SSD_GROUPS, SSD_STATE)
        cm = cm.astype(jnp.float32).reshape(b, s, N_SSD_GROUPS, SSD_STATE)
        dt = jax.nn.softplus(dt_raw.astype(jnp.float32) + dt_bias[i].astype(jnp.float32))
        a = -jnp.exp(a_log[i].astype(jnp.float32))
        y = ssd_scan(xs, dt, a, bm, cm) + d_skip[i].astype(jnp.float32)[:, None] * xs
        y = y.reshape(b, s, SSD_WIDTH) * jax.nn.silu(z.astype(jnp.float32))
        y = rms_norm(y.reshape(b, s, N_SSD_GROUPS, SSD_WIDTH // N_SSD_GROUPS),
                     ssd_norm_w[i].reshape(N_SSD_GROUPS, SSD_WIDTH // N_SSD_GROUPS))
        y = y.reshape(b, s, SSD_WIDTH).astype(x.dtype)

        x = x + jnp.concatenate([o_attn, y], axis=-1) @ w_out[i]

        hm = rms_norm(x, mlp_norm_w[i])
        x = x + jnp.square(jax.nn.relu(hm @ w_up[i])) @ w_down[i]
    return x
```

```python
import functools

import jax
import jax.numpy as jnp
from jax import lax
from jax.experimental import pallas as pl
from jax.experimental.pallas import tpu as pltpu

F32 = jnp.float32
BF16 = jnp.bfloat16

HEAD_DIM = 64
N_ATTN_HEADS = 8
ATTN_WIDTH = N_ATTN_HEADS * HEAD_DIM
ROPE_THETA = 10000.0
Q_BLOCK = 128
DILATIONS = (1, 4, 16)
SSD_HEAD_DIM = 64
N_SSD_HEADS = 8
SSD_WIDTH = N_SSD_HEADS * SSD_HEAD_DIM
N_SSD_GROUPS = 2
HEADS_PER_GROUP = N_SSD_HEADS // N_SSD_GROUPS
SSD_STATE = 128
CONV_WIDTH = 4
CHUNK = 128
CONV_CHANNELS = SSD_WIDTH + 2 * N_SSD_GROUPS * SSD_STATE
EPS = 1e-6
NEG_BIG = -1e30

LANES = 128
DT_ROWS = 16
MAIN_WIDTH = 3 * ATTN_WIDTH + SSD_WIDTH + CONV_CHANNELS
VMEM_LIMIT = 56 * 1024 * 1024


def _dot(a, b):
    return jnp.dot(a, b, preferred_element_type=F32)


def _dot_nt(a, b):
    return lax.dot_general(a, b, (((1,), (1,)), ((), ())), preferred_element_type=F32)


def _split_dot(a, b, parts, lhs_split=True):
    src = a if lhs_split else b
    acc = None
    rem = src
    for _ in range(parts):
        piece = rem.astype(BF16)
        rem = rem - piece.astype(F32)
        term = _dot(piece, b) if lhs_split else _dot(a, piece)
        acc = term if acc is None else acc + term
    return acc


def _resident(shape):
    zeros = (0,) * len(shape)
    return pl.BlockSpec(shape, lambda *_: zeros, pipeline_mode=pl.Buffered(1))


def _in_proj_kernel(x_ref, nw_ref, w_ref, wdt_ref, wdtt_ref, gm_ref, qw_ref, kw_ref,
                    cos_ref, sin_ref,
                    q_ref, k_ref, v_ref, z_ref, xbc_ref, dtc_ref, dtr_ref):
    x = x_ref[...]
    h = x * lax.rsqrt(jnp.mean(x * x, axis=-1, keepdims=True) + EPS) * nw_ref[...]
    hb = h.astype(BF16)

    lane = lax.broadcasted_iota(jnp.int32, (x.shape[0], ATTN_WIDTH), 1)
    first_half = (lane % HEAD_DIM) < (HEAD_DIM // 2)
    cos = cos_ref[...]
    sin = sin_ref[...]

    def head_norm_rope(t, w):
        ms = _split_dot(t * t, gm_ref[...], 2)
        n = t * lax.rsqrt(ms + EPS) * w
        partner = jnp.where(first_half,
                            pltpu.roll(n, ATTN_WIDTH - HEAD_DIM // 2, 1),
                            pltpu.roll(n, HEAD_DIM // 2, 1))
        return n * cos + partner * sin

    o = 0
    q = _dot(hb, w_ref[:, o:o + ATTN_WIDTH]); o += ATTN_WIDTH
    q_ref[...] = (head_norm_rope(q, qw_ref[...]) * (HEAD_DIM ** -0.5)).astype(BF16)
    k = _dot(hb, w_ref[:, o:o + ATTN_WIDTH]); o += ATTN_WIDTH
    k_ref[...] = head_norm_rope(k, kw_ref[...]).astype(BF16)
    v_ref[...] = _dot(hb, w_ref[:, o:o + ATTN_WIDTH]).astype(BF16); o += ATTN_WIDTH
    z_ref[...] = _dot(hb, w_ref[:, o:o + SSD_WIDTH]); o += SSD_WIDTH
    xbc_ref[...] = _dot(hb, w_ref[:, o:o + CONV_CHANNELS])
    dtc_ref[...] = _dot(hb, wdt_ref[...])
    dtr_ref[...] = _dot_nt(wdtt_ref[...], hb)


def _in_proj(x, attn_norm_w, w_main, w_dt, w_dt_t, group_mean, qw, kw, cos, sin, tm):
    b, s, d = x.shape
    grid = (s // tm, b)
    tok = lambda width: pl.BlockSpec((None, tm, width), lambda j, i: (i, j, 0))
    out_shapes = (
        jax.ShapeDtypeStruct((b, s, ATTN_WIDTH), BF16),
        jax.ShapeDtypeStruct((b, s, ATTN_WIDTH), BF16),
        jax.ShapeDtypeStruct((b, s, ATTN_WIDTH), BF16),
        jax.ShapeDtypeStruct((b, s, SSD_WIDTH), F32),
        jax.ShapeDtypeStruct((b, s, CONV_CHANNELS), F32),
        jax.ShapeDtypeStruct((b, s, LANES), F32),
        jax.ShapeDtypeStruct((b, DT_ROWS, s), F32),
    )
    return pl.pallas_call(
        _in_proj_kernel,
        grid=grid,
        in_specs=[
            tok(d),
            _resident((1, d)),
            _resident(w_main.shape),
            _resident(w_dt.shape),
            _resident(w_dt_t.shape),
            _resident(group_mean.shape),
            _resident((1, ATTN_WIDTH)),
            _resident((1, ATTN_WIDTH)),
            pl.BlockSpec((tm, ATTN_WIDTH), lambda j, i: (j, 0)),
            pl.BlockSpec((tm, ATTN_WIDTH), lambda j, i: (j, 0)),
        ],
        out_specs=(
            tok(ATTN_WIDTH), tok(ATTN_WIDTH), tok(ATTN_WIDTH), tok(SSD_WIDTH),
            tok(CONV_CHANNELS), tok(LANES),
            pl.BlockSpec((None, DT_ROWS, tm), lambda j, i: (i, 0, j)),
        ),
        out_shape=out_shapes,
        compiler_params=pltpu.CompilerParams(
            dimension_semantics=("parallel", "parallel"), vmem_limit_bytes=VMEM_LIMIT),
        name="in_proj",
    )(x, attn_norm_w, w_main, w_dt, w_dt_t, group_mean, qw, kw, cos, sin)


def _attn_kernel(q_ref, k_ref, v_ref, o_ref, qf, kf, vf, acc, mrun, lrun):
    s = q_ref.shape[0]
    qb = Q_BLOCK
    qf[...] = q_ref[...].astype(F32)
    kf[...] = k_ref[...].astype(F32)
    vf[...] = v_ref[...].astype(F32)

    head0 = lax.broadcasted_iota(jnp.int32, (qb, LANES), 1) < HEAD_DIM
    row2 = lax.broadcasted_iota(jnp.int32, (qb, 2 * qb), 0)
    col2 = lax.broadcasted_iota(jnp.int32, (qb, 2 * qb), 1)
    band_ok = (col2 >= row2) & (col2 <= row2 + qb)
    row1 = lax.broadcasted_iota(jnp.int32, (qb, qb), 0)
    col1 = lax.broadcasted_iota(jnp.int32, (qb, qb), 1)
    causal_ok = col1 <= row1

    def tile(q_rows, kv_rows, ok):
        qt = qf[q_rows, :]
        kb = kf[kv_rows, :].astype(BF16)
        vb = vf[kv_rows, :].astype(BF16)
        per_head = []
        for h in range(2):
            mine = head0 if h == 0 else jnp.logical_not(head0)
            qh = jnp.where(mine, qt, 0.0).astype(BF16)
            sc = jnp.where(ok, _dot_nt(qh, kb), NEG_BIG)
            m = jnp.max(sc, axis=1, keepdims=True)
            p = jnp.exp(sc - m)
            l = jnp.sum(p, axis=1, keepdims=True)
            per_head.append((_dot(p.astype(BF16), vb), m, l))
        (o0, m0, l0), (o1, m1, l1) = per_head
        return (jnp.where(head0, o0, o1),
                jnp.where(head0, m0, m1),
                jnp.where(head0, l0, l1))

    def store_first(rows, o, m, l):
        acc[rows, :] = o
        mrun[rows, :] = m
        lrun[rows, :] = l

    def merge(rows, o, m, l):
        m_old = mrun[rows, :]
        m_new = jnp.maximum(m_old, m)
        a_old = jnp.exp(m_old - m_new)
        a_new = jnp.exp(m - m_new)
        acc[rows, :] = a_old * acc[rows, :] + a_new * o
        lrun[rows, :] = a_old * lrun[rows, :] + a_new * l
        mrun[rows, :] = m_new

    for dil in DILATIONS:
        sink = store_first if dil == DILATIONS[0] else merge
        span = qb * dil
        n_blocks = s // span

        def first_block(r, dil=dil, sink=sink):
            rows = pl.ds(r, qb, stride=dil) if dil > 1 else pl.ds(r, qb)
            sink(rows, *tile(rows, rows, causal_ok))

        def later_block(j, r, dil=dil, span=span, sink=sink):
            start = r + j * span
            if dil > 1:
                rows = pl.ds(start, qb, stride=dil)
                kv_rows = pl.ds(start - span, 2 * qb, stride=dil)
            else:
                start = pl.multiple_of(start, qb)
                rows = pl.ds(start, qb)
                kv_rows = pl.ds(start - span, 2 * qb)
            sink(rows, *tile(rows, kv_rows, band_ok))

        def residue(r, carry, n_blocks=n_blocks, first_block=first_block,
                    later_block=later_block):
            first_block(r)
            if n_blocks > 1:
                lax.fori_loop(1, n_blocks,
                              lambda j, c: (later_block(j, r), c)[1], 0)
            return carry

        if dil == 1:
            residue(0, 0)
        else:
            lax.fori_loop(0, dil, residue, 0)

    o_ref[...] = (acc[...] / lrun[...]).astype(o_ref.dtype)


def _attention(q, k, v):
    b, s, w = q.shape
    blk = pl.BlockSpec((None, s, LANES), lambda i, j: (i, 0, j))
    return pl.pallas_call(
        _attn_kernel,
        grid=(b, w // LANES),
        in_specs=[blk, blk, blk],
        out_specs=blk,
        out_shape=jax.ShapeDtypeStruct((b, s, w), BF16),
        scratch_shapes=[pltpu.VMEM((s, LANES), F32) for _ in range(6)],
        compiler_params=pltpu.CompilerParams(
            dimension_semantics=("parallel", "parallel"), vmem_limit_bytes=VMEM_LIMIT),
        name="dilated_attn",
    )(q, k, v)


def _softplus(x):
    return jnp.maximum(x, 0.0) + jnp.log1p(jnp.exp(-jnp.abs(x)))


def _silu(x):
    return x * (1.0 / (1.0 + jnp.exp(-x)))


def _ssd_kernel(xbc_ref, z_ref, dtc_ref, dtr_ref, cw_ref, cb_ref, dtb_row_ref, dtb_col_ref,
                alog_row_ref, alog_col_ref, dskip_ref, nw_ref, y_ref, ubuf, state):
    L = CHUNK
    pad = 8
    c = pl.program_id(1)

    @pl.when(c == 0)
    def _():
        ubuf[0:pad, :] = jnp.zeros((pad, CONV_CHANNELS), F32)
        state[...] = jnp.zeros(state.shape, F32)

    u = xbc_ref[...]
    ubuf[pad:pad + L, :] = u
    conv = cb_ref[...] + cw_ref[CONV_WIDTH - 1:CONV_WIDTH, :] * u
    for tap in range(CONV_WIDTH - 1):
        shift = CONV_WIDTH - 1 - tap
        conv = conv + cw_ref[tap:tap + 1, :] * ubuf[pad - shift:pad - shift + L, :]
    ubuf[0:pad, :] = u[L - pad:L, :]
    act = _silu(conv)
    xs = act[:, :SSD_WIDTH]
    gs = N_SSD_GROUPS * SSD_STATE
    bm = act[:, SSD_WIDTH:SSD_WIDTH + gs]
    cm = act[:, SSD_WIDTH + gs:SSD_WIDTH + 2 * gs]

    dt_c = _softplus(dtc_ref[...] + dtb_row_ref[...])
    dt_r = _softplus(dtr_ref[...] + dtb_col_ref[...])
    a_row = -jnp.exp(alog_row_ref[...])
    a_col = -jnp.exp(alog_col_ref[...])
    ri = lax.broadcasted_iota(jnp.int32, (L, L), 0)
    ci = lax.broadcasted_iota(jnp.int32, (L, L), 1)
    causal = ci <= ri
    tril = jnp.where(causal, 1.0, 0.0).astype(BF16)
    triu = jnp.where(ri <= ci, 1.0, 0.0).astype(BF16)
    acs_c = _split_dot(tril, dt_c * a_row, 3, lhs_split=False)
    acs_r = _split_dot(dt_r * a_col, triu, 3)
    e_c = jnp.exp(acs_c)

    ys = []
    for g in range(N_SSD_GROUPS):
        b_g = bm[:, g * SSD_STATE:(g + 1) * SSD_STATE]
        c_g = cm[:, g * SSD_STATE:(g + 1) * SSD_STATE]
        c_gb = c_g.astype(BF16)
        cb = _dot_nt(c_gb, b_g.astype(BF16))
        b_gt = b_g.T
        for r in range(HEADS_PER_GROUP):
            h = g * HEADS_PER_GROUP + r
            xs_h = xs[:, h * SSD_HEAD_DIM:(h + 1) * SSD_HEAD_DIM]
            xs_hb = xs_h.astype(BF16)
            acs_col = acs_c[:, h:h + 1]
            acs_row = acs_r[h:h + 1, :]
            dt_row = dt_r[h:h + 1, :]
            seg = jnp.exp(jnp.where(causal, acs_col - acs_row, NEG_BIG))
            w_diag = (cb * seg * dt_row).astype(BF16)
            st = state[h]
            y_h = (_dot(w_diag, xs_hb)
                   + e_c[:, h:h + 1] * _dot(c_gb, st.astype(BF16))
                   + dskip_ref[:, h * SSD_HEAD_DIM:(h + 1) * SSD_HEAD_DIM] * xs_h)
            acs_last = acs_row[:, L - 1:L]
            w_state = jnp.exp(acs_last - acs_row) * dt_row
            state[h] = jnp.exp(acs_last) * st + _dot((b_gt * w_state).astype(BF16), xs_hb)
            ys.append(y_h)
    y = jnp.concatenate(ys, axis=1) * _silu(z_ref[...])

    gw = SSD_WIDTH // N_SSD_GROUPS
    outs = []
    for g in range(N_SSD_GROUPS):
        yg = y[:, g * gw:(g + 1) * gw]
        outs.append(yg * lax.rsqrt(jnp.mean(yg * yg, axis=-1, keepdims=True) + EPS))
    y_ref[...] = (jnp.concatenate(outs, axis=1) * nw_ref[...]).astype(y_ref.dtype)


def _ssd(xbc, z, dtc, dtr, conv_w, conv_b, dtb_row, dtb_col, alog_row, alog_col, dskip, nw):
    b, s, _ = xbc.shape
    nc = s // CHUNK
    tok = lambda width: pl.BlockSpec((None, CHUNK, width), lambda i, c: (i, c, 0))
    return pl.pallas_call(
        _ssd_kernel,
        grid=(b, nc),
        in_specs=[
            tok(CONV_CHANNELS), tok(SSD_WIDTH), tok(LANES),
            pl.BlockSpec((None, DT_ROWS, CHUNK), lambda i, c: (i, 0, c)),
            _resident(conv_w.shape), _resident(conv_b.shape),
            _resident(dtb_row.shape), _resident(dtb_col.shape),
            _resident(alog_row.shape), _resident(alog_col.shape),
            _resident(dskip.shape), _resident(nw.shape),
        ],
        out_specs=tok(SSD_WIDTH),
        out_shape=jax.ShapeDtypeStruct((b, s, SSD_WIDTH), BF16),
        scratch_shapes=[
            pltpu.VMEM((8 + CHUNK, CONV_CHANNELS), F32),
            pltpu.VMEM((N_SSD_HEADS, SSD_STATE, SSD_HEAD_DIM), F32),
        ],
        compiler_params=pltpu.CompilerParams(
            dimension_semantics=("parallel", "arbitrary"), vmem_limit_bytes=VMEM_LIMIT),
        name="ssd",
    )(xbc, z, dtc, dtr, conv_w, conv_b, dtb_row, dtb_col, alog_row, alog_col, dskip, nw)


def _out_mlp_kernel(x_ref, oa_ref, ys_ref, woa_ref, wos_ref, nw_ref, wup_ref, wdn_ref, o_ref):
    x1 = x_ref[...] + _dot(oa_ref[...], woa_ref[...]) + _dot(ys_ref[...], wos_ref[...])
    hm = x1 * lax.rsqrt(jnp.mean(x1 * x1, axis=-1, keepdims=True) + EPS) * nw_ref[...]
    up = _dot(hm.astype(BF16), wup_ref[...])
    act = jnp.square(jnp.maximum(up, 0.0)).astype(BF16)
    o_ref[...] = x1 + _dot(act, wdn_ref[...])


def _out_mlp(x, o_attn, y_ssd, wo_a, wo_s, mlp_norm_w, w_up, w_down, tm):
    t, d = x.shape
    tok = lambda width: pl.BlockSpec((tm, width), lambda i: (i, 0))
    return pl.pallas_call(
        _out_mlp_kernel,
        grid=(t // tm,),
        in_specs=[
            tok(d), tok(ATTN_WIDTH), tok(SSD_WIDTH),
            _resident(wo_a.shape), _resident(wo_s.shape), _resident((1, d)),
            _resident(w_up.shape), _resident(w_down.shape),
        ],
        out_specs=tok(d),
        out_shape=jax.ShapeDtypeStruct((t, d), F32),
        compiler_params=pltpu.CompilerParams(
            dimension_semantics=("parallel",), vmem_limit_bytes=VMEM_LIMIT),
        name="out_mlp",
    )(x, o_attn, y_ssd, wo_a, wo_s, mlp_norm_w, w_up, w_down)


def _rope_tables(s):
    half = HEAD_DIM // 2
    inv_freq = ROPE_THETA ** (-jnp.arange(half, dtype=F32) / half)
    ang = jnp.arange(s, dtype=F32)[:, None] * inv_freq[None, :]
    cos, sin = jnp.cos(ang), jnp.sin(ang)
    cos_h = jnp.concatenate([cos, cos], axis=1)
    sin_h = jnp.concatenate([-sin, sin], axis=1)
    return jnp.tile(cos_h, (1, N_ATTN_HEADS)), jnp.tile(sin_h, (1, N_ATTN_HEADS))


def _pad_lanes(row):
    return jnp.pad(row, ((0, 0), (0, LANES - row.shape[1])))


def _layer(x, attn_norm_w, w_in, q_norm_w, k_norm_w, conv_w, conv_b, dt_bias, a_log, d_skip,
           ssd_norm_w, w_out, mlp_norm_w, w_up, w_down):
    b, s, d = x.shape
    w_main = w_in[:, :MAIN_WIDTH].astype(BF16)
    w_dt = w_in[:, MAIN_WIDTH:]
    w_dt_c = jnp.pad(w_dt, ((0, 0), (0, LANES - N_SSD_HEADS))).astype(BF16)
    w_dt_r = jnp.pad(w_dt.T, ((0, DT_ROWS - N_SSD_HEADS), (0, 0))).astype(BF16)
    head_of = jnp.arange(ATTN_WIDTH, dtype=jnp.int32) // HEAD_DIM
    group_mean = jnp.where(head_of[:, None] == head_of[None, :], 1.0 / HEAD_DIM, 0.0).astype(BF16)
    cos, sin = _rope_tables(s)
    q, k, v, z, xbc, dtc, dtr = _in_proj(
        x, attn_norm_w[None, :], w_main, w_dt_c, w_dt_r, group_mean,
        jnp.tile(q_norm_w, N_ATTN_HEADS)[None, :], jnp.tile(k_norm_w, N_ATTN_HEADS)[None, :],
        cos, sin, tm=256)

    o_attn = _attention(q, k, v)

    col = lambda p: jnp.pad(p[:, None], ((0, DT_ROWS - N_SSD_HEADS), (0, 0)))
    y_ssd = _ssd(xbc, z, dtc, dtr, conv_w, conv_b[None, :],
                 _pad_lanes(dt_bias[None, :]), col(dt_bias),
                 _pad_lanes(a_log[None, :]), col(a_log),
                 jnp.repeat(d_skip, SSD_HEAD_DIM)[None, :], ssd_norm_w[None, :])

    out = _out_mlp(x.reshape(b * s, d), o_attn.reshape(b * s, ATTN_WIDTH),
                   y_ssd.reshape(b * s, SSD_WIDTH),
                   w_out[:ATTN_WIDTH].astype(BF16), w_out[ATTN_WIDTH:].astype(BF16),
                   mlp_norm_w[None, :], w_up.astype(BF16), w_down.astype(BF16), tm=256)
    return out.reshape(b, s, d)


def kernel(x, attn_norm_w, w_in, q_norm_w, k_norm_w, conv_w, conv_b, dt_bias, a_log, d_skip,
           ssd_norm_w, w_out, mlp_norm_w, w_up, w_down):
    for i in range(attn_norm_w.shape[0]):
        x = _layer(x, attn_norm_w[i], w_in[i], q_norm_w[i], k_norm_w[i], conv_w[i], conv_b[i],
                   dt_bias[i], a_log[i], d_skip[i], ssd_norm_w[i], w_out[i], mlp_norm_w[i],
                   w_up[i], w_down[i])
    return x
```

```python
import functools

import jax
import jax.numpy as jnp
from jax import lax
from jax.experimental import pallas as pl
from jax.experimental.pallas import tpu as pltpu

F32 = jnp.float32
BF16 = jnp.bfloat16

HEAD_DIM = 64
N_ATTN_HEADS = 8
ATTN_WIDTH = N_ATTN_HEADS * HEAD_DIM
ROPE_THETA = 10000.0
Q_BLOCK = 128
DILATIONS = (1, 4, 16)
ATTN_BLOCKS_PER_STEP = 4
SSD_HEAD_DIM = 64
N_SSD_HEADS = 8
SSD_WIDTH = N_SSD_HEADS * SSD_HEAD_DIM
N_SSD_GROUPS = 2
HEADS_PER_GROUP = N_SSD_HEADS // N_SSD_GROUPS
SSD_STATE = 128
CONV_WIDTH = 4
CHUNK = 128
CONV_CHANNELS = SSD_WIDTH + 2 * N_SSD_GROUPS * SSD_STATE
EPS = 1e-6
NEG_BIG = -1e30

LANES = 128
DT_ROWS = 16
MAIN_WIDTH = 3 * ATTN_WIDTH + SSD_WIDTH + CONV_CHANNELS
VMEM_LIMIT = 56 * 1024 * 1024


def _dot(a, b):
    return jnp.dot(a, b, preferred_element_type=F32)


def _dot_nt(a, b):
    return lax.dot_general(a, b, (((1,), (1,)), ((), ())), preferred_element_type=F32)


def _split_dot(a, b, parts, lhs_split=True):
    src = a if lhs_split else b
    acc = None
    rem = src
    for _ in range(parts):
        piece = rem.astype(BF16)
        rem = rem - piece.astype(F32)
        term = _dot(piece, b) if lhs_split else _dot(a, piece)
        acc = term if acc is None else acc + term
    return acc


def _resident(shape):
    zeros = (0,) * len(shape)
    return pl.BlockSpec(shape, lambda *_: zeros, pipeline_mode=pl.Buffered(1))


def _in_proj_kernel(x_ref, nw_ref, w_ref, wdt_ref, wdtt_ref, gm_ref, qw_ref, kw_ref,
                    cos_ref, sin_ref,
                    q_ref, k_ref, v_ref, z_ref, xbc_ref, dtc_ref, dtr_ref):
    x = x_ref[...]
    h = x * lax.rsqrt(jnp.mean(x * x, axis=-1, keepdims=True) + EPS) * nw_ref[...]
    hb = h.astype(BF16)

    lane = lax.broadcasted_iota(jnp.int32, (x.shape[0], ATTN_WIDTH), 1)
    first_half = (lane % HEAD_DIM) < (HEAD_DIM // 2)
    cos = cos_ref[...]
    sin = sin_ref[...]

    def head_norm_rope(t, w):
        ms = _split_dot(t * t, gm_ref[...], 2)
        n = t * lax.rsqrt(ms + EPS) * w
        partner = jnp.where(first_half,
                            pltpu.roll(n, ATTN_WIDTH - HEAD_DIM // 2, 1),
                            pltpu.roll(n, HEAD_DIM // 2, 1))
        return n * cos + partner * sin

    o = 0
    q = _dot(hb, w_ref[:, o:o + ATTN_WIDTH]); o += ATTN_WIDTH
    q_ref[...] = (head_norm_rope(q, qw_ref[...]) * (HEAD_DIM ** -0.5)).astype(BF16)
    k = _dot(hb, w_ref[:, o:o + ATTN_WIDTH]); o += ATTN_WIDTH
    k_ref[...] = head_norm_rope(k, kw_ref[...]).astype(BF16)
    v_ref[...] = _dot(hb, w_ref[:, o:o + ATTN_WIDTH]).astype(BF16); o += ATTN_WIDTH
    z_ref[...] = _dot(hb, w_ref[:, o:o + SSD_WIDTH]); o += SSD_WIDTH
    xbc_ref[...] = _dot(hb, w_ref[:, o:o + CONV_CHANNELS])
    dtc_ref[...] = _dot(hb, wdt_ref[...])
    dtr_ref[...] = _dot_nt(wdtt_ref[...], hb)


def _in_proj(x, attn_norm_w, w_main, w_dt, w_dt_t, group_mean, qw, kw, cos, sin, tm):
    b, s, d = x.shape
    grid = (s // tm, b)
    tok = lambda width: pl.BlockSpec((None, tm, width), lambda j, i: (i, j, 0))
    out_shapes = (
        jax.ShapeDtypeStruct((b, s, ATTN_WIDTH), BF16),
        jax.ShapeDtypeStruct((b, s, ATTN_WIDTH), BF16),
        jax.ShapeDtypeStruct((b, s, ATTN_WIDTH), BF16),
        jax.ShapeDtypeStruct((b, s, SSD_WIDTH), F32),
        jax.ShapeDtypeStruct((b, s, CONV_CHANNELS), F32),
        jax.ShapeDtypeStruct((b, s, LANES), F32),
        jax.ShapeDtypeStruct((b, DT_ROWS, s), F32),
    )
    return pl.pallas_call(
        _in_proj_kernel,
        grid=grid,
        in_specs=[
            tok(d),
            _resident((1, d)),
            _resident(w_main.shape),
            _resident(w_dt.shape),
            _resident(w_dt_t.shape),
            _resident(group_mean.shape),
            _resident((1, ATTN_WIDTH)),
            _resident((1, ATTN_WIDTH)),
            pl.BlockSpec((tm, ATTN_WIDTH), lambda j, i: (j, 0)),
            pl.BlockSpec((tm, ATTN_WIDTH), lambda j, i: (j, 0)),
        ],
        out_specs=(
            tok(ATTN_WIDTH), tok(ATTN_WIDTH), tok(ATTN_WIDTH), tok(SSD_WIDTH),
            tok(CONV_CHANNELS), tok(LANES),
            pl.BlockSpec((None, DT_ROWS, tm), lambda j, i: (i, 0, j)),
        ),
        out_shape=out_shapes,
        compiler_params=pltpu.CompilerParams(
            dimension_semantics=("parallel", "parallel"), vmem_limit_bytes=VMEM_LIMIT),
        name="in_proj",
    )(x, attn_norm_w, w_main, w_dt, w_dt_t, group_mean, qw, kw, cos, sin)


def _attn_blocks(blocks, lse_lane0, store):
    qb = Q_BLOCK
    n_pairs = ATTN_WIDTH // LANES
    lane = lax.broadcasted_iota(jnp.int32, (qb, LANES), 1)
    head0 = lane < HEAD_DIM
    items = [(i, pair) for i in range(len(blocks)) for pair in range(n_pairs)]

    def scores(item):
        i, pair = item
        load_q, load_k, _, load_bias = blocks[i]
        qt, kb, bias = load_q(pair), load_k(pair), load_bias()
        out = []
        for h in range(2):
            mine = head0 if h == 0 else jnp.logical_not(head0)
            qh = jnp.where(mine, qt, jnp.zeros_like(qt))
            out.append(_dot_nt(qh, kb) + bias)
        return out

    pending = scores(items[0])
    outs, lse_tile = [], jnp.zeros((qb, LANES), F32)
    for n, (i, pair) in enumerate(items):
        upcoming = scores(items[n + 1]) if n + 1 < len(items) else None
        vb = blocks[i][2](pair)
        v_ext = jnp.concatenate([vb, jnp.ones_like(vb)], axis=1)
        o_pair = None
        for h, sc in enumerate(pending):
            m = jnp.max(sc, axis=1, keepdims=True)
            p = jnp.exp(sc - m).astype(BF16)
            r = _dot(p, v_ext)
            l = r[:, LANES:]
            o_h = r[:, :LANES] / l
            o_pair = o_h if o_pair is None else jnp.where(head0, o_pair, o_h)
            lse_tile = jnp.where(lane == lse_lane0 + 2 * pair + h, m + jnp.log(l), lse_tile)
        outs.append(o_pair.astype(BF16))
        if pair == n_pairs - 1:
            store(i, jnp.concatenate(outs, axis=1), lse_tile)
            outs, lse_tile = [], jnp.zeros((qb, LANES), F32)
        pending = upcoming


def _band_bias(first):
    qb = Q_BLOCK
    row = lax.broadcasted_iota(jnp.int32, (qb, 2 * qb), 0)
    col = lax.broadcasted_iota(jnp.int32, (qb, 2 * qb), 1)
    ok = (col <= row) if first else ((col >= row) & (col <= row + qb))
    return jnp.where(ok, 0.0, NEG_BIG)


def _causal_bias():
    qb = Q_BLOCK
    row = lax.broadcasted_iota(jnp.int32, (qb, qb), 0)
    col = lax.broadcasted_iota(jnp.int32, (qb, qb), 1)
    return jnp.where(col <= row, 0.0, NEG_BIG)


def _attn_class_kernel(q_ref, k_ref, v_ref, o_ref, lse_ref, bias_ref, *, n_classes, lse_lane0):
    qb = Q_BLOCK
    n_blocks = q_ref.shape[0] // qb
    bias_ref[:, :qb] = _causal_bias()
    if n_blocks > 1:
        bias_ref[:, qb:] = _band_bias(False)

    def block(c, j):
        q_rows = pl.ds(j * qb, qb)
        lanes = lambda pair: pl.ds(c * ATTN_WIDTH + pair * LANES, LANES)
        if j == 0:
            kv_rows, load_bias = q_rows, lambda: bias_ref[:, :qb]
        else:
            kv_rows, load_bias = pl.ds((j - 1) * qb, 2 * qb), lambda: bias_ref[:, qb:]
        return (lambda p: q_ref[q_rows, lanes(p)], lambda p: k_ref[kv_rows, lanes(p)],
                lambda p: v_ref[kv_rows, lanes(p)], load_bias)

    where = [(c, j) for c in range(n_classes) for j in range(n_blocks)]

    def store(i, o, lse):
        c, j = where[i]
        o_ref[pl.ds(j * qb, qb), pl.ds(c * ATTN_WIDTH, ATTN_WIDTH)] = o
        lse_ref[pl.ds(j * qb, qb), pl.ds(c * LANES, LANES)] = lse

    _attn_blocks([block(c, j) for c, j in where], lse_lane0, store)


def _attn_dense_kernel(q_ref, k_ref, v_ref, o_ref, lse_ref, bias_ref, *, lse_lane0):
    qb = Q_BLOCK
    g = pl.program_id(1)
    n_blocks = q_ref.shape[0] // qb
    bias_ref[0] = _band_bias(False)
    bias_ref[1] = _band_bias(True)
    is_first = (g == 0).astype(jnp.int32)

    def block(j):
        q_rows = pl.ds(j * qb, qb)
        lanes = lambda pair: pl.ds(pair * LANES, LANES)
        if j == 0:
            start = pl.multiple_of((g * n_blocks - 1 + is_first) * qb, qb)
            load_bias = lambda: bias_ref[is_first]
        else:
            start = pl.multiple_of((g * n_blocks + j - 1) * qb, qb)
            load_bias = lambda: bias_ref[0]
        kv_rows = pl.ds(start, 2 * qb)
        return (lambda p: q_ref[q_rows, lanes(p)], lambda p: k_ref[kv_rows, lanes(p)],
                lambda p: v_ref[kv_rows, lanes(p)], load_bias)

    def store(j, o, lse):
        o_ref[pl.ds(j * qb, qb), :] = o
        lse_ref[pl.ds(j * qb, qb), :] = lse

    _attn_blocks([block(j) for j in range(n_blocks)], lse_lane0, store)


def _attention_branch(q, k, v, dil, lse_lane0):
    b, s, w = q.shape
    params = pltpu.CompilerParams(
        dimension_semantics=("parallel", "parallel"), vmem_limit_bytes=VMEM_LIMIT)
    if dil == 1:
        rows = ATTN_BLOCKS_PER_STEP * Q_BLOCK
        q_blk = lambda width: pl.BlockSpec((None, rows, width), lambda i, g: (i, g, 0))
        kv_blk = pl.BlockSpec((None, s, w), lambda i, g: (i, 0, 0))
        return pl.pallas_call(
            functools.partial(_attn_dense_kernel, lse_lane0=lse_lane0),
            grid=(b, s // rows),
            in_specs=[q_blk(w), kv_blk, kv_blk],
            out_specs=(q_blk(w), q_blk(LANES)),
            out_shape=(jax.ShapeDtypeStruct((b, s, w), BF16),
                       jax.ShapeDtypeStruct((b, s, LANES), F32)),
            scratch_shapes=[pltpu.VMEM((2, Q_BLOCK, 2 * Q_BLOCK), F32)],
            compiler_params=params,
            name="attn_dil1",
        )(q, k, v)
    class_len = s // dil
    n_classes = max(1, ATTN_BLOCKS_PER_STEP * Q_BLOCK // class_len)
    view = lambda t: t.reshape(b, class_len, dil * t.shape[-1])
    blk = lambda width: pl.BlockSpec((None, class_len, n_classes * width), lambda i, g: (i, 0, g))
    o, lse = pl.pallas_call(
        functools.partial(_attn_class_kernel, n_classes=n_classes, lse_lane0=lse_lane0),
        grid=(b, dil // n_classes),
        in_specs=[blk(w), blk(w), blk(w)],
        out_specs=(blk(w), blk(LANES)),
        out_shape=(jax.ShapeDtypeStruct((b, class_len, dil * w), BF16),
                   jax.ShapeDtypeStruct((b, class_len, dil * LANES), F32)),
        scratch_shapes=[pltpu.VMEM((Q_BLOCK, 3 * Q_BLOCK), F32)],
        compiler_params=params,
        name=f"attn_dil{dil}",
    )(view(q), view(k), view(v))
    return o.reshape(b, s, w), lse.reshape(b, s, LANES)


def _softplus(x):
    return jnp.maximum(x, 0.0) + jnp.log1p(jnp.exp(-jnp.abs(x)))


def _silu(x):
    return x * (1.0 / (1.0 + jnp.exp(-x)))


def _ssd_kernel(xbc_ref, z_ref, dtc_ref, dtr_ref, cw_ref, cb_ref, dtb_row_ref, dtb_col_ref,
                alog_row_ref, alog_col_ref, dskip_ref, nw_ref, y_ref, ubuf, state):
    L = CHUNK
    pad = 8
    c = pl.program_id(1)

    @pl.when(c == 0)
    def _():
        ubuf[0:pad, :] = jnp.zeros((pad, CONV_CHANNELS), F32)
        state[...] = jnp.zeros(state.shape, F32)

    u = xbc_ref[...]
    ubuf[pad:pad + L, :] = u
    conv = cb_ref[...] + cw_ref[CONV_WIDTH - 1:CONV_WIDTH, :] * u
    for tap in range(CONV_WIDTH - 1):
        shift = CONV_WIDTH - 1 - tap
        conv = conv + cw_ref[tap:tap + 1, :] * ubuf[pad - shift:pad - shift + L, :]
    ubuf[0:pad, :] = u[L - pad:L, :]
    act = _silu(conv)
    xs = act[:, :SSD_WIDTH]
    gs = N_SSD_GROUPS * SSD_STATE
    bm = act[:, SSD_WIDTH:SSD_WIDTH + gs]
    cm = act[:, SSD_WIDTH + gs:SSD_WIDTH + 2 * gs]

    dt_c = _softplus(dtc_ref[...] + dtb_row_ref[...])
    dt_r = _softplus(dtr_ref[...] + dtb_col_ref[...])
    a_row = -jnp.exp(alog_row_ref[...])
    a_col = -jnp.exp(alog_col_ref[...])
    ri = lax.broadcasted_iota(jnp.int32, (L, L), 0)
    ci = lax.broadcasted_iota(jnp.int32, (L, L), 1)
    causal = ci <= ri
    tril = jnp.where(causal, 1.0, 0.0).astype(BF16)
    triu = jnp.where(ri <= ci, 1.0, 0.0).astype(BF16)
    acs_c = _split_dot(tril, dt_c * a_row, 3, lhs_split=False)
    acs_r = _split_dot(dt_r * a_col, triu, 3)
    e_c = jnp.exp(acs_c)

    ys = []
    for g in range(N_SSD_GROUPS):
        b_g = bm[:, g * SSD_STATE:(g + 1) * SSD_STATE]
        c_g = cm[:, g * SSD_STATE:(g + 1) * SSD_STATE]
        c_gb = c_g.astype(BF16)
        cb = _dot_nt(c_gb, b_g.astype(BF16))
        b_gt = b_g.T
        for r in range(HEADS_PER_GROUP):
            h = g * HEADS_PER_GROUP + r
            xs_h = xs[:, h * SSD_HEAD_DIM:(h + 1) * SSD_HEAD_DIM]
            xs_hb = xs_h.astype(BF16)
            acs_col = acs_c[:, h:h + 1]
            acs_row = acs_r[h:h + 1, :]
            dt_row = dt_r[h:h + 1, :]
            seg = jnp.exp(jnp.where(causal, acs_col - acs_row, NEG_BIG))
            w_diag = (cb * seg * dt_row).astype(BF16)
            st = state[h]
            y_h = (_dot(w_diag, xs_hb)
                   + e_c[:, h:h + 1] * _dot(c_gb, st.astype(BF16))
                   + dskip_ref[:, h * SSD_HEAD_DIM:(h + 1) * SSD_HEAD_DIM] * xs_h)
            acs_last = acs_row[:, L - 1:L]
            w_state = jnp.exp(acs_last - acs_row) * dt_row
            state[h] = jnp.exp(acs_last) * st + _dot((b_gt * w_state).astype(BF16), xs_hb)
            ys.append(y_h)
    y = jnp.concatenate(ys, axis=1) * _silu(z_ref[...])

    gw = SSD_WIDTH // N_SSD_GROUPS
    outs = []
    for g in range(N_SSD_GROUPS):
        yg = y[:, g * gw:(g + 1) * gw]
        outs.append(yg * lax.rsqrt(jnp.mean(yg * yg, axis=-1, keepdims=True) + EPS))
    y_ref[...] = (jnp.concatenate(outs, axis=1) * nw_ref[...]).astype(y_ref.dtype)


def _ssd(xbc, z, dtc, dtr, conv_w, conv_b, dtb_row, dtb_col, alog_row, alog_col, dskip, nw):
    b, s, _ = xbc.shape
    nc = s // CHUNK
    tok = lambda width: pl.BlockSpec((None, CHUNK, width), lambda i, c: (i, c, 0))
    return pl.pallas_call(
        _ssd_kernel,
        grid=(b, nc),
        in_specs=[
            tok(CONV_CHANNELS), tok(SSD_WIDTH), tok(LANES),
            pl.BlockSpec((None, DT_ROWS, CHUNK), lambda i, c: (i, 0, c)),
            _resident(conv_w.shape), _resident(conv_b.shape),
            _resident(dtb_row.shape), _resident(dtb_col.shape),
            _resident(alog_row.shape), _resident(alog_col.shape),
            _resident(dskip.shape), _resident(nw.shape),
        ],
        out_specs=tok(SSD_WIDTH),
        out_shape=jax.ShapeDtypeStruct((b, s, SSD_WIDTH), BF16),
        scratch_shapes=[
            pltpu.VMEM((8 + CHUNK, CONV_CHANNELS), F32),
            pltpu.VMEM((N_SSD_HEADS, SSD_STATE, SSD_HEAD_DIM), F32),
        ],
        compiler_params=pltpu.CompilerParams(
            dimension_semantics=("parallel", "arbitrary"), vmem_limit_bytes=VMEM_LIMIT),
        name="ssd",
    )(xbc, z, dtc, dtr, conv_w, conv_b, dtb_row, dtb_col, alog_row, alog_col, dskip, nw)


def _out_mlp_kernel(x_ref, o1_ref, o2_ref, o3_ref, l1_ref, l2_ref, l3_ref, ys_ref, expand_ref,
                    woa_ref, wos_ref, nw_ref, wup_ref, wdn_ref, o_ref):
    lse = l1_ref[...] + l2_ref[...] + l3_ref[...]
    cands = [lse] + [pltpu.roll(lse, LANES - N_ATTN_HEADS * i, 1) for i in (1, 2)]
    m = jnp.maximum(jnp.maximum(cands[0], cands[1]), cands[2])
    es = [jnp.exp(c - m) for c in cands]
    inv = 1.0 / (es[0] + es[1] + es[2])
    o_attn = None
    for e, ob_ref in zip(es, (o1_ref, o2_ref, o3_ref)):
        w = _split_dot(e * inv, expand_ref[...], 2)
        term = w * ob_ref[...].astype(F32)
        o_attn = term if o_attn is None else o_attn + term

    x1 = (x_ref[...] + _dot(o_attn.astype(BF16), woa_ref[...])
          + _dot(ys_ref[...], wos_ref[...]))
    hm = x1 * lax.rsqrt(jnp.mean(x1 * x1, axis=-1, keepdims=True) + EPS) * nw_ref[...]
    up = _dot(hm.astype(BF16), wup_ref[...])
    act = jnp.square(jnp.maximum(up, 0.0)).astype(BF16)
    o_ref[...] = x1 + _dot(act, wdn_ref[...])


def _out_mlp(x, branch_o, branch_lse, y_ssd, expand, wo_a, wo_s, mlp_norm_w, w_up, w_down, tm):
    t, d = x.shape
    tok = lambda width: pl.BlockSpec((tm, width), lambda i: (i, 0))
    return pl.pallas_call(
        _out_mlp_kernel,
        grid=(t // tm,),
        in_specs=[
            tok(d), tok(ATTN_WIDTH), tok(ATTN_WIDTH), tok(ATTN_WIDTH),
            tok(LANES), tok(LANES), tok(LANES), tok(SSD_WIDTH),
            _resident(expand.shape),
            _resident(wo_a.shape), _resident(wo_s.shape), _resident((1, d)),
            _resident(w_up.shape), _resident(w_down.shape),
        ],
        out_specs=tok(d),
        out_shape=jax.ShapeDtypeStruct((t, d), F32),
        compiler_params=pltpu.CompilerParams(
            dimension_semantics=("parallel",), vmem_limit_bytes=VMEM_LIMIT),
        name="out_mlp",
    )(x, *branch_o, *branch_lse, y_ssd, expand, wo_a, wo_s, mlp_norm_w, w_up, w_down)


def _rope_tables(s):
    half = HEAD_DIM // 2
    inv_freq = ROPE_THETA ** (-jnp.arange(half, dtype=F32) / half)
    ang = jnp.arange(s, dtype=F32)[:, None] * inv_freq[None, :]
    cos, sin = jnp.cos(ang), jnp.sin(ang)
    cos_h = jnp.concatenate([cos, cos], axis=1)
    sin_h = jnp.concatenate([-sin, sin], axis=1)
    return jnp.tile(cos_h, (1, N_ATTN_HEADS)), jnp.tile(sin_h, (1, N_ATTN_HEADS))


def _pad_lanes(row):
    return jnp.pad(row, ((0, 0), (0, LANES - row.shape[1])))


def _layer(x, attn_norm_w, w_in, q_norm_w, k_norm_w, conv_w, conv_b, dt_bias, a_log, d_skip,
           ssd_norm_w, w_out, mlp_norm_w, w_up, w_down):
    b, s, d = x.shape
    w_main = w_in[:, :MAIN_WIDTH].astype(BF16)
    w_dt = w_in[:, MAIN_WIDTH:]
    w_dt_c = jnp.pad(w_dt, ((0, 0), (0, LANES - N_SSD_HEADS))).astype(BF16)
    w_dt_r = jnp.pad(w_dt.T, ((0, DT_ROWS - N_SSD_HEADS), (0, 0))).astype(BF16)
    head_of = jnp.arange(ATTN_WIDTH, dtype=jnp.int32) // HEAD_DIM
    group_mean = jnp.where(head_of[:, None] == head_of[None, :], 1.0 / HEAD_DIM, 0.0).astype(BF16)
    cos, sin = _rope_tables(s)
    q, k, v, z, xbc, dtc, dtr = _in_proj(
        x, attn_norm_w[None, :], w_main, w_dt_c, w_dt_r, group_mean,
        jnp.tile(q_norm_w, N_ATTN_HEADS)[None, :], jnp.tile(k_norm_w, N_ATTN_HEADS)[None, :],
        cos, sin, tm=256)

    branches = [_attention_branch(q, k, v, dil, N_ATTN_HEADS * i)
                for i, dil in enumerate(DILATIONS)]

    col = lambda p: jnp.pad(p[:, None], ((0, DT_ROWS - N_SSD_HEADS), (0, 0)))
    y_ssd = _ssd(xbc, z, dtc, dtr, conv_w, conv_b[None, :],
                 _pad_lanes(dt_bias[None, :]), col(dt_bias),
                 _pad_lanes(a_log[None, :]), col(a_log),
                 jnp.repeat(d_skip, SSD_HEAD_DIM)[None, :], ssd_norm_w[None, :])

    expand = (jnp.arange(LANES, dtype=jnp.int32)[:, None] == head_of[None, :]).astype(BF16)
    flat = lambda t: t.reshape(b * s, t.shape[-1])
    out = _out_mlp(flat(x), [flat(o) for o, _ in branches], [flat(l) for _, l in branches],
                   flat(y_ssd), expand,
                   w_out[:ATTN_WIDTH].astype(BF16), w_out[ATTN_WIDTH:].astype(BF16),
                   mlp_norm_w[None, :], w_up.astype(BF16), w_down.astype(BF16), tm=256)
    return out.reshape(b, s, d)


def kernel(x, attn_norm_w, w_in, q_norm_w, k_norm_w, conv_w, conv_b, dt_bias, a_log, d_skip,
           ssd_norm_w, w_out, mlp_norm_w, w_up, w_down):
    for i in range(attn_norm_w.shape[0]):
        x = _layer(x, attn_norm_w[i], w_in[i], q_norm_w[i], k_norm_w[i], conv_w[i], conv_b[i],
                   dt_bias[i], a_log[i], d_skip[i], ssd_norm_w[i], w_out[i], mlp_norm_w[i],
                   w_up[i], w_down[i])
    return x
```

```python
import functools

import jax
import jax.numpy as jnp
from jax import lax
from jax.experimental import pallas as pl
from jax.experimental.pallas import tpu as pltpu

F32 = jnp.float32
BF16 = jnp.bfloat16

HEAD_DIM = 64
N_ATTN_HEADS = 8
ATTN_WIDTH = N_ATTN_HEADS * HEAD_DIM
ROPE_THETA = 10000.0
Q_BLOCK = 128
DILATIONS = (1, 4, 16)
FAR_DIL = 4
FAR_RATIO = 4
SSD_HEAD_DIM = 64
N_SSD_HEADS = 8
SSD_WIDTH = N_SSD_HEADS * SSD_HEAD_DIM
N_SSD_GROUPS = 2
HEADS_PER_GROUP = N_SSD_HEADS // N_SSD_GROUPS
SSD_STATE = 128
CONV_WIDTH = 4
CHUNK = 128
CONV_CHANNELS = SSD_WIDTH + 2 * N_SSD_GROUPS * SSD_STATE
EPS = 1e-6
NEG_BIG = -1e30

LANES = 128
DT_ROWS = 16
MAIN_WIDTH = 3 * ATTN_WIDTH + SSD_WIDTH + CONV_CHANNELS
VMEM_LIMIT = 56 * 1024 * 1024


def _dot(a, b):
    return jnp.dot(a, b, preferred_element_type=F32)


def _dot_nt(a, b):
    return lax.dot_general(a, b, (((1,), (1,)), ((), ())), preferred_element_type=F32)


def _split_dot(a, b, parts, lhs_split=True):
    src = a if lhs_split else b
    acc = None
    rem = src
    for _ in range(parts):
        piece = rem.astype(BF16)
        rem = rem - piece.astype(F32)
        term = _dot(piece, b) if lhs_split else _dot(a, piece)
        acc = term if acc is None else acc + term
    return acc


def _resident(shape):
    zeros = (0,) * len(shape)
    return pl.BlockSpec(shape, lambda *_: zeros, pipeline_mode=pl.Buffered(1))


def _in_proj_kernel(x_ref, nw_ref, w_ref, wdt_ref, wdtt_ref, gm_ref, qw_ref, kw_ref,
                    cos_ref, sin_ref,
                    q_ref, k_ref, v_ref, z_ref, xbc_ref, dtc_ref, dtr_ref):
    x = x_ref[...]
    h = x * lax.rsqrt(jnp.mean(x * x, axis=-1, keepdims=True) + EPS) * nw_ref[...]
    hb = h.astype(BF16)

    lane = lax.broadcasted_iota(jnp.int32, (x.shape[0], ATTN_WIDTH), 1)
    first_half = (lane % HEAD_DIM) < (HEAD_DIM // 2)
    cos = cos_ref[...]
    sin = sin_ref[...]

    def head_norm_rope(t, w):
        ms = _split_dot(t * t, gm_ref[...], 2)
        n = t * lax.rsqrt(ms + EPS) * w
        partner = jnp.where(first_half,
                            pltpu.roll(n, ATTN_WIDTH - HEAD_DIM // 2, 1),
                            pltpu.roll(n, HEAD_DIM // 2, 1))
        return n * cos + partner * sin

    o = 0
    q = _dot(hb, w_ref[:, o:o + ATTN_WIDTH]); o += ATTN_WIDTH
    q_ref[...] = (head_norm_rope(q, qw_ref[...]) * (HEAD_DIM ** -0.5)).astype(BF16)
    k = _dot(hb, w_ref[:, o:o + ATTN_WIDTH]); o += ATTN_WIDTH
    k_ref[...] = head_norm_rope(k, kw_ref[...]).astype(BF16)
    v_ref[...] = _dot(hb, w_ref[:, o:o + ATTN_WIDTH]).astype(BF16); o += ATTN_WIDTH
    z_ref[...] = _dot(hb, w_ref[:, o:o + SSD_WIDTH]); o += SSD_WIDTH
    xbc_ref[...] = _dot(hb, w_ref[:, o:o + CONV_CHANNELS])
    dtc_ref[...] = _dot(hb, wdt_ref[...])
    dtr_ref[...] = _dot_nt(wdtt_ref[...], hb)


def _in_proj(x, attn_norm_w, w_main, w_dt, w_dt_t, group_mean, qw, kw, cos, sin, tm):
    b, s, d = x.shape
    grid = (s // tm, b)
    tok = lambda width: pl.BlockSpec((None, tm, width), lambda j, i: (i, j, 0))
    out_shapes = (
        jax.ShapeDtypeStruct((b, s, ATTN_WIDTH), BF16),
        jax.ShapeDtypeStruct((b, s, ATTN_WIDTH), BF16),
        jax.ShapeDtypeStruct((b, s, ATTN_WIDTH), BF16),
        jax.ShapeDtypeStruct((b, s, SSD_WIDTH), F32),
        jax.ShapeDtypeStruct((b, s, CONV_CHANNELS), F32),
        jax.ShapeDtypeStruct((b, s, LANES), F32),
        jax.ShapeDtypeStruct((b, DT_ROWS, s), F32),
    )
    return pl.pallas_call(
        _in_proj_kernel,
        grid=grid,
        in_specs=[
            tok(d),
            _resident((1, d)),
            _resident(w_main.shape),
            _resident(w_dt.shape),
            _resident(w_dt_t.shape),
            _resident(group_mean.shape),
            _resident((1, ATTN_WIDTH)),
            _resident((1, ATTN_WIDTH)),
            pl.BlockSpec((tm, ATTN_WIDTH), lambda j, i: (j, 0)),
            pl.BlockSpec((tm, ATTN_WIDTH), lambda j, i: (j, 0)),
        ],
        out_specs=(
            tok(ATTN_WIDTH), tok(ATTN_WIDTH), tok(ATTN_WIDTH), tok(SSD_WIDTH),
            tok(CONV_CHANNELS), tok(LANES),
            pl.BlockSpec((None, DT_ROWS, tm), lambda j, i: (i, 0, j)),
        ),
        out_shape=out_shapes,
        compiler_params=pltpu.CompilerParams(
            dimension_semantics=("parallel", "parallel"), vmem_limit_bytes=VMEM_LIMIT),
        name="in_proj",
    )(x, attn_norm_w, w_main, w_dt, w_dt_t, group_mean, qw, kw, cos, sin)


def _attn_blocks(blocks, store):
    qb = Q_BLOCK
    n_pairs = ATTN_WIDTH // LANES
    head0 = lax.broadcasted_iota(jnp.int32, (qb, LANES), 1) < HEAD_DIM
    items = [(i, pair) for i in range(len(blocks)) for pair in range(n_pairs)]

    def scores(item):
        i, pair = item
        load_q, load_k, _, load_bias = blocks[i]
        qt, kb, bias = load_q(pair), load_k(pair), load_bias()
        out = []
        for h in range(2):
            mine = head0 if h == 0 else jnp.logical_not(head0)
            qh = jnp.where(mine, qt, jnp.zeros_like(qt))
            out.append(_dot_nt(qh, kb) + bias)
        return out

    pending = scores(items[0])
    outs, lses = [], []
    for n, (i, pair) in enumerate(items):
        upcoming = scores(items[n + 1]) if n + 1 < len(items) else None
        vb = blocks[i][2](pair)
        v_ext = jnp.concatenate([vb, jnp.ones_like(vb)], axis=1)
        o_pair = lse_pair = None
        for h, sc in enumerate(pending):
            m = jnp.max(sc, axis=1, keepdims=True)
            p = jnp.exp(sc - m).astype(BF16)
            r = _dot(p, v_ext)
            l = r[:, LANES:]
            o_h = r[:, :LANES] / l
            lse_h = m + jnp.log(l)
            o_pair = o_h if h == 0 else jnp.where(head0, o_pair, o_h)
            lse_pair = lse_h if h == 0 else jnp.where(head0, lse_pair, lse_h)
        outs.append(o_pair)
        lses.append(lse_pair)
        if pair == n_pairs - 1:
            store(i, jnp.concatenate(outs, axis=1), jnp.concatenate(lses, axis=1))
            outs, lses = [], []
        pending = upcoming


def _near_bias(first):
    qb = Q_BLOCK
    row = lax.broadcasted_iota(jnp.int32, (qb, 2 * qb), 0)
    col = lax.broadcasted_iota(jnp.int32, (qb, 2 * qb), 1)
    ok = (col <= row) if first else ((col >= row) & (col <= row + qb))
    return jnp.where(ok, 0.0, NEG_BIG)


def _far_bias(blocks_back):
    qb = Q_BLOCK
    row = lax.broadcasted_iota(jnp.int32, (qb, qb), 0)
    col = lax.broadcasted_iota(jnp.int32, (qb, qb), 1)
    dist = row - col + blocks_back * qb
    in_window = (dist >= 0) & (dist <= qb)
    on_stride = (dist >= 0) & ((dist & (FAR_RATIO - 1)) == 0)
    return jnp.where(in_window & on_stride, jnp.log(2.0),
                     jnp.where(in_window | on_stride, 0.0, NEG_BIG))


def _attn_kernel(q_ref, k_ref, v_ref, o_ref,
                 tmp, q_cls, k_cls, v_cls, o_far, lse_far, near_bias, far_bias):
    qb = Q_BLOCK
    step = pl.program_id(1)
    rows = q_ref.shape[0]
    s = k_ref.shape[0]
    n_steps = s // rows
    n_pairs = ATTN_WIDTH // LANES
    lanes = lambda pair: pl.ds(pair * LANES, LANES)

    def to_class_order(src_rows, dst, at):
        for p in range(n_pairs):
            tmp[p] = src_rows[:, p * LANES:(p + 1) * LANES].astype(F32)
            for c in range(FAR_DIL):
                dst[c, at:at + qb, lanes(p)] = tmp[p, pl.ds(c, qb, stride=FAR_DIL), :].astype(BF16)

    @pl.when(step == 0)
    def _():
        for i in range(n_steps):
            to_class_order(k_ref[i * rows:(i + 1) * rows, :], k_cls, i * qb)
            to_class_order(v_ref[i * rows:(i + 1) * rows, :], v_cls, i * qb)

    to_class_order(q_ref[...], q_cls, 0)
    near_bias[0] = _near_bias(False)
    near_bias[1] = _near_bias(True)
    for d in range(3):
        far_bias[d] = _far_bias(d)

    def far_store(c, o, lse):
        for p in range(n_pairs):
            o_far[p, pl.ds(c, qb, stride=FAR_DIL), :] = o[:, p * LANES:(p + 1) * LANES]
            lse_far[p, pl.ds(c, qb, stride=FAR_DIL), :] = lse[:, p * LANES:(p + 1) * LANES]

    for n_kb in range(1, n_steps + 1):
        @pl.when(step == n_kb - 1)
        def _(n_kb=n_kb):
            def block(c):
                load_bias = lambda: jnp.concatenate(
                    [far_bias[min(n_kb - 1 - kb, 2)] for kb in range(n_kb)], axis=1)
                return (lambda p: q_cls[c, :, lanes(p)],
                        lambda p: k_cls[c, 0:n_kb * qb, lanes(p)],
                        lambda p: v_cls[c, 0:n_kb * qb, lanes(p)], load_bias)
            _attn_blocks([block(c) for c in range(FAR_DIL)], far_store)

    is_first = (step == 0).astype(jnp.int32)

    def near_block(j):
        q_rows = pl.ds(j * qb, qb)
        if j == 0:
            start = pl.multiple_of((step * FAR_DIL - 1 + is_first) * qb, qb)
            load_bias = lambda: near_bias[is_first]
        else:
            start = pl.multiple_of((step * FAR_DIL + j - 1) * qb, qb)
            load_bias = lambda: near_bias[0]
        kv_rows = pl.ds(start, 2 * qb)
        return (lambda p: q_ref[q_rows, lanes(p)], lambda p: k_ref[kv_rows, lanes(p)],
                lambda p: v_ref[kv_rows, lanes(p)], load_bias)

    def near_store(j, o, lse):
        q_rows = pl.ds(j * qb, qb)
        o_f = jnp.concatenate([o_far[p, q_rows, :] for p in range(n_pairs)], axis=1)
        lse_f = jnp.concatenate([lse_far[p, q_rows, :] for p in range(n_pairs)], axis=1)
        top = jnp.maximum(lse, lse_f)
        w_near, w_far = jnp.exp(lse - top), jnp.exp(lse_f - top)
        o_ref[q_rows, :] = ((w_near * o + w_far * o_f) / (w_near + w_far)).astype(o_ref.dtype)

    _attn_blocks([near_block(j) for j in range(rows // qb)], near_store)


def _attention(q, k, v):
    b, s, w = q.shape
    assert DILATIONS == (1, FAR_DIL, FAR_DIL * FAR_RATIO) and s == Q_BLOCK * DILATIONS[-1]
    rows = FAR_DIL * Q_BLOCK
    q_blk = pl.BlockSpec((None, rows, w), lambda i, g: (i, g, 0))
    kv_blk = pl.BlockSpec((None, s, w), lambda i, g: (i, 0, 0))
    return pl.pallas_call(
        _attn_kernel,
        grid=(b, s // rows),
        in_specs=[q_blk, kv_blk, kv_blk],
        out_specs=q_blk,
        out_shape=jax.ShapeDtypeStruct((b, s, w), BF16),
        scratch_shapes=[
            pltpu.VMEM((w // LANES, rows, LANES), F32),
            pltpu.VMEM((FAR_DIL, Q_BLOCK, w), BF16),
            pltpu.VMEM((FAR_DIL, s // FAR_DIL, w), BF16),
            pltpu.VMEM((FAR_DIL, s // FAR_DIL, w), BF16),
            pltpu.VMEM((w // LANES, rows, LANES), F32),
            pltpu.VMEM((w // LANES, rows, LANES), F32),
            pltpu.VMEM((2, Q_BLOCK, 2 * Q_BLOCK), F32),
            pltpu.VMEM((3, Q_BLOCK, Q_BLOCK), F32),
        ],
        compiler_params=pltpu.CompilerParams(
            dimension_semantics=("parallel", "arbitrary"), vmem_limit_bytes=VMEM_LIMIT),
        name="dilated_attn",
    )(q, k, v)


def _softplus(x):
    return jnp.maximum(x, 0.0) + jnp.log1p(jnp.exp(-jnp.abs(x)))


def _silu(x):
    return x * (1.0 / (1.0 + jnp.exp(-x)))


def _ssd_kernel(xbc_ref, z_ref, dtc_ref, dtr_ref, cw_ref, cb_ref, dtb_row_ref, dtb_col_ref,
                alog_row_ref, alog_col_ref, dskip_ref, nw_ref, y_ref, ubuf, state):
    L = CHUNK
    pad = 8
    c = pl.program_id(1)

    @pl.when(c == 0)
    def _():
        ubuf[0:pad, :] = jnp.zeros((pad, CONV_CHANNELS), F32)
        state[...] = jnp.zeros(state.shape, F32)

    u = xbc_ref[...]
    ubuf[pad:pad + L, :] = u
    conv = cb_ref[...] + cw_ref[CONV_WIDTH - 1:CONV_WIDTH, :] * u
    for tap in range(CONV_WIDTH - 1):
        shift = CONV_WIDTH - 1 - tap
        conv = conv + cw_ref[tap:tap + 1, :] * ubuf[pad - shift:pad - shift + L, :]
    ubuf[0:pad, :] = u[L - pad:L, :]
    act = _silu(conv)
    xs = act[:, :SSD_WIDTH]
    gs = N_SSD_GROUPS * SSD_STATE
    bm = act[:, SSD_WIDTH:SSD_WIDTH + gs]
    cm = act[:, SSD_WIDTH + gs:SSD_WIDTH + 2 * gs]

    dt_c = _softplus(dtc_ref[...] + dtb_row_ref[...])
    dt_r = _softplus(dtr_ref[...] + dtb_col_ref[...])
    a_row = -jnp.exp(alog_row_ref[...])
    a_col = -jnp.exp(alog_col_ref[...])
    ri = lax.broadcasted_iota(jnp.int32, (L, L), 0)
    ci = lax.broadcasted_iota(jnp.int32, (L, L), 1)
    causal = ci <= ri
    tril = jnp.where(causal, 1.0, 0.0).astype(BF16)
    triu = jnp.where(ri <= ci, 1.0, 0.0).astype(BF16)
    acs_c = _split_dot(tril, dt_c * a_row, 3, lhs_split=False)
    acs_r = _split_dot(dt_r * a_col, triu, 3)
    e_c = jnp.exp(acs_c)

    ys = []
    for g in range(N_SSD_GROUPS):
        b_g = bm[:, g * SSD_STATE:(g + 1) * SSD_STATE]
        c_g = cm[:, g * SSD_STATE:(g + 1) * SSD_STATE]
        c_gb = c_g.astype(BF16)
        cb = _dot_nt(c_gb, b_g.astype(BF16))
        b_gt = b_g.T
        for r in range(HEADS_PER_GROUP):
            h = g * HEADS_PER_GROUP + r
            xs_h = xs[:, h * SSD_HEAD_DIM:(h + 1) * SSD_HEAD_DIM]
            xs_hb = xs_h.astype(BF16)
            acs_col = acs_c[:, h:h + 1]
            acs_row = acs_r[h:h + 1, :]
            dt_row = dt_r[h:h + 1, :]
            seg = jnp.exp(jnp.where(causal, acs_col - acs_row, NEG_BIG))
            w_diag = (cb * seg * dt_row).astype(BF16)
            st = state[h]
            y_h = (_dot(w_diag, xs_hb)
                   + e_c[:, h:h + 1] * _dot(c_gb, st.astype(BF16))
                   + dskip_ref[:, h * SSD_HEAD_DIM:(h + 1) * SSD_HEAD_DIM] * xs_h)
            acs_last = acs_row[:, L - 1:L]
            w_state = jnp.exp(acs_last - acs_row) * dt_row
            state[h] = jnp.exp(acs_last) * st + _dot((b_gt * w_state).astype(BF16), xs_hb)
            ys.append(y_h)
    y = jnp.concatenate(ys, axis=1) * _silu(z_ref[...])

    gw = SSD_WIDTH // N_SSD_GROUPS
    outs = []
    for g in range(N_SSD_GROUPS):
        yg = y[:, g * gw:(g + 1) * gw]
        outs.append(yg * lax.rsqrt(jnp.mean(yg * yg, axis=-1, keepdims=True) + EPS))
    y_ref[...] = (jnp.concatenate(outs, axis=1) * nw_ref[...]).astype(y_ref.dtype)


def _ssd(xbc, z, dtc, dtr, conv_w, conv_b, dtb_row, dtb_col, alog_row, alog_col, dskip, nw):
    b, s, _ = xbc.shape
    nc = s // CHUNK
    tok = lambda width: pl.BlockSpec((None, CHUNK, width), lambda i, c: (i, c, 0))
    return pl.pallas_call(
        _ssd_kernel,
        grid=(b, nc),
        in_specs=[
            tok(CONV_CHANNELS), tok(SSD_WIDTH), tok(LANES),
            pl.BlockSpec((None, DT_ROWS, CHUNK), lambda i, c: (i, 0, c)),
            _resident(conv_w.shape), _resident(conv_b.shape),
            _resident(dtb_row.shape), _resident(dtb_col.shape),
            _resident(alog_row.shape), _resident(alog_col.shape),
            _resident(dskip.shape), _resident(nw.shape),
        ],
        out_specs=tok(SSD_WIDTH),
        out_shape=jax.ShapeDtypeStruct((b, s, SSD_WIDTH), BF16),
        scratch_shapes=[
            pltpu.VMEM((8 + CHUNK, CONV_CHANNELS), F32),
            pltpu.VMEM((N_SSD_HEADS, SSD_STATE, SSD_HEAD_DIM), F32),
        ],
        compiler_params=pltpu.CompilerParams(
            dimension_semantics=("parallel", "arbitrary"), vmem_limit_bytes=VMEM_LIMIT),
        name="ssd",
    )(xbc, z, dtc, dtr, conv_w, conv_b, dtb_row, dtb_col, alog_row, alog_col, dskip, nw)


def _out_mlp_kernel(x_ref, oa_ref, ys_ref, woa_ref, wos_ref, nw_ref, wup_ref, wdn_ref, o_ref):
    x1 = x_ref[...] + _dot(oa_ref[...], woa_ref[...]) + _dot(ys_ref[...], wos_ref[...])
    hm = x1 * lax.rsqrt(jnp.mean(x1 * x1, axis=-1, keepdims=True) + EPS) * nw_ref[...]
    up = _dot(hm.astype(BF16), wup_ref[...])
    act = jnp.square(jnp.maximum(up, 0.0)).astype(BF16)
    o_ref[...] = x1 + _dot(act, wdn_ref[...])


def _out_mlp(x, o_attn, y_ssd, wo_a, wo_s, mlp_norm_w, w_up, w_down, tm):
    t, d = x.shape
    tok = lambda width: pl.BlockSpec((tm, width), lambda i: (i, 0))
    return pl.pallas_call(
        _out_mlp_kernel,
        grid=(t // tm,),
        in_specs=[
            tok(d), tok(ATTN_WIDTH), tok(SSD_WIDTH),
            _resident(wo_a.shape), _resident(wo_s.shape), _resident((1, d)),
            _resident(w_up.shape), _resident(w_down.shape),
        ],
        out_specs=tok(d),
        out_shape=jax.ShapeDtypeStruct((t, d), F32),
        compiler_params=pltpu.CompilerParams(
            dimension_semantics=("parallel",), vmem_limit_bytes=VMEM_LIMIT),
        name="out_mlp",
    )(x, o_attn, y_ssd, wo_a, wo_s, mlp_norm_w, w_up, w_down)


def _rope_tables(s):
    half = HEAD_DIM // 2
    inv_freq = ROPE_THETA ** (-jnp.arange(half, dtype=F32) / half)
    ang = jnp.arange(s, dtype=F32)[:, None] * inv_freq[None, :]
    cos, sin = jnp.cos(ang), jnp.sin(ang)
    cos_h = jnp.concatenate([cos, cos], axis=1)
    sin_h = jnp.concatenate([-sin, sin], axis=1)
    return jnp.tile(cos_h, (1, N_ATTN_HEADS)), jnp.tile(sin_h, (1, N_ATTN_HEADS))


def _pad_lanes(row):
    return jnp.pad(row, ((0, 0), (0, LANES - row.shape[1])))


def _layer(x, attn_norm_w, w_in, q_norm_w, k_norm_w, conv_w, conv_b, dt_bias, a_log, d_skip,
           ssd_norm_w, w_out, mlp_norm_w, w_up, w_down):
    b, s, d = x.shape
    w_main = w_in[:, :MAIN_WIDTH].astype(BF16)
    w_dt = w_in[:, MAIN_WIDTH:]
    w_dt_c = jnp.pad(w_dt, ((0, 0), (0, LANES - N_SSD_HEADS))).astype(BF16)
    w_dt_r = jnp.pad(w_dt.T, ((0, DT_ROWS - N_SSD_HEADS), (0, 0))).astype(BF16)
    head_of = jnp.arange(ATTN_WIDTH, dtype=jnp.int32) // HEAD_DIM
    group_mean = jnp.where(head_of[:, None] == head_of[None, :], 1.0 / HEAD_DIM, 0.0).astype(BF16)
    cos, sin = _rope_tables(s)
    q, k, v, z, xbc, dtc, dtr = _in_proj(
        x, attn_norm_w[None, :], w_main, w_dt_c, w_dt_r, group_mean,
        jnp.tile(q_norm_w, N_ATTN_HEADS)[None, :], jnp.tile(k_norm_w, N_ATTN_HEADS)[None, :],
        cos, sin, tm=256)

    o_attn = _attention(q, k, v)

    col = lambda p: jnp.pad(p[:, None], ((0, DT_ROWS - N_SSD_HEADS), (0, 0)))
    y_ssd = _ssd(xbc, z, dtc, dtr, conv_w, conv_b[None, :],
                 _pad_lanes(dt_bias[None, :]), col(dt_bias),
                 _pad_lanes(a_log[None, :]), col(a_log),
                 jnp.repeat(d_skip, SSD_HEAD_DIM)[None, :], ssd_norm_w[None, :])

    out = _out_mlp(x.reshape(b * s, d), o_attn.reshape(b * s, ATTN_WIDTH),
                   y_ssd.reshape(b * s, SSD_WIDTH),
                   w_out[:ATTN_WIDTH].astype(BF16), w_out[ATTN_WIDTH:].astype(BF16),
                   mlp_norm_w[None, :], w_up.astype(BF16), w_down.astype(BF16), tm=256)
    return out.reshape(b, s, d)


def kernel(x, attn_norm_w, w_in, q_norm_w, k_norm_w, conv_w, conv_b, dt_bias, a_log, d_skip,
           ssd_norm_w, w_out, mlp_norm_w, w_up, w_down):
    for i in range(attn_norm_w.shape[0]):
        x = _layer(x, attn_norm_w[i], w_in[i], q_norm_w[i], k_norm_w[i], conv_w[i], conv_b[i],
                   dt_bias[i], a_log[i], d_skip[i], ssd_norm_w[i], w_out[i], mlp_norm_w[i],
                   w_up[i], w_down[i])
    return x
```

```python
import functools

import jax
import jax.numpy as jnp
from jax import lax
from jax.experimental import pallas as pl
from jax.experimental.pallas import tpu as pltpu

F32 = jnp.float32
BF16 = jnp.bfloat16

HEAD_DIM = 64
N_ATTN_HEADS = 8
ATTN_WIDTH = N_ATTN_HEADS * HEAD_DIM
ROPE_THETA = 10000.0
Q_BLOCK = 128
DILATIONS = (1, 4, 16)
FAR_DIL = 4
FAR_RATIO = 4
SSD_HEAD_DIM = 64
N_SSD_HEADS = 8
SSD_WIDTH = N_SSD_HEADS * SSD_HEAD_DIM
N_SSD_GROUPS = 2
HEADS_PER_GROUP = N_SSD_HEADS // N_SSD_GROUPS
SSD_STATE = 128
CONV_WIDTH = 4
CHUNK = 128
CONV_CHANNELS = SSD_WIDTH + 2 * N_SSD_GROUPS * SSD_STATE
EPS = 1e-6
NEG_BIG = -1e30

LANES = 128
DT_ROWS = 16
MAIN_WIDTH = 3 * ATTN_WIDTH + SSD_WIDTH + CONV_CHANNELS
VMEM_LIMIT = 56 * 1024 * 1024
PROJ_TOKENS = 512
MLP_TOKENS = 512


def _dot(a, b):
    return jnp.dot(a, b, preferred_element_type=F32)


def _dot_nt(a, b):
    return lax.dot_general(a, b, (((1,), (1,)), ((), ())), preferred_element_type=F32)


def _split_dot(a, b, parts, lhs_split=True):
    src = a if lhs_split else b
    acc = None
    rem = src
    for _ in range(parts):
        piece = rem.astype(BF16)
        rem = rem - piece.astype(F32)
        term = _dot(piece, b) if lhs_split else _dot(a, piece)
        acc = term if acc is None else acc + term
    return acc


def _resident(shape):
    zeros = (0,) * len(shape)
    return pl.BlockSpec(shape, lambda *_: zeros, pipeline_mode=pl.Buffered(1))


def _in_proj_kernel(x_ref, nw_ref, w_ref, wdt_ref, gm_ref, qw_ref, kw_ref, cos_ref, sin_ref,
                    q_ref, k_ref, v_ref, z_ref, xbc_ref, dtc_ref, dtr_ref):
    x = x_ref[...]
    h = x * lax.rsqrt(jnp.mean(x * x, axis=-1, keepdims=True) + EPS) * nw_ref[...]
    hb = h.astype(BF16)

    lane = lax.broadcasted_iota(jnp.int32, (x.shape[0], ATTN_WIDTH), 1)
    first_half = (lane % HEAD_DIM) < (HEAD_DIM // 2)
    cos = jnp.concatenate([cos_ref[...]] * (ATTN_WIDTH // LANES), axis=1)
    sin = jnp.concatenate([sin_ref[...]] * (ATTN_WIDTH // LANES), axis=1)

    def head_norm_rope(t, w):
        ms = _dot((t * t).astype(BF16), gm_ref[...])
        n = t * lax.rsqrt(ms + EPS) * w
        partner = jnp.where(first_half,
                            pltpu.roll(n, ATTN_WIDTH - HEAD_DIM // 2, 1),
                            pltpu.roll(n, HEAD_DIM // 2, 1))
        return n * cos + partner * sin

    o = 0
    q = _dot(hb, w_ref[:, o:o + ATTN_WIDTH]); o += ATTN_WIDTH
    q_ref[...] = (head_norm_rope(q, qw_ref[...]) * (HEAD_DIM ** -0.5)).astype(BF16)
    k = _dot(hb, w_ref[:, o:o + ATTN_WIDTH]); o += ATTN_WIDTH
    k_ref[...] = head_norm_rope(k, kw_ref[...]).astype(BF16)
    v_ref[...] = _dot(hb, w_ref[:, o:o + ATTN_WIDTH]).astype(BF16); o += ATTN_WIDTH
    z_ref[...] = _dot(hb, w_ref[:, o:o + SSD_WIDTH]); o += SSD_WIDTH
    xbc_ref[...] = _dot(hb, w_ref[:, o:o + CONV_CHANNELS])
    dtc = _dot(hb, wdt_ref[...])
    dtc_ref[...] = dtc
    dtr_ref[...] = dtc.T[:DT_ROWS, :]


def _in_proj(x, attn_norm_w, w_main, w_dt, group_mean, qw, kw, cos, sin, tm):
    b, s, d = x.shape
    grid = (s // tm, b)
    tok = lambda width: pl.BlockSpec((None, tm, width), lambda j, i: (i, j, 0))
    out_shapes = (
        jax.ShapeDtypeStruct((b, s, ATTN_WIDTH), BF16),
        jax.ShapeDtypeStruct((b, s, ATTN_WIDTH), BF16),
        jax.ShapeDtypeStruct((b, s, ATTN_WIDTH), BF16),
        jax.ShapeDtypeStruct((b, s, SSD_WIDTH), F32),
        jax.ShapeDtypeStruct((b, s, CONV_CHANNELS), F32),
        jax.ShapeDtypeStruct((b, s, LANES), F32),
        jax.ShapeDtypeStruct((b, DT_ROWS, s), F32),
    )
    return pl.pallas_call(
        _in_proj_kernel,
        grid=grid,
        in_specs=[
            tok(d),
            _resident((1, d)),
            _resident(w_main.shape),
            _resident(w_dt.shape),
            _resident(group_mean.shape),
            _resident((1, ATTN_WIDTH)),
            _resident((1, ATTN_WIDTH)),
            pl.BlockSpec((tm, LANES), lambda j, i: (j, 0)),
            pl.BlockSpec((tm, LANES), lambda j, i: (j, 0)),
        ],
        out_specs=(
            tok(ATTN_WIDTH), tok(ATTN_WIDTH), tok(ATTN_WIDTH), tok(SSD_WIDTH),
            tok(CONV_CHANNELS), tok(LANES),
            pl.BlockSpec((None, DT_ROWS, tm), lambda j, i: (i, 0, j)),
        ),
        out_shape=out_shapes,
        compiler_params=pltpu.CompilerParams(
            dimension_semantics=("parallel", "parallel"), vmem_limit_bytes=VMEM_LIMIT),
        name="in_proj",
    )(x, attn_norm_w, w_main, w_dt, group_mean, qw, kw, cos, sin)


def _attn_blocks(blocks, store):
    qb = Q_BLOCK
    n_pairs = ATTN_WIDTH // LANES
    head0 = lax.broadcasted_iota(jnp.int32, (qb, LANES), 1) < HEAD_DIM
    items = [(i, pair) for i in range(len(blocks)) for pair in range(n_pairs)]

    def scores(item):
        i, pair = item
        load_q, load_k, _, load_bias = blocks[i]
        qt, kb, bias = load_q(pair), load_k(pair), load_bias()
        out = []
        for h in range(2):
            mine = head0 if h == 0 else jnp.logical_not(head0)
            qh = jnp.where(mine, qt, jnp.zeros_like(qt))
            out.append(_dot_nt(qh, kb) + bias)
        return out

    pending = scores(items[0])
    outs, lses = [], []
    for n, (i, pair) in enumerate(items):
        upcoming = scores(items[n + 1]) if n + 1 < len(items) else None
        vb = blocks[i][2](pair)
        v_ext = jnp.concatenate([vb, jnp.ones_like(vb)], axis=1)
        o_pair = lse_pair = None
        for h, sc in enumerate(pending):
            m = jnp.max(sc, axis=1, keepdims=True)
            p = jnp.exp(sc - m).astype(BF16)
            r = _dot(p, v_ext)
            l = r[:, LANES:]
            o_h = r[:, :LANES] / l
            lse_h = m + jnp.log(l)
            o_pair = o_h if h == 0 else jnp.where(head0, o_pair, o_h)
            lse_pair = lse_h if h == 0 else jnp.where(head0, lse_pair, lse_h)
        outs.append(o_pair)
        lses.append(lse_pair)
        if pair == n_pairs - 1:
            store(i, jnp.concatenate(outs, axis=1), jnp.concatenate(lses, axis=1))
            outs, lses = [], []
        pending = upcoming


def _near_bias(first):
    qb = Q_BLOCK
    row = lax.broadcasted_iota(jnp.int32, (qb, 2 * qb), 0)
    col = lax.broadcasted_iota(jnp.int32, (qb, 2 * qb), 1)
    ok = (col <= row) if first else ((col >= row) & (col <= row + qb))
    return jnp.where(ok, 0.0, NEG_BIG)


def _far_bias(blocks_back):
    qb = Q_BLOCK
    row = lax.broadcasted_iota(jnp.int32, (qb, qb), 0)
    col = lax.broadcasted_iota(jnp.int32, (qb, qb), 1)
    dist = row - col + blocks_back * qb
    in_window = (dist >= 0) & (dist <= qb)
    on_stride = (dist >= 0) & ((dist & (FAR_RATIO - 1)) == 0)
    return jnp.where(in_window & on_stride, jnp.log(2.0),
                     jnp.where(in_window | on_stride, 0.0, NEG_BIG))


def _attn_kernel(q_ref, k_ref, v_ref, o_ref,
                 tmp, q_cls, k_cls, v_cls, o_far, lse_far, near_bias, far_bias):
    qb = Q_BLOCK
    step = pl.program_id(1)
    rows = q_ref.shape[0]
    s = k_ref.shape[0]
    n_steps = s // rows
    n_pairs = ATTN_WIDTH // LANES
    lanes = lambda pair: pl.ds(pair * LANES, LANES)

    def to_class_order(src_rows, dst, at):
        for p in range(n_pairs):
            tmp[p] = src_rows[:, p * LANES:(p + 1) * LANES].astype(F32)
            for c in range(FAR_DIL):
                dst[c, at:at + qb, lanes(p)] = tmp[p, pl.ds(c, qb, stride=FAR_DIL), :].astype(BF16)

    @pl.when(step == 0)
    def _():
        for i in range(n_steps):
            to_class_order(k_ref[i * rows:(i + 1) * rows, :], k_cls, i * qb)
            to_class_order(v_ref[i * rows:(i + 1) * rows, :], v_cls, i * qb)

    to_class_order(q_ref[...], q_cls, 0)
    near_bias[0] = _near_bias(False)
    near_bias[1] = _near_bias(True)
    for d in range(3):
        far_bias[d] = _far_bias(d)

    def far_store(c, o, lse):
        for p in range(n_pairs):
            o_far[p, pl.ds(c, qb, stride=FAR_DIL), :] = o[:, p * LANES:(p + 1) * LANES]
            lse_far[p, pl.ds(c, qb, stride=FAR_DIL), :] = lse[:, p * LANES:(p + 1) * LANES]

    for n_kb in range(1, n_steps + 1):
        @pl.when(step == n_kb - 1)
        def _(n_kb=n_kb):
            def block(c):
                load_bias = lambda: jnp.concatenate(
                    [far_bias[min(n_kb - 1 - kb, 2)] for kb in range(n_kb)], axis=1)
                return (lambda p: q_cls[c, :, lanes(p)],
                        lambda p: k_cls[c, 0:n_kb * qb, lanes(p)],
                        lambda p: v_cls[c, 0:n_kb * qb, lanes(p)], load_bias)
            _attn_blocks([block(c) for c in range(FAR_DIL)], far_store)

    is_first = (step == 0).astype(jnp.int32)

    def near_block(j):
        q_rows = pl.ds(j * qb, qb)
        if j == 0:
            start = pl.multiple_of((step * FAR_DIL - 1 + is_first) * qb, qb)
            load_bias = lambda: near_bias[is_first]
        else:
            start = pl.multiple_of((step * FAR_DIL + j - 1) * qb, qb)
            load_bias = lambda: near_bias[0]
        kv_rows = pl.ds(start, 2 * qb)
        return (lambda p: q_ref[q_rows, lanes(p)], lambda p: k_ref[kv_rows, lanes(p)],
                lambda p: v_ref[kv_rows, lanes(p)], load_bias)

    def near_store(j, o, lse):
        q_rows = pl.ds(j * qb, qb)
        o_f = jnp.concatenate([o_far[p, q_rows, :] for p in range(n_pairs)], axis=1)
        lse_f = jnp.concatenate([lse_far[p, q_rows, :] for p in range(n_pairs)], axis=1)
        top = jnp.maximum(lse, lse_f)
        w_near, w_far = jnp.exp(lse - top), jnp.exp(lse_f - top)
        o_ref[q_rows, :] = ((w_near * o + w_far * o_f) / (w_near + w_far)).astype(o_ref.dtype)

    _attn_blocks([near_block(j) for j in range(rows // qb)], near_store)


def _attention(q, k, v):
    b, s, w = q.shape
    assert DILATIONS == (1, FAR_DIL, FAR_DIL * FAR_RATIO) and s == Q_BLOCK * DILATIONS[-1]
    rows = FAR_DIL * Q_BLOCK
    q_blk = pl.BlockSpec((None, rows, w), lambda i, g: (i, g, 0))
    kv_blk = pl.BlockSpec((None, s, w), lambda i, g: (i, 0, 0))
    return pl.pallas_call(
        _attn_kernel,
        grid=(b, s // rows),
        in_specs=[q_blk, kv_blk, kv_blk],
        out_specs=q_blk,
        out_shape=jax.ShapeDtypeStruct((b, s, w), BF16),
        scratch_shapes=[
            pltpu.VMEM((w // LANES, rows, LANES), F32),
            pltpu.VMEM((FAR_DIL, Q_BLOCK, w), BF16),
            pltpu.VMEM((FAR_DIL, s // FAR_DIL, w), BF16),
            pltpu.VMEM((FAR_DIL, s // FAR_DIL, w), BF16),
            pltpu.VMEM((w // LANES, rows, LANES), F32),
            pltpu.VMEM((w // LANES, rows, LANES), F32),
            pltpu.VMEM((2, Q_BLOCK, 2 * Q_BLOCK), F32),
            pltpu.VMEM((3, Q_BLOCK, Q_BLOCK), F32),
        ],
        compiler_params=pltpu.CompilerParams(
            dimension_semantics=("parallel", "arbitrary"), vmem_limit_bytes=VMEM_LIMIT),
        name="dilated_attn",
    )(q, k, v)


def _softplus(x):
    return jnp.maximum(x, 0.0) + jnp.log1p(jnp.exp(-jnp.abs(x)))


def _silu(x):
    return x * (1.0 / (1.0 + jnp.exp(-x)))


def _ssd_kernel(xbc_ref, z_ref, dtc_ref, dtr_ref, cw_ref, cb_ref, dtb_row_ref, dtb_col_ref,
                alog_row_ref, alog_col_ref, dskip_ref, nw_ref, y_ref, ubuf, state):
    L = CHUNK
    pad = 8
    c = pl.program_id(1)

    @pl.when(c == 0)
    def _():
        ubuf[0:pad, :] = jnp.zeros((pad, CONV_CHANNELS), F32)
        state[...] = jnp.zeros(state.shape, F32)

    u = xbc_ref[...]
    ubuf[pad:pad + L, :] = u
    conv = cb_ref[...] + cw_ref[CONV_WIDTH - 1:CONV_WIDTH, :] * u
    for tap in range(CONV_WIDTH - 1):
        shift = CONV_WIDTH - 1 - tap
        conv = conv + cw_ref[tap:tap + 1, :] * ubuf[pad - shift:pad - shift + L, :]
    ubuf[0:pad, :] = u[L - pad:L, :]
    act = _silu(conv)
    xs = act[:, :SSD_WIDTH]
    gs = N_SSD_GROUPS * SSD_STATE
    bm = act[:, SSD_WIDTH:SSD_WIDTH + gs]
    cm = act[:, SSD_WIDTH + gs:SSD_WIDTH + 2 * gs]

    dt_c = _softplus(dtc_ref[...] + dtb_row_ref[...])
    dt_r = _softplus(dtr_ref[...] + dtb_col_ref[...])
    a_row = -jnp.exp(alog_row_ref[...])
    a_col = -jnp.exp(alog_col_ref[...])
    ri = lax.broadcasted_iota(jnp.int32, (L, L), 0)
    ci = lax.broadcasted_iota(jnp.int32, (L, L), 1)
    causal = ci <= ri
    tril = jnp.where(causal, 1.0, 0.0).astype(BF16)
    triu = jnp.where(ri <= ci, 1.0, 0.0).astype(BF16)
    acs_c = _split_dot(tril, dt_c * a_row, 3, lhs_split=False)
    acs_r = _split_dot(dt_r * a_col, triu, 3)
    e_c = jnp.exp(acs_c)

    ys = []
    for g in range(N_SSD_GROUPS):
        b_g = bm[:, g * SSD_STATE:(g + 1) * SSD_STATE]
        c_g = cm[:, g * SSD_STATE:(g + 1) * SSD_STATE]
        c_gb = c_g.astype(BF16)
        cb = _dot_nt(c_gb, b_g.astype(BF16))
        b_gt = b_g.T
        for r in range(HEADS_PER_GROUP):
            h = g * HEADS_PER_GROUP + r
            xs_h = xs[:, h * SSD_HEAD_DIM:(h + 1) * SSD_HEAD_DIM]
            xs_hb = xs_h.astype(BF16)
            acs_col = acs_c[:, h:h + 1]
            acs_row = acs_r[h:h + 1, :]
            dt_row = dt_r[h:h + 1, :]
            seg = jnp.exp(jnp.where(causal, acs_col - acs_row, NEG_BIG))
            w_diag = (cb * seg * dt_row).astype(BF16)
            st = state[h]
            y_h = (_dot(w_diag, xs_hb)
                   + e_c[:, h:h + 1] * _dot(c_gb, st.astype(BF16))
                   + dskip_ref[:, h * SSD_HEAD_DIM:(h + 1) * SSD_HEAD_DIM] * xs_h)
            acs_last = acs_row[:, L - 1:L]
            w_state = jnp.exp(acs_last - acs_row) * dt_row
            state[h] = jnp.exp(acs_last) * st + _dot((b_gt * w_state).astype(BF16), xs_hb)
            ys.append(y_h)
    y = jnp.concatenate(ys, axis=1) * _silu(z_ref[...])

    gw = SSD_WIDTH // N_SSD_GROUPS
    outs = []
    for g in range(N_SSD_GROUPS):
        yg = y[:, g * gw:(g + 1) * gw]
        outs.append(yg * lax.rsqrt(jnp.mean(yg * yg, axis=-1, keepdims=True) + EPS))
    y_ref[...] = (jnp.concatenate(outs, axis=1) * nw_ref[...]).astype(y_ref.dtype)


def _ssd(xbc, z, dtc, dtr, conv_w, conv_b, dtb_row, dtb_col, alog_row, alog_col, dskip, nw):
    b, s, _ = xbc.shape
    nc = s // CHUNK
    tok = lambda width: pl.BlockSpec((None, CHUNK, width), lambda i, c: (i, c, 0))
    return pl.pallas_call(
        _ssd_kernel,
        grid=(b, nc),
        in_specs=[
            tok(CONV_CHANNELS), tok(SSD_WIDTH), tok(LANES),
            pl.BlockSpec((None, DT_ROWS, CHUNK), lambda i, c: (i, 0, c)),
            _resident(conv_w.shape), _resident(conv_b.shape),
            _resident(dtb_row.shape), _resident(dtb_col.shape),
            _resident(alog_row.shape), _resident(alog_col.shape),
            _resident(dskip.shape), _resident(nw.shape),
        ],
        out_specs=tok(SSD_WIDTH),
        out_shape=jax.ShapeDtypeStruct((b, s, SSD_WIDTH), BF16),
        scratch_shapes=[
            pltpu.VMEM((8 + CHUNK, CONV_CHANNELS), F32),
            pltpu.VMEM((N_SSD_HEADS, SSD_STATE, SSD_HEAD_DIM), F32),
        ],
        compiler_params=pltpu.CompilerParams(
            dimension_semantics=("parallel", "arbitrary"), vmem_limit_bytes=VMEM_LIMIT),
        name="ssd",
    )(xbc, z, dtc, dtr, conv_w, conv_b, dtb_row, dtb_col, alog_row, alog_col, dskip, nw)


def _out_mlp_kernel(x_ref, oa_ref, ys_ref, woa_ref, wos_ref, nw_ref, wup_ref, wdn_ref, o_ref):
    x1 = x_ref[...] + _dot(oa_ref[...], woa_ref[...]) + _dot(ys_ref[...], wos_ref[...])
    hm = x1 * lax.rsqrt(jnp.mean(x1 * x1, axis=-1, keepdims=True) + EPS) * nw_ref[...]
    up = _dot(hm.astype(BF16), wup_ref[...])
    act = jnp.square(jnp.maximum(up, 0.0)).astype(BF16)
    o_ref[...] = x1 + _dot(act, wdn_ref[...])


def _out_mlp(x, o_attn, y_ssd, wo_a, wo_s, mlp_norm_w, w_up, w_down, tm):
    t, d = x.shape
    tok = lambda width: pl.BlockSpec((tm, width), lambda i: (i, 0))
    return pl.pallas_call(
        _out_mlp_kernel,
        grid=(t // tm,),
        in_specs=[
            tok(d), tok(ATTN_WIDTH), tok(SSD_WIDTH),
            _resident(wo_a.shape), _resident(wo_s.shape), _resident((1, d)),
            _resident(w_up.shape), _resident(w_down.shape),
        ],
        out_specs=tok(d),
        out_shape=jax.ShapeDtypeStruct((t, d), F32),
        compiler_params=pltpu.CompilerParams(
            dimension_semantics=("parallel",), vmem_limit_bytes=VMEM_LIMIT),
        name="out_mlp",
    )(x, o_attn, y_ssd, wo_a, wo_s, mlp_norm_w, w_up, w_down)


def _rope_tables(s):
    half = HEAD_DIM // 2
    inv_freq = ROPE_THETA ** (-jnp.arange(half, dtype=F32) / half)
    lane_freq = jnp.tile(inv_freq, LANES // half)
    sign = jnp.tile(jnp.repeat(jnp.array([-1.0, 1.0], F32), half), LANES // HEAD_DIM)
    ang = jnp.arange(s, dtype=F32)[:, None] * lane_freq[None, :]
    return jnp.cos(ang), jnp.sin(ang) * sign[None, :]


def _pad_lanes(row):
    return jnp.pad(row, ((0, 0), (0, LANES - row.shape[1])))


def _layer(x, attn_norm_w, w_in, q_norm_w, k_norm_w, conv_w, conv_b, dt_bias, a_log, d_skip,
           ssd_norm_w, w_out, mlp_norm_w, w_up, w_down):
    b, s, d = x.shape
    w_main = w_in[:, :MAIN_WIDTH].astype(BF16)
    w_dt = w_in[:, MAIN_WIDTH:]
    w_dt_c = jnp.pad(w_dt, ((0, 0), (0, LANES - N_SSD_HEADS))).astype(BF16)
    head_of = jnp.arange(ATTN_WIDTH, dtype=jnp.int32) // HEAD_DIM
    group_mean = jnp.where(head_of[:, None] == head_of[None, :], 1.0 / HEAD_DIM, 0.0).astype(BF16)
    cos, sin = _rope_tables(s)
    q, k, v, z, xbc, dtc, dtr = _in_proj(
        x, attn_norm_w[None, :], w_main, w_dt_c, group_mean,
        jnp.tile(q_norm_w, N_ATTN_HEADS)[None, :], jnp.tile(k_norm_w, N_ATTN_HEADS)[None, :],
        cos, sin, tm=PROJ_TOKENS)

    o_attn = _attention(q, k, v)

    col = lambda p: jnp.pad(p[:, None], ((0, DT_ROWS - N_SSD_HEADS), (0, 0)))
    y_ssd = _ssd(xbc, z, dtc, dtr, conv_w, conv_b[None, :],
                 _pad_lanes(dt_bias[None, :]), col(dt_bias),
                 _pad_lanes(a_log[None, :]), col(a_log),
                 jnp.repeat(d_skip, SSD_HEAD_DIM)[None, :], ssd_norm_w[None, :])

    out = _out_mlp(x.reshape(b * s, d), o_attn.reshape(b * s, ATTN_WIDTH),
                   y_ssd.reshape(b * s, SSD_WIDTH),
                   w_out[:ATTN_WIDTH].astype(BF16), w_out[ATTN_WIDTH:].astype(BF16),
                   mlp_norm_w[None, :], w_up.astype(BF16), w_down.astype(BF16), tm=MLP_TOKENS)
    return out.reshape(b, s, d)


def kernel(x, attn_norm_w, w_in, q_norm_w, k_norm_w, conv_w, conv_b, dt_bias, a_log, d_skip,
           ssd_norm_w, w_out, mlp_norm_w, w_up, w_down):
    for i in range(attn_norm_w.shape[0]):
        x = _layer(x, attn_norm_w[i], w_in[i], q_norm_w[i], k_norm_w[i], conv_w[i], conv_b[i],
                   dt_bias[i], a_log[i], d_skip[i], ssd_norm_w[i], w_out[i], mlp_norm_w[i],
                   w_up[i], w_down[i])
    return x
```

```python
import functools

import jax
import jax.numpy as jnp
from jax import lax
from jax.experimental import pallas as pl
from jax.experimental.pallas import tpu as pltpu

F32 = jnp.float32
BF16 = jnp.bfloat16

HEAD_DIM = 64
N_ATTN_HEADS = 8
ATTN_WIDTH = N_ATTN_HEADS * HEAD_DIM
ROPE_THETA = 10000.0
Q_BLOCK = 128
DILATIONS = (1, 4, 16)
FAR_DIL = 4
FAR_RATIO = 4
ATTN_LOOKAHEAD = 5
SSD_HEAD_DIM = 64
N_SSD_HEADS = 8
SSD_WIDTH = N_SSD_HEADS * SSD_HEAD_DIM
N_SSD_GROUPS = 2
HEADS_PER_GROUP = N_SSD_HEADS // N_SSD_GROUPS
SSD_STATE = 128
CONV_WIDTH = 4
CHUNK = 128
CONV_CHANNELS = SSD_WIDTH + 2 * N_SSD_GROUPS * SSD_STATE
EPS = 1e-6
NEG_BIG = -1e30
SCORE_SCALE = HEAD_DIM ** -0.5 * 1.4426950408889634

LANES = 128
DT_ROWS = 16
MAIN_WIDTH = 3 * ATTN_WIDTH + SSD_WIDTH + CONV_CHANNELS
VMEM_LIMIT = 56 * 1024 * 1024
PROJ_TOKENS = 512
MLP_TOKENS = 512


def _dot(a, b):
    return jnp.dot(a, b, preferred_element_type=F32)


def _dot_nt(a, b):
    return lax.dot_general(a, b, (((1,), (1,)), ((), ())), preferred_element_type=F32)


def _split_dot(a, b, parts, lhs_split=True):
    src = a if lhs_split else b
    acc = None
    rem = src
    for _ in range(parts):
        piece = rem.astype(BF16)
        rem = rem - piece.astype(F32)
        term = _dot(piece, b) if lhs_split else _dot(a, piece)
        acc = term if acc is None else acc + term
    return acc


def _resident(shape):
    zeros = (0,) * len(shape)
    return pl.BlockSpec(shape, lambda *_: zeros, pipeline_mode=pl.Buffered(1))


def _in_proj_kernel(x_ref, nw_ref, w_ref, wdt_ref, gm_ref, qw_ref, kw_ref, cos_ref, sin_ref,
                    q_ref, k_ref, v_ref, z_ref, xbc_ref, dtc_ref, dtr_ref):
    x = x_ref[...]
    h = x * lax.rsqrt(jnp.mean(x * x, axis=-1, keepdims=True) + EPS) * nw_ref[...]
    hb = h.astype(BF16)

    lane = lax.broadcasted_iota(jnp.int32, (x.shape[0], ATTN_WIDTH), 1)
    first_half = (lane % HEAD_DIM) < (HEAD_DIM // 2)
    cos = jnp.concatenate([cos_ref[...]] * (ATTN_WIDTH // LANES), axis=1)
    sin = jnp.concatenate([sin_ref[...]] * (ATTN_WIDTH // LANES), axis=1)

    def head_norm_rope(t, w):
        ms = _dot((t * t).astype(BF16), gm_ref[...])
        n = t * lax.rsqrt(ms + EPS) * w
        partner = jnp.where(first_half,
                            pltpu.roll(n, ATTN_WIDTH - HEAD_DIM // 2, 1),
                            pltpu.roll(n, HEAD_DIM // 2, 1))
        return n * cos + partner * sin

    o = 0
    q = _dot(hb, w_ref[:, o:o + ATTN_WIDTH]); o += ATTN_WIDTH
    q_ref[...] = (head_norm_rope(q, qw_ref[...]) * SCORE_SCALE).astype(BF16)
    k = _dot(hb, w_ref[:, o:o + ATTN_WIDTH]); o += ATTN_WIDTH
    k_ref[...] = head_norm_rope(k, kw_ref[...]).astype(BF16)
    v_ref[...] = _dot(hb, w_ref[:, o:o + ATTN_WIDTH]).astype(BF16); o += ATTN_WIDTH
    z_ref[...] = _dot(hb, w_ref[:, o:o + SSD_WIDTH]); o += SSD_WIDTH
    xbc_ref[...] = _dot(hb, w_ref[:, o:o + CONV_CHANNELS])
    dtc = _dot(hb, wdt_ref[...])
    dtc_ref[...] = dtc
    dtr_ref[...] = dtc.T[:DT_ROWS, :]


def _in_proj(x, attn_norm_w, w_main, w_dt, group_mean, qw, kw, cos, sin, tm):
    b, s, d = x.shape
    grid = (s // tm, b)
    tok = lambda width: pl.BlockSpec((None, tm, width), lambda j, i: (i, j, 0))
    out_shapes = (
        jax.ShapeDtypeStruct((b, s, ATTN_WIDTH), BF16),
        jax.ShapeDtypeStruct((b, s, ATTN_WIDTH), BF16),
        jax.ShapeDtypeStruct((b, s, ATTN_WIDTH), BF16),
        jax.ShapeDtypeStruct((b, s, SSD_WIDTH), F32),
        jax.ShapeDtypeStruct((b, s, CONV_CHANNELS), F32),
        jax.ShapeDtypeStruct((b, s, LANES), F32),
        jax.ShapeDtypeStruct((b, DT_ROWS, s), F32),
    )
    return pl.pallas_call(
        _in_proj_kernel,
        grid=grid,
        in_specs=[
            tok(d),
            _resident((1, d)),
            _resident(w_main.shape),
            _resident(w_dt.shape),
            _resident(group_mean.shape),
            _resident((1, ATTN_WIDTH)),
            _resident((1, ATTN_WIDTH)),
            pl.BlockSpec((tm, LANES), lambda j, i: (j, 0)),
            pl.BlockSpec((tm, LANES), lambda j, i: (j, 0)),
        ],
        out_specs=(
            tok(ATTN_WIDTH), tok(ATTN_WIDTH), tok(ATTN_WIDTH), tok(SSD_WIDTH),
            tok(CONV_CHANNELS), tok(LANES),
            pl.BlockSpec((None, DT_ROWS, tm), lambda j, i: (i, 0, j)),
        ),
        out_shape=out_shapes,
        compiler_params=pltpu.CompilerParams(
            dimension_semantics=("parallel", "parallel"), vmem_limit_bytes=VMEM_LIMIT),
        name="in_proj",
    )(x, attn_norm_w, w_main, w_dt, group_mean, qw, kw, cos, sin)


def _attn_blocks(blocks, store):
    qb = Q_BLOCK
    n_pairs = ATTN_WIDTH // LANES
    head0 = lax.broadcasted_iota(jnp.int32, (qb, LANES), 1) < HEAD_DIM
    items = [(i, pair) for i in range(len(blocks)) for pair in range(n_pairs)]

    def scores(item):
        i, pair = item
        load_q, load_k, _, load_bias = blocks[i]
        qt, kb, bias = load_q(pair), load_k(pair), load_bias()
        out = []
        for h in range(2):
            mine = head0 if h == 0 else jnp.logical_not(head0)
            qh = jnp.where(mine, qt, jnp.zeros_like(qt))
            out.append(_dot_nt(qh, kb) + bias)
        return out

    queue = [scores(item) for item in items[:ATTN_LOOKAHEAD]]
    parts = []
    for n, (i, pair) in enumerate(items):
        if n + ATTN_LOOKAHEAD < len(items):
            queue.append(scores(items[n + ATTN_LOOKAHEAD]))
        pending = queue.pop(0)
        vb = blocks[i][2](pair)
        v_ext = jnp.concatenate([vb, jnp.ones_like(vb)], axis=1)
        heads = []
        for sc in pending:
            m = jnp.max(sc, axis=1, keepdims=True)
            r = _dot(jnp.exp2(sc - m).astype(BF16), v_ext)
            heads.append((r[:, :LANES], r[:, LANES:], jnp.broadcast_to(m, (qb, LANES))))
        parts.append([jnp.where(head0, a, b) for a, b in zip(*heads)])
        if pair == n_pairs - 1:
            store(i, *[jnp.concatenate(t, axis=1) for t in zip(*parts)])
            parts = []


def _near_bias(first):
    qb = Q_BLOCK
    row = lax.broadcasted_iota(jnp.int32, (qb, 2 * qb), 0)
    col = lax.broadcasted_iota(jnp.int32, (qb, 2 * qb), 1)
    ok = (col <= row) if first else ((col >= row) & (col <= row + qb))
    return jnp.where(ok, 0.0, NEG_BIG)


def _far_bias(blocks_back):
    qb = Q_BLOCK
    row = lax.broadcasted_iota(jnp.int32, (qb, qb), 0)
    col = lax.broadcasted_iota(jnp.int32, (qb, qb), 1)
    dist = row - col + blocks_back * qb
    in_window = (dist >= 0) & (dist <= qb)
    on_stride = (dist >= 0) & ((dist & (FAR_RATIO - 1)) == 0)
    return jnp.where(in_window & on_stride, 1.0,
                     jnp.where(in_window | on_stride, 0.0, NEG_BIG))


def _attn_kernel(q_ref, k_ref, v_ref, o_ref,
                 tmp, q_cls, k_cls, v_cls, num_far, den_far, top_far, near_bias, far_bias):
    qb = Q_BLOCK
    step = pl.program_id(1)
    rows = q_ref.shape[0]
    s = k_ref.shape[0]
    n_steps = s // rows
    n_pairs = ATTN_WIDTH // LANES
    lanes = lambda pair: pl.ds(pair * LANES, LANES)
    far = (num_far, den_far, top_far)

    def to_class_order(src_rows, dst, at):
        for p in range(n_pairs):
            tmp[p] = src_rows[:, p * LANES:(p + 1) * LANES].astype(F32)
            for c in range(FAR_DIL):
                dst[c, at:at + qb, lanes(p)] = tmp[p, pl.ds(c, qb, stride=FAR_DIL), :].astype(BF16)

    @pl.when(step == 0)
    def _():
        for i in range(n_steps):
            to_class_order(k_ref[i * rows:(i + 1) * rows, :], k_cls, i * qb)
            to_class_order(v_ref[i * rows:(i + 1) * rows, :], v_cls, i * qb)

    to_class_order(q_ref[...], q_cls, 0)
    near_bias[0] = _near_bias(False)
    near_bias[1] = _near_bias(True)
    for d in range(3):
        far_bias[d] = _far_bias(d)

    def far_store(c, *stats):
        for dst, t in zip(far, stats):
            for p in range(n_pairs):
                dst[p, pl.ds(c, qb, stride=FAR_DIL), :] = t[:, p * LANES:(p + 1) * LANES]

    for n_kb in range(1, n_steps + 1):
        @pl.when(step == n_kb - 1)
        def _(n_kb=n_kb):
            def block(c):
                load_bias = lambda: jnp.concatenate(
                    [far_bias[min(n_kb - 1 - kb, 2)] for kb in range(n_kb)], axis=1)
                return (lambda p: q_cls[c, :, lanes(p)],
                        lambda p: k_cls[c, 0:n_kb * qb, lanes(p)],
                        lambda p: v_cls[c, 0:n_kb * qb, lanes(p)], load_bias)
            _attn_blocks([block(c) for c in range(FAR_DIL)], far_store)

    is_first = (step == 0).astype(jnp.int32)

    def near_block(j):
        q_rows = pl.ds(j * qb, qb)
        if j == 0:
            start = pl.multiple_of((step * FAR_DIL - 1 + is_first) * qb, qb)
            load_bias = lambda: near_bias[is_first]
        else:
            start = pl.multiple_of((step * FAR_DIL + j - 1) * qb, qb)
            load_bias = lambda: near_bias[0]
        kv_rows = pl.ds(start, 2 * qb)
        return (lambda p: q_ref[q_rows, lanes(p)], lambda p: k_ref[kv_rows, lanes(p)],
                lambda p: v_ref[kv_rows, lanes(p)], load_bias)

    def near_store(j, num, den, top):
        q_rows = pl.ds(j * qb, qb)
        num_f, den_f, top_f = [
            jnp.concatenate([t[p, q_rows, :] for p in range(n_pairs)], axis=1) for t in far]
        both = jnp.maximum(top, top_f)
        w_near, w_far = jnp.exp2(top - both), jnp.exp2(top_f - both)
        o_ref[q_rows, :] = ((w_near * num + w_far * num_f)
                            / (w_near * den + w_far * den_f)).astype(o_ref.dtype)

    _attn_blocks([near_block(j) for j in range(rows // qb)], near_store)


def _attention(q, k, v):
    b, s, w = q.shape
    assert DILATIONS == (1, FAR_DIL, FAR_DIL * FAR_RATIO) and s == Q_BLOCK * DILATIONS[-1]
    rows = FAR_DIL * Q_BLOCK
    q_blk = pl.BlockSpec((None, rows, w), lambda i, g: (i, g, 0))
    kv_blk = pl.BlockSpec((None, s, w), lambda i, g: (i, 0, 0))
    return pl.pallas_call(
        _attn_kernel,
        grid=(b, s // rows),
        in_specs=[q_blk, kv_blk, kv_blk],
        out_specs=q_blk,
        out_shape=jax.ShapeDtypeStruct((b, s, w), BF16),
        scratch_shapes=[
            pltpu.VMEM((w // LANES, rows, LANES), F32),
            pltpu.VMEM((FAR_DIL, Q_BLOCK, w), BF16),
            pltpu.VMEM((FAR_DIL, s // FAR_DIL, w), BF16),
            pltpu.VMEM((FAR_DIL, s // FAR_DIL, w), BF16),
            pltpu.VMEM((w // LANES, rows, LANES), F32),
            pltpu.VMEM((w // LANES, rows, LANES), F32),
            pltpu.VMEM((w // LANES, rows, LANES), F32),
            pltpu.VMEM((2, Q_BLOCK, 2 * Q_BLOCK), F32),
            pltpu.VMEM((3, Q_BLOCK, Q_BLOCK), F32),
        ],
        compiler_params=pltpu.CompilerParams(
            dimension_semantics=("parallel", "arbitrary"), vmem_limit_bytes=VMEM_LIMIT),
        name="dilated_attn",
    )(q, k, v)


def _softplus(x):
    return jnp.maximum(x, 0.0) + jnp.log1p(jnp.exp(-jnp.abs(x)))


def _silu(x):
    return x * (1.0 / (1.0 + jnp.exp(-x)))


def _ssd_kernel(xbc_ref, z_ref, dtc_ref, dtr_ref, cw_ref, cb_ref, dtb_row_ref, dtb_col_ref,
                alog_row_ref, alog_col_ref, dskip_ref, nw_ref, y_ref, ubuf, state):
    L = CHUNK
    pad = 8
    c = pl.program_id(1)

    @pl.when(c == 0)
    def _():
        ubuf[0:pad, :] = jnp.zeros((pad, CONV_CHANNELS), F32)
        state[...] = jnp.zeros(state.shape, F32)

    u = xbc_ref[...]
    ubuf[pad:pad + L, :] = u
    conv = cb_ref[...] + cw_ref[CONV_WIDTH - 1:CONV_WIDTH, :] * u
    for tap in range(CONV_WIDTH - 1):
        shift = CONV_WIDTH - 1 - tap
        conv = conv + cw_ref[tap:tap + 1, :] * ubuf[pad - shift:pad - shift + L, :]
    ubuf[0:pad, :] = u[L - pad:L, :]
    act = _silu(conv)
    xs = act[:, :SSD_WIDTH]
    gs = N_SSD_GROUPS * SSD_STATE
    bm = act[:, SSD_WIDTH:SSD_WIDTH + gs]
    cm = act[:, SSD_WIDTH + gs:SSD_WIDTH + 2 * gs]

    dt_c = _softplus(dtc_ref[...] + dtb_row_ref[...])
    dt_r = _softplus(dtr_ref[...] + dtb_col_ref[...])
    a_row = -jnp.exp(alog_row_ref[...])
    a_col = -jnp.exp(alog_col_ref[...])
    ri = lax.broadcasted_iota(jnp.int32, (L, L), 0)
    ci = lax.broadcasted_iota(jnp.int32, (L, L), 1)
    causal = ci <= ri
    tril = jnp.where(causal, 1.0, 0.0).astype(BF16)
    triu = jnp.where(ri <= ci, 1.0, 0.0).astype(BF16)
    acs_c = _split_dot(tril, dt_c * a_row, 3, lhs_split=False)
    acs_r = _split_dot(dt_r * a_col, triu, 3)
    e_c = jnp.exp(acs_c)

    ys = []
    for g in range(N_SSD_GROUPS):
        b_g = bm[:, g * SSD_STATE:(g + 1) * SSD_STATE]
        c_g = cm[:, g * SSD_STATE:(g + 1) * SSD_STATE]
        c_gb = c_g.astype(BF16)
        cb = _dot_nt(c_gb, b_g.astype(BF16))
        b_gt = b_g.T
        for r in range(HEADS_PER_GROUP):
            h = g * HEADS_PER_GROUP + r
            xs_h = xs[:, h * SSD_HEAD_DIM:(h + 1) * SSD_HEAD_DIM]
            xs_hb = xs_h.astype(BF16)
            acs_col = acs_c[:, h:h + 1]
            acs_row = acs_r[h:h + 1, :]
            dt_row = dt_r[h:h + 1, :]
            seg = jnp.exp(jnp.where(causal, acs_col - acs_row, NEG_BIG))
            w_diag = (cb * seg * dt_row).astype(BF16)
            st = state[h]
            y_h = (_dot(w_diag, xs_hb)
                   + e_c[:, h:h + 1] * _dot(c_gb, st.astype(BF16))
                   + dskip_ref[:, h * SSD_HEAD_DIM:(h + 1) * SSD_HEAD_DIM] * xs_h)
            acs_last = acs_row[:, L - 1:L]
            w_state = jnp.exp(acs_last - acs_row) * dt_row
            state[h] = jnp.exp(acs_last) * st + _dot((b_gt * w_state).astype(BF16), xs_hb)
            ys.append(y_h)
    y = jnp.concatenate(ys, axis=1) * _silu(z_ref[...])

    gw = SSD_WIDTH // N_SSD_GROUPS
    outs = []
    for g in range(N_SSD_GROUPS):
        yg = y[:, g * gw:(g + 1) * gw]
        outs.append(yg * lax.rsqrt(jnp.mean(yg * yg, axis=-1, keepdims=True) + EPS))
    y_ref[...] = (jnp.concatenate(outs, axis=1) * nw_ref[...]).astype(y_ref.dtype)


def _ssd(xbc, z, dtc, dtr, conv_w, conv_b, dtb_row, dtb_col, alog_row, alog_col, dskip, nw):
    b, s, _ = xbc.shape
    nc = s // CHUNK
    tok = lambda width: pl.BlockSpec((None, CHUNK, width), lambda i, c: (i, c, 0))
    return pl.pallas_call(
        _ssd_kernel,
        grid=(b, nc),
        in_specs=[
            tok(CONV_CHANNELS), tok(SSD_WIDTH), tok(LANES),
            pl.BlockSpec((None, DT_ROWS, CHUNK), lambda i, c: (i, 0, c)),
            _resident(conv_w.shape), _resident(conv_b.shape),
            _resident(dtb_row.shape), _resident(dtb_col.shape),
            _resident(alog_row.shape), _resident(alog_col.shape),
            _resident(dskip.shape), _resident(nw.shape),
        ],
        out_specs=tok(SSD_WIDTH),
        out_shape=jax.ShapeDtypeStruct((b, s, SSD_WIDTH), BF16),
        scratch_shapes=[
            pltpu.VMEM((8 + CHUNK, CONV_CHANNELS), F32),
            pltpu.VMEM((N_SSD_HEADS, SSD_STATE, SSD_HEAD_DIM), F32),
        ],
        compiler_params=pltpu.CompilerParams(
            dimension_semantics=("parallel", "arbitrary"), vmem_limit_bytes=VMEM_LIMIT),
        name="ssd",
    )(xbc, z, dtc, dtr, conv_w, conv_b, dtb_row, dtb_col, alog_row, alog_col, dskip, nw)


def _out_mlp_kernel(x_ref, oa_ref, ys_ref, woa_ref, wos_ref, nw_ref, wup_ref, wdn_ref, o_ref):
    x1 = x_ref[...] + _dot(oa_ref[...], woa_ref[...]) + _dot(ys_ref[...], wos_ref[...])
    hm = x1 * lax.rsqrt(jnp.mean(x1 * x1, axis=-1, keepdims=True) + EPS) * nw_ref[...]
    up = _dot(hm.astype(BF16), wup_ref[...])
    act = jnp.square(jnp.maximum(up, 0.0)).astype(BF16)
    o_ref[...] = x1 + _dot(act, wdn_ref[...])


def _out_mlp(x, o_attn, y_ssd, wo_a, wo_s, mlp_norm_w, w_up, w_down, tm):
    t, d = x.shape
    tok = lambda width: pl.BlockSpec((tm, width), lambda i: (i, 0))
    return pl.pallas_call(
        _out_mlp_kernel,
        grid=(t // tm,),
        in_specs=[
            tok(d), tok(ATTN_WIDTH), tok(SSD_WIDTH),
            _resident(wo_a.shape), _resident(wo_s.shape), _resident((1, d)),
            _resident(w_up.shape), _resident(w_down.shape),
        ],
        out_specs=tok(d),
        out_shape=jax.ShapeDtypeStruct((t, d), F32),
        compiler_params=pltpu.CompilerParams(
            dimension_semantics=("parallel",), vmem_limit_bytes=VMEM_LIMIT),
        name="out_mlp",
    )(x, o_attn, y_ssd, wo_a, wo_s, mlp_norm_w, w_up, w_down)


def _rope_tables(s):
    half = HEAD_DIM // 2
    inv_freq = ROPE_THETA ** (-jnp.arange(half, dtype=F32) / half)
    lane_freq = jnp.tile(inv_freq, LANES // half)
    sign = jnp.tile(jnp.repeat(jnp.array([-1.0, 1.0], F32), half), LANES // HEAD_DIM)
    ang = jnp.arange(s, dtype=F32)[:, None] * lane_freq[None, :]
    return jnp.cos(ang), jnp.sin(ang) * sign[None, :]


def _pad_lanes(row):
    return jnp.pad(row, ((0, 0), (0, LANES - row.shape[1])))


def _layer(x, attn_norm_w, w_in, q_norm_w, k_norm_w, conv_w, conv_b, dt_bias, a_log, d_skip,
           ssd_norm_w, w_out, mlp_norm_w, w_up, w_down):
    b, s, d = x.shape
    w_main = w_in[:, :MAIN_WIDTH].astype(BF16)
    w_dt = w_in[:, MAIN_WIDTH:]
    w_dt_c = jnp.pad(w_dt, ((0, 0), (0, LANES - N_SSD_HEADS))).astype(BF16)
    head_of = jnp.arange(ATTN_WIDTH, dtype=jnp.int32) // HEAD_DIM
    group_mean = jnp.where(head_of[:, None] == head_of[None, :], 1.0 / HEAD_DIM, 0.0).astype(BF16)
    cos, sin = _rope_tables(s)
    q, k, v, z, xbc, dtc, dtr = _in_proj(
        x, attn_norm_w[None, :], w_main, w_dt_c, group_mean,
        jnp.tile(q_norm_w, N_ATTN_HEADS)[None, :], jnp.tile(k_norm_w, N_ATTN_HEADS)[None, :],
        cos, sin, tm=PROJ_TOKENS)

    o_attn = _attention(q, k, v)

    col = lambda p: jnp.pad(p[:, None], ((0, DT_ROWS - N_SSD_HEADS), (0, 0)))
    y_ssd = _ssd(xbc, z, dtc, dtr, conv_w, conv_b[None, :],
                 _pad_lanes(dt_bias[None, :]), col(dt_bias),
                 _pad_lanes(a_log[None, :]), col(a_log),
                 jnp.repeat(d_skip, SSD_HEAD_DIM)[None, :], ssd_norm_w[None, :])

    out = _out_mlp(x.reshape(b * s, d), o_attn.reshape(b * s, ATTN_WIDTH),
                   y_ssd.reshape(b * s, SSD_WIDTH),
                   w_out[:ATTN_WIDTH].astype(BF16), w_out[ATTN_WIDTH:].astype(BF16),
                   mlp_norm_w[None, :], w_up.astype(BF16), w_down.astype(BF16), tm=MLP_TOKENS)
    return out.reshape(b, s, d)


def kernel(x, attn_norm_w, w_in, q_norm_w, k_norm_w, conv_w, conv_b, dt_bias, a_log, d_skip,
           ssd_norm_w, w_out, mlp_norm_w, w_up, w_down):
    for i in range(attn_norm_w.shape[0]):
        x = _layer(x, attn_norm_w[i], w_in[i], q_norm_w[i], k_norm_w[i], conv_w[i], conv_b[i],
                   dt_bias[i], a_log[i], d_skip[i], ssd_norm_w[i], w_out[i], mlp_norm_w[i],
                   w_up[i], w_down[i])
    return x
```

```python
import functools

import jax
import jax.numpy as jnp
from jax import lax
from jax.experimental import pallas as pl
from jax.experimental.pallas import tpu as pltpu

F32 = jnp.float32
BF16 = jnp.bfloat16

HEAD_DIM = 64
N_ATTN_HEADS = 8
ATTN_WIDTH = N_ATTN_HEADS * HEAD_DIM
ROPE_THETA = 10000.0
Q_BLOCK = 128
DILATIONS = (1, 4, 16)
FAR_DIL = 4
FAR_RATIO = 4
ATTN_LOOKAHEAD = 5
SSD_HEAD_DIM = 64
N_SSD_HEADS = 8
SSD_WIDTH = N_SSD_HEADS * SSD_HEAD_DIM
N_SSD_GROUPS = 2
HEADS_PER_GROUP = N_SSD_HEADS // N_SSD_GROUPS
SSD_STATE = 128
CONV_WIDTH = 4
CHUNK = 128
SSD_CHUNKS_PER_STEP = 4
CONV_CHANNELS = SSD_WIDTH + 2 * N_SSD_GROUPS * SSD_STATE
CONV_HISTORY = 16
EPS = 1e-6
NEG_BIG = -1e30
SCORE_SCALE = HEAD_DIM ** -0.5 * 1.4426950408889634

LANES = 128
DT_ROWS = 16
MAIN_WIDTH = 3 * ATTN_WIDTH + SSD_WIDTH + CONV_CHANNELS
VMEM_LIMIT = 56 * 1024 * 1024
PROJ_TOKENS = 512
MLP_TOKENS = 512


def _dot(a, b):
    return jnp.dot(a, b, preferred_element_type=F32)


def _dot_nt(a, b):
    return lax.dot_general(a, b, (((1,), (1,)), ((), ())), preferred_element_type=F32)


def _split_dot(a, b, parts, lhs_split=True):
    src = a if lhs_split else b
    acc = None
    rem = src
    for _ in range(parts):
        piece = rem.astype(BF16)
        rem = rem - piece.astype(F32)
        term = _dot(piece, b) if lhs_split else _dot(a, piece)
        acc = term if acc is None else acc + term
    return acc


def _resident(shape):
    zeros = (0,) * len(shape)
    return pl.BlockSpec(shape, lambda *_: zeros, pipeline_mode=pl.Buffered(1))


def _in_proj_kernel(x_ref, xprev_ref, nw_ref, w_ref, wdt_ref, gm_ref, qw_ref, kw_ref,
                    cos_ref, sin_ref, cw_ref, cb_ref, dtb_ref,
                    q_ref, k_ref, v_ref, gate_ref, act_ref, dtc_ref, dtr_ref, ubuf):
    tm = x_ref.shape[0]

    def normed(x):
        y = x * lax.rsqrt(jnp.mean(x * x, axis=-1, keepdims=True) + EPS) * nw_ref[...]
        return y.astype(BF16)

    hb = normed(x_ref[...])

    lane = lax.broadcasted_iota(jnp.int32, (tm, ATTN_WIDTH), 1)
    first_half = (lane % HEAD_DIM) < (HEAD_DIM // 2)
    cos = jnp.concatenate([cos_ref[...]] * (ATTN_WIDTH // LANES), axis=1)
    sin = jnp.concatenate([sin_ref[...]] * (ATTN_WIDTH // LANES), axis=1)

    def head_norm_rope(t, w):
        ms = _dot((t * t).astype(BF16), gm_ref[...])
        n = t * lax.rsqrt(ms + EPS) * w
        partner = jnp.where(first_half,
                            pltpu.roll(n, ATTN_WIDTH - HEAD_DIM // 2, 1),
                            pltpu.roll(n, HEAD_DIM // 2, 1))
        return n * cos + partner * sin

    o = 0
    q = _dot(hb, w_ref[:, o:o + ATTN_WIDTH]); o += ATTN_WIDTH
    q_ref[...] = (head_norm_rope(q, qw_ref[...]) * SCORE_SCALE).astype(BF16)
    k = _dot(hb, w_ref[:, o:o + ATTN_WIDTH]); o += ATTN_WIDTH
    k_ref[...] = head_norm_rope(k, kw_ref[...]).astype(BF16)
    v_ref[...] = _dot(hb, w_ref[:, o:o + ATTN_WIDTH]).astype(BF16); o += ATTN_WIDTH
    gate_ref[...] = _silu(_dot(hb, w_ref[:, o:o + SSD_WIDTH])); o += SSD_WIDTH

    w_xbc = w_ref[:, o:o + CONV_CHANNELS]
    u = _dot(hb, w_xbc)
    hist = _dot(normed(xprev_ref[...]), w_xbc)
    ubuf[0:CONV_HISTORY, :] = jnp.where(pl.program_id(0) == 0, 0.0, hist)
    ubuf[CONV_HISTORY:CONV_HISTORY + tm, :] = u
    conv = cb_ref[...] + cw_ref[CONV_WIDTH - 1:CONV_WIDTH, :] * u
    for tap in range(CONV_WIDTH - 1):
        first = CONV_HISTORY - (CONV_WIDTH - 1 - tap)
        conv = conv + cw_ref[tap:tap + 1, :] * ubuf[first:first + tm, :]
    act_ref[...] = _silu(conv)

    dtc = _softplus(_dot(hb, wdt_ref[...]) + dtb_ref[...])
    dtc_ref[...] = dtc
    dtr_ref[...] = dtc.T[:DT_ROWS, :]


def _in_proj(x, attn_norm_w, w_main, w_dt, group_mean, qw, kw, cos, sin, conv_w, conv_b, dt_bias,
             tm):
    b, s, d = x.shape
    grid = (s // tm, b)
    tok = lambda width: pl.BlockSpec((None, tm, width), lambda j, i: (i, j, 0))
    prev = pl.BlockSpec((None, CONV_HISTORY, d),
                        lambda j, i: (i, jnp.maximum(j * (tm // CONV_HISTORY) - 1, 0), 0))
    out_shapes = (
        jax.ShapeDtypeStruct((b, s, ATTN_WIDTH), BF16),
        jax.ShapeDtypeStruct((b, s, ATTN_WIDTH), BF16),
        jax.ShapeDtypeStruct((b, s, ATTN_WIDTH), BF16),
        jax.ShapeDtypeStruct((b, s, SSD_WIDTH), F32),
        jax.ShapeDtypeStruct((b, s, CONV_CHANNELS), F32),
        jax.ShapeDtypeStruct((b, s, LANES), F32),
        jax.ShapeDtypeStruct((b, DT_ROWS, s), F32),
    )
    return pl.pallas_call(
        _in_proj_kernel,
        grid=grid,
        in_specs=[
            tok(d),
            prev,
            _resident((1, d)),
            _resident(w_main.shape),
            _resident(w_dt.shape),
            _resident(group_mean.shape),
            _resident((1, ATTN_WIDTH)),
            _resident((1, ATTN_WIDTH)),
            pl.BlockSpec((tm, LANES), lambda j, i: (j, 0)),
            pl.BlockSpec((tm, LANES), lambda j, i: (j, 0)),
            _resident(conv_w.shape), _resident(conv_b.shape), _resident(dt_bias.shape),
        ],
        out_specs=(
            tok(ATTN_WIDTH), tok(ATTN_WIDTH), tok(ATTN_WIDTH), tok(SSD_WIDTH),
            tok(CONV_CHANNELS), tok(LANES),
            pl.BlockSpec((None, DT_ROWS, tm), lambda j, i: (i, 0, j)),
        ),
        out_shape=out_shapes,
        scratch_shapes=[pltpu.VMEM((CONV_HISTORY + tm, CONV_CHANNELS), F32)],
        compiler_params=pltpu.CompilerParams(
            dimension_semantics=("parallel", "parallel"), vmem_limit_bytes=VMEM_LIMIT),
        name="in_proj",
    )(x, x, attn_norm_w, w_main, w_dt, group_mean, qw, kw, cos, sin, conv_w, conv_b, dt_bias)


def _attn_blocks(blocks, store):
    qb = Q_BLOCK
    n_pairs = ATTN_WIDTH // LANES
    head0 = lax.broadcasted_iota(jnp.int32, (qb, LANES), 1) < HEAD_DIM
    items = [(i, pair) for i in range(len(blocks)) for pair in range(n_pairs)]

    def scores(item):
        i, pair = item
        load_q, load_k, _, load_bias = blocks[i]
        qt, kb, bias = load_q(pair), load_k(pair), load_bias()
        out = []
        for h in range(2):
            mine = head0 if h == 0 else jnp.logical_not(head0)
            qh = jnp.where(mine, qt, jnp.zeros_like(qt))
            out.append(_dot_nt(qh, kb) + bias)
        return out

    queue = [scores(item) for item in items[:ATTN_LOOKAHEAD]]
    parts = []
    for n, (i, pair) in enumerate(items):
        if n + ATTN_LOOKAHEAD < len(items):
            queue.append(scores(items[n + ATTN_LOOKAHEAD]))
        pending = queue.pop(0)
        vb = blocks[i][2](pair)
        v_ext = jnp.concatenate([vb, jnp.ones_like(vb)], axis=1)
        heads = []
        for sc in pending:
            m = jnp.max(sc, axis=1, keepdims=True)
            r = _dot(jnp.exp2(sc - m).astype(BF16), v_ext)
            heads.append((r[:, :LANES], r[:, LANES:], jnp.broadcast_to(m, (qb, LANES))))
        parts.append([jnp.where(head0, a, b) for a, b in zip(*heads)])
        if pair == n_pairs - 1:
            store(i, *[jnp.concatenate(t, axis=1) for t in zip(*parts)])
            parts = []


def _near_bias(first):
    qb = Q_BLOCK
    row = lax.broadcasted_iota(jnp.int32, (qb, 2 * qb), 0)
    col = lax.broadcasted_iota(jnp.int32, (qb, 2 * qb), 1)
    ok = (col <= row) if first else ((col >= row) & (col <= row + qb))
    return jnp.where(ok, 0.0, NEG_BIG)


def _far_bias(blocks_back):
    qb = Q_BLOCK
    row = lax.broadcasted_iota(jnp.int32, (qb, qb), 0)
    col = lax.broadcasted_iota(jnp.int32, (qb, qb), 1)
    dist = row - col + blocks_back * qb
    in_window = (dist >= 0) & (dist <= qb)
    on_stride = (dist >= 0) & ((dist & (FAR_RATIO - 1)) == 0)
    return jnp.where(in_window & on_stride, 1.0,
                     jnp.where(in_window | on_stride, 0.0, NEG_BIG))


def _attn_kernel(q_ref, k_ref, v_ref, o_ref,
                 tmp, q_cls, k_cls, v_cls, num_far, den_far, top_far, near_bias, far_bias):
    qb = Q_BLOCK
    step = pl.program_id(1)
    rows = q_ref.shape[0]
    s = k_ref.shape[0]
    n_steps = s // rows
    n_pairs = ATTN_WIDTH // LANES
    lanes = lambda pair: pl.ds(pair * LANES, LANES)
    far = (num_far, den_far, top_far)

    def to_class_order(src_rows, dst, at):
        for p in range(n_pairs):
            tmp[p] = src_rows[:, p * LANES:(p + 1) * LANES].astype(F32)
            for c in range(FAR_DIL):
                dst[c, at:at + qb, lanes(p)] = tmp[p, pl.ds(c, qb, stride=FAR_DIL), :].astype(BF16)

    @pl.when(step == 0)
    def _():
        for i in range(n_steps):
            to_class_order(k_ref[i * rows:(i + 1) * rows, :], k_cls, i * qb)
            to_class_order(v_ref[i * rows:(i + 1) * rows, :], v_cls, i * qb)

    to_class_order(q_ref[...], q_cls, 0)
    near_bias[0] = _near_bias(False)
    near_bias[1] = _near_bias(True)
    for d in range(3):
        far_bias[d] = _far_bias(d)

    def far_store(c, *stats):
        for dst, t in zip(far, stats):
            for p in range(n_pairs):
                dst[p, pl.ds(c, qb, stride=FAR_DIL), :] = t[:, p * LANES:(p + 1) * LANES]

    for n_kb in range(1, n_steps + 1):
        @pl.when(step == n_kb - 1)
        def _(n_kb=n_kb):
            def block(c):
                load_bias = lambda: jnp.concatenate(
                    [far_bias[min(n_kb - 1 - kb, 2)] for kb in range(n_kb)], axis=1)
                return (lambda p: q_cls[c, :, lanes(p)],
                        lambda p: k_cls[c, 0:n_kb * qb, lanes(p)],
                        lambda p: v_cls[c, 0:n_kb * qb, lanes(p)], load_bias)
            _attn_blocks([block(c) for c in range(FAR_DIL)], far_store)

    is_first = (step == 0).astype(jnp.int32)

    def near_block(j):
        q_rows = pl.ds(j * qb, qb)
        if j == 0:
            start = pl.multiple_of((step * FAR_DIL - 1 + is_first) * qb, qb)
            load_bias = lambda: near_bias[is_first]
        else:
            start = pl.multiple_of((step * FAR_DIL + j - 1) * qb, qb)
            load_bias = lambda: near_bias[0]
        kv_rows = pl.ds(start, 2 * qb)
        return (lambda p: q_ref[q_rows, lanes(p)], lambda p: k_ref[kv_rows, lanes(p)],
                lambda p: v_ref[kv_rows, lanes(p)], load_bias)

    def near_store(j, num, den, top):
        q_rows = pl.ds(j * qb, qb)
        num_f, den_f, top_f = [
            jnp.concatenate([t[p, q_rows, :] for p in range(n_pairs)], axis=1) for t in far]
        both = jnp.maximum(top, top_f)
        w_near, w_far = jnp.exp2(top - both), jnp.exp2(top_f - both)
        o_ref[q_rows, :] = ((w_near * num + w_far * num_f)
                            / (w_near * den + w_far * den_f)).astype(o_ref.dtype)

    _attn_blocks([near_block(j) for j in range(rows // qb)], near_store)


def _attention(q, k, v):
    b, s, w = q.shape
    assert DILATIONS == (1, FAR_DIL, FAR_DIL * FAR_RATIO) and s == Q_BLOCK * DILATIONS[-1]
    rows = FAR_DIL * Q_BLOCK
    q_blk = pl.BlockSpec((None, rows, w), lambda i, g: (i, g, 0))
    kv_blk = pl.BlockSpec((None, s, w), lambda i, g: (i, 0, 0))
    return pl.pallas_call(
        _attn_kernel,
        grid=(b, s // rows),
        in_specs=[q_blk, kv_blk, kv_blk],
        out_specs=q_blk,
        out_shape=jax.ShapeDtypeStruct((b, s, w), BF16),
        scratch_shapes=[
            pltpu.VMEM((w // LANES, rows, LANES), F32),
            pltpu.VMEM((FAR_DIL, Q_BLOCK, w), BF16),
            pltpu.VMEM((FAR_DIL, s // FAR_DIL, w), BF16),
            pltpu.VMEM((FAR_DIL, s // FAR_DIL, w), BF16),
            pltpu.VMEM((w // LANES, rows, LANES), F32),
            pltpu.VMEM((w // LANES, rows, LANES), F32),
            pltpu.VMEM((w // LANES, rows, LANES), F32),
            pltpu.VMEM((2, Q_BLOCK, 2 * Q_BLOCK), F32),
            pltpu.VMEM((3, Q_BLOCK, Q_BLOCK), F32),
        ],
        compiler_params=pltpu.CompilerParams(
            dimension_semantics=("parallel", "arbitrary"), vmem_limit_bytes=VMEM_LIMIT),
        name="dilated_attn",
    )(q, k, v)


def _softplus(x):
    return jnp.maximum(x, 0.0) + jnp.log1p(jnp.exp(-jnp.abs(x)))


def _silu(x):
    return x * (1.0 / (1.0 + jnp.exp(-x)))


def _ssd_kernel(act_ref, gate_ref, dtc_ref, dtr_ref, alog_row_ref, alog_col_ref, dskip_ref,
                nw_ref, y_ref, state):
    L = CHUNK

    @pl.when(pl.program_id(1) == 0)
    def _():
        state[...] = jnp.zeros(state.shape, F32)

    a_row = -jnp.exp(alog_row_ref[...])
    a_col = -jnp.exp(alog_col_ref[...])
    ri = lax.broadcasted_iota(jnp.int32, (L, L), 0)
    ci = lax.broadcasted_iota(jnp.int32, (L, L), 1)
    causal = ci <= ri
    tril = jnp.where(causal, 1.0, 0.0).astype(BF16)
    triu = jnp.where(ri <= ci, 1.0, 0.0).astype(BF16)
    gs = N_SSD_GROUPS * SSD_STATE
    gw = SSD_WIDTH // N_SSD_GROUPS

    n_pairs = SSD_WIDTH // LANES
    pairs_per_group = n_pairs // N_SSD_GROUPS
    head0 = lax.broadcasted_iota(jnp.int32, (L, LANES), 1) < SSD_HEAD_DIM
    head0_row = head0[:1, :]
    spread = (lax.broadcasted_iota(jnp.int32, (LANES, SSD_WIDTH), 0)
              == lax.broadcasted_iota(jnp.int32, (LANES, SSD_WIDTH), 1) // SSD_HEAD_DIM)
    spread = jnp.where(spread, 1.0, 0.0).astype(BF16)

    states = [state[p] for p in range(n_pairs)]
    for chunk in range(act_ref.shape[0] // L):
        rows = pl.ds(chunk * L, L)
        act = act_ref[rows, :]
        xs = act[:, :SSD_WIDTH]
        xs_b = xs.astype(BF16)
        bm = act[:, SSD_WIDTH:SSD_WIDTH + gs]
        cm = act[:, SSD_WIDTH + gs:SSD_WIDTH + 2 * gs]

        dt_c = dtc_ref[rows, :]
        dt_r = dtr_ref[:, chunk * L:(chunk + 1) * L]
        acs_c = _split_dot(tril, dt_c * a_row, 3, lhs_split=False)
        acs_r = _split_dot(dt_r * a_col, triu, 3)
        e_wide = _split_dot(jnp.exp(acs_c), spread, 2)

        ys = []
        for g in range(N_SSD_GROUPS):
            b_g = bm[:, g * SSD_STATE:(g + 1) * SSD_STATE]
            c_gb = cm[:, g * SSD_STATE:(g + 1) * SSD_STATE].astype(BF16)
            cb = _dot_nt(c_gb, b_g.astype(BF16))
            b_gt = b_g.T
            for q in range(pairs_per_group):
                p = g * pairs_per_group + q
                x_pair = xs_b[:, p * LANES:(p + 1) * LANES]
                zero = jnp.zeros_like(x_pair)
                x_split = jnp.concatenate([jnp.where(head0, x_pair, zero),
                                           jnp.where(head0, zero, x_pair)], axis=0)
                w_diag, w_state, last = [], [], []
                for h in (2 * p, 2 * p + 1):
                    acs_col = acs_c[:, h:h + 1]
                    acs_row = acs_r[h:h + 1, :]
                    dt_row = dt_r[h:h + 1, :]
                    seg = jnp.exp(jnp.where(causal, acs_col - acs_row, NEG_BIG))
                    w_diag.append((cb * seg * dt_row).astype(BF16))
                    acs_last = acs_row[:, L - 1:L]
                    w_state.append((b_gt * (jnp.exp(acs_last - acs_row) * dt_row)).astype(BF16))
                    last.append(jnp.exp(acs_last))
                y_pair = (_dot(jnp.concatenate(w_diag, axis=1), x_split)
                          + e_wide[:, p * LANES:(p + 1) * LANES]
                          * _dot(c_gb, states[p].astype(BF16)))
                states[p] = (jnp.where(head0_row, last[0], last[1]) * states[p]
                             + _dot(jnp.concatenate(w_state, axis=1), x_split))
                ys.append(y_pair)
        y = (jnp.concatenate(ys, axis=1) + dskip_ref[...] * xs) * gate_ref[rows, :]

        outs = []
        for g in range(N_SSD_GROUPS):
            yg = y[:, g * gw:(g + 1) * gw]
            outs.append(yg * lax.rsqrt(jnp.mean(yg * yg, axis=-1, keepdims=True) + EPS))
        y_ref[rows, :] = (jnp.concatenate(outs, axis=1) * nw_ref[...]).astype(y_ref.dtype)

    for p in range(n_pairs):
        state[p] = states[p]


def _ssd(act, gate, dtc, dtr, alog_row, alog_col, dskip, nw):
    b, s, _ = act.shape
    rows = SSD_CHUNKS_PER_STEP * CHUNK
    tok = lambda width: pl.BlockSpec((None, rows, width), lambda i, c: (i, c, 0))
    return pl.pallas_call(
        _ssd_kernel,
        grid=(b, s // rows),
        in_specs=[
            tok(CONV_CHANNELS), tok(SSD_WIDTH), tok(LANES),
            pl.BlockSpec((None, DT_ROWS, rows), lambda i, c: (i, 0, c)),
            _resident(alog_row.shape), _resident(alog_col.shape),
            _resident(dskip.shape), _resident(nw.shape),
        ],
        out_specs=tok(SSD_WIDTH),
        out_shape=jax.ShapeDtypeStruct((b, s, SSD_WIDTH), BF16),
        scratch_shapes=[pltpu.VMEM((SSD_WIDTH // LANES, SSD_STATE, LANES), F32)],
        compiler_params=pltpu.CompilerParams(
            dimension_semantics=("parallel", "arbitrary"), vmem_limit_bytes=VMEM_LIMIT),
        name="ssd",
    )(act, gate, dtc, dtr, alog_row, alog_col, dskip, nw)


def _out_mlp_kernel(x_ref, oa_ref, ys_ref, woa_ref, wos_ref, nw_ref, wup_ref, wdn_ref, o_ref):
    x1 = x_ref[...] + _dot(oa_ref[...], woa_ref[...]) + _dot(ys_ref[...], wos_ref[...])
    hm = x1 * lax.rsqrt(jnp.mean(x1 * x1, axis=-1, keepdims=True) + EPS) * nw_ref[...]
    up = _dot(hm.astype(BF16), wup_ref[...])
    act = jnp.square(jnp.maximum(up, 0.0)).astype(BF16)
    o_ref[...] = x1 + _dot(act, wdn_ref[...])


def _out_mlp(x, o_attn, y_ssd, wo_a, wo_s, mlp_norm_w, w_up, w_down, tm):
    t, d = x.shape
    tok = lambda width: pl.BlockSpec((tm, width), lambda i: (i, 0))
    return pl.pallas_call(
        _out_mlp_kernel,
        grid=(t // tm,),
        in_specs=[
            tok(d), tok(ATTN_WIDTH), tok(SSD_WIDTH),
            _resident(wo_a.shape), _resident(wo_s.shape), _resident((1, d)),
            _resident(w_up.shape), _resident(w_down.shape),
        ],
        out_specs=tok(d),
        out_shape=jax.ShapeDtypeStruct((t, d), F32),
        compiler_params=pltpu.CompilerParams(
            dimension_semantics=("parallel",), vmem_limit_bytes=VMEM_LIMIT),
        name="out_mlp",
    )(x, o_attn, y_ssd, wo_a, wo_s, mlp_norm_w, w_up, w_down)


def _rope_tables(s):
    half = HEAD_DIM // 2
    inv_freq = ROPE_THETA ** (-jnp.arange(half, dtype=F32) / half)
    lane_freq = jnp.tile(inv_freq, LANES // half)
    sign = jnp.tile(jnp.repeat(jnp.array([-1.0, 1.0], F32), half), LANES // HEAD_DIM)
    ang = jnp.arange(s, dtype=F32)[:, None] * lane_freq[None, :]
    return jnp.cos(ang), jnp.sin(ang) * sign[None, :]


def _pad_lanes(row):
    return jnp.pad(row, ((0, 0), (0, LANES - row.shape[1])))


def _layer(x, attn_norm_w, w_in, q_norm_w, k_norm_w, conv_w, conv_b, dt_bias, a_log, d_skip,
           ssd_norm_w, w_out, mlp_norm_w, w_up, w_down):
    b, s, d = x.shape
    w_main = w_in[:, :MAIN_WIDTH].astype(BF16)
    w_dt = w_in[:, MAIN_WIDTH:]
    w_dt_c = jnp.pad(w_dt, ((0, 0), (0, LANES - N_SSD_HEADS))).astype(BF16)
    head_of = jnp.arange(ATTN_WIDTH, dtype=jnp.int32) // HEAD_DIM
    group_mean = jnp.where(head_of[:, None] == head_of[None, :], 1.0 / HEAD_DIM, 0.0).astype(BF16)
    cos, sin = _rope_tables(s)
    q, k, v, gate, act, dtc, dtr = _in_proj(
        x, attn_norm_w[None, :], w_main, w_dt_c, group_mean,
        jnp.tile(q_norm_w, N_ATTN_HEADS)[None, :], jnp.tile(k_norm_w, N_ATTN_HEADS)[None, :],
        cos, sin, conv_w, conv_b[None, :], _pad_lanes(dt_bias[None, :]), tm=PROJ_TOKENS)

    o_attn = _attention(q, k, v)

    col = lambda p: jnp.pad(p[:, None], ((0, DT_ROWS - N_SSD_HEADS), (0, 0)))
    y_ssd = _ssd(act, gate, dtc, dtr, _pad_lanes(a_log[None, :]), col(a_log),
                 jnp.repeat(d_skip, SSD_HEAD_DIM)[None, :], ssd_norm_w[None, :])

    out = _out_mlp(x.reshape(b * s, d), o_attn.reshape(b * s, ATTN_WIDTH),
                   y_ssd.reshape(b * s, SSD_WIDTH),
                   w_out[:ATTN_WIDTH].astype(BF16), w_out[ATTN_WIDTH:].astype(BF16),
                   mlp_norm_w[None, :], w_up.astype(BF16), w_down.astype(BF16), tm=MLP_TOKENS)
    return out.reshape(b, s, d)


def kernel(x, attn_norm_w, w_in, q_norm_w, k_norm_w, conv_w, conv_b, dt_bias, a_log, d_skip,
           ssd_norm_w, w_out, mlp_norm_w, w_up, w_down):
    for i in range(attn_norm_w.shape[0]):
        x = _layer(x, attn_norm_w[i], w_in[i], q_norm_w[i], k_norm_w[i], conv_w[i], conv_b[i],
                   dt_bias[i], a_log[i], d_skip[i], ssd_norm_w[i], w_out[i], mlp_norm_w[i],
                   w_up[i], w_down[i])
    return x
```

```python
import functools

import jax
import jax.numpy as jnp
from jax import lax
from jax.experimental import pallas as pl
from jax.experimental.pallas import tpu as pltpu

F32 = jnp.float32
BF16 = jnp.bfloat16

HEAD_DIM = 64
N_ATTN_HEADS = 8
ATTN_WIDTH = N_ATTN_HEADS * HEAD_DIM
ROPE_THETA = 10000.0
Q_BLOCK = 128
DILATIONS = (1, 4, 16)
FAR_DIL = 4
FAR_RATIO = 4
ATTN_LOOKAHEAD = 5
SSD_HEAD_DIM = 64
N_SSD_HEADS = 8
SSD_WIDTH = N_SSD_HEADS * SSD_HEAD_DIM
N_SSD_GROUPS = 2
HEADS_PER_GROUP = N_SSD_HEADS // N_SSD_GROUPS
SSD_STATE = 128
CONV_WIDTH = 4
CHUNK = 128
SSD_CHUNKS_PER_STEP = 4
CONV_CHANNELS = SSD_WIDTH + 2 * N_SSD_GROUPS * SSD_STATE
CONV_HISTORY = 16
NORM_ROWS = 32
EPS = 1e-6
NEG_BIG = -1e30
SCORE_SCALE = HEAD_DIM ** -0.5 * 1.4426950408889634

LANES = 128
MXU_WIDTH = 256
DT_ROWS = 16
VMEM_LIMIT = 56 * 1024 * 1024
PROJ_TOKENS = 512
MLP_TOKENS = 512


def _dot(a, b):
    return jnp.dot(a, b, preferred_element_type=F32)


def _dot_nt(a, b):
    return lax.dot_general(a, b, (((1,), (1,)), ((), ())), preferred_element_type=F32)


def _split_dot(a, b, parts, lhs_split=True):
    src = a if lhs_split else b
    acc = None
    rem = src
    for _ in range(parts):
        piece = rem.astype(BF16)
        rem = rem - piece.astype(F32)
        term = _dot(piece, b) if lhs_split else _dot(a, piece)
        acc = term if acc is None else acc + term
    return acc


def _resident(shape):
    zeros = (0,) * len(shape)
    return pl.BlockSpec(shape, lambda *_: zeros, pipeline_mode=pl.Buffered(1))


def _rms_norm_bf16(x, w):
    return (x * lax.rsqrt(jnp.mean(x * x, axis=-1, keepdims=True) + EPS) * w).astype(BF16)


def _in_proj_kernel(xnext_ref, xfirst_ref, xprev_ref, nw_ref, w_ref, gm_ref, qw_ref, kw_ref,
                    cos_ref, sin_ref, cw_ref, cb_ref, dtb_ref,
                    q_ref, k_ref, v_ref, gate_ref, act_ref, dtc_ref, dtr_ref,
                    *scratch, n_tiles, tiles_per_row):
    step = pl.program_id(0)
    slots = (scratch[:4], scratch[4:])

    @pl.when(step == 0)
    def _():
        slots[0][0][...] = _rms_norm_bf16(xfirst_ref[...], nw_ref[...])
        for ref in slots[1][1:]:
            ref[...] = jnp.zeros(ref.shape, F32)

    for parity in range(2):
        pl.when(step % 2 == parity)(functools.partial(
            _in_proj_step, xnext_ref, xprev_ref, nw_ref, w_ref, gm_ref, qw_ref, kw_ref, cos_ref,
            sin_ref, cw_ref, cb_ref, dtb_ref, q_ref, k_ref, v_ref, gate_ref, act_ref, dtc_ref,
            dtr_ref, slots[parity], slots[1 - parity],
            starts_row=jnp.minimum(step, n_tiles - 1) % tiles_per_row == 0))


def _in_proj_step(xnext_ref, xprev_ref, nw_ref, w_ref, gm_ref, qw_ref, kw_ref, cos_ref, sin_ref,
                  cw_ref, cb_ref, dtb_ref, q_ref, k_ref, v_ref, gate_ref, act_ref, dtc_ref,
                  dtr_ref, fill, drain, *, starts_row):
    tm = xnext_ref.shape[0]
    hb_ref, fill_main, fill_conv, fill_dt = fill
    next_hb_ref, drain_main, drain_conv, drain_dt = drain
    main_width = fill_main.shape[1]

    def main_piece(c):
        cols = pl.ds(c * MXU_WIDTH, MXU_WIDTH)

        def run():
            fill_main[:, cols] = _dot(hb_ref[...], w_ref[:, cols])
        return run

    def conv_piece(c):
        src = pl.ds(main_width + c * MXU_WIDTH, MXU_WIDTH)
        dst = pl.ds(c * MXU_WIDTH, MXU_WIDTH)

        def run():
            fill_conv[CONV_HISTORY:, dst] = _dot(hb_ref[...], w_ref[:, src])
            hist = _dot(_rms_norm_bf16(xprev_ref[...], nw_ref[...]), w_ref[:, src])
            fill_conv[0:CONV_HISTORY, dst] = jnp.where(starts_row, 0.0, hist)
        return run

    def dt_piece():
        fill_dt[...] = _dot(hb_ref[...], w_ref[:, main_width + CONV_CHANNELS:])

    matmul_pieces = ([main_piece(c) for c in range(main_width // MXU_WIDTH)]
                     + [conv_piece(c) for c in range(CONV_CHANNELS // MXU_WIDTH)] + [dt_piece])

    rb = Q_BLOCK
    n_pairs = ATTN_WIDTH // LANES
    lane = lax.broadcasted_iota(jnp.int32, (rb, LANES), 1)
    first_half = (lane % HEAD_DIM) < (HEAD_DIM // 2)
    pair_mean = gm_ref[0:LANES, 0:LANES]

    def head_norm_rope(t, w, rows):
        ms = _dot((t * t).astype(BF16), pair_mean)
        n = t * lax.rsqrt(ms + EPS) * w
        partner = jnp.where(first_half,
                            pltpu.roll(n, LANES - HEAD_DIM // 2, 1),
                            pltpu.roll(n, HEAD_DIM // 2, 1))
        return n * cos_ref[rows, :] + partner * sin_ref[rows, :]

    def qk_piece(r, p):
        rows = pl.ds(r * rb, rb)
        lanes = pl.ds(p * LANES, LANES)

        def run():
            q = drain_main[rows, pl.ds(p * LANES, LANES)]
            k = drain_main[rows, pl.ds(ATTN_WIDTH + p * LANES, LANES)]
            q_ref[rows, lanes] = (head_norm_rope(q, qw_ref[:, lanes], rows)
                                  * SCORE_SCALE).astype(BF16)
            k_ref[rows, lanes] = head_norm_rope(k, kw_ref[:, lanes], rows).astype(BF16)
        return run

    def vz_piece(r):
        rows = pl.ds(r * rb, rb)

        def run():
            v_ref[rows, :] = drain_main[rows, 2 * ATTN_WIDTH:3 * ATTN_WIDTH].astype(BF16)
            gate_ref[rows, :] = _silu(drain_main[rows, 3 * ATTN_WIDTH:])
            dtc = _softplus(drain_dt[rows, :] + dtb_ref[...])
            dtc_ref[rows, :] = dtc
            dtr_ref[:, rows] = dtc.T[:DT_ROWS, :]
        return run

    def conv_silu_piece(r, c):
        lanes = pl.ds(c * MXU_WIDTH, MXU_WIDTH)

        def run():
            conv = cb_ref[:, lanes]
            for tap in range(CONV_WIDTH):
                first = CONV_HISTORY - (CONV_WIDTH - 1 - tap) + r * rb
                conv = conv + cw_ref[tap:tap + 1, lanes] * drain_conv[first:first + rb, lanes]
            act_ref[pl.ds(r * rb, rb), lanes] = _silu(conv)
        return run

    def norm_piece(r):
        rows = pl.ds(r * NORM_ROWS, NORM_ROWS)

        def run():
            next_hb_ref[rows, :] = _rms_norm_bf16(xnext_ref[rows, :], nw_ref[...])
        return run

    elementwise_pieces = []
    for r in range(tm // rb):
        per_block = ([qk_piece(r, p) for p in range(n_pairs)]
                     + [conv_silu_piece(r, c) for c in range(CONV_CHANNELS // MXU_WIDTH)]
                     + [norm_piece(r * (rb // NORM_ROWS) + i) for i in range(rb // NORM_ROWS)])
        elementwise_pieces += [per_block[i::4][j] for i in range(4) for j in range(3)]
        elementwise_pieces.append(vz_piece(r))

    per_matmul = -(-len(elementwise_pieces) // len(matmul_pieces))
    for i, matmul_piece in enumerate(matmul_pieces):
        matmul_piece()
        for piece in elementwise_pieces[i * per_matmul:(i + 1) * per_matmul]:
            piece()


def _in_proj(x, attn_norm_w, w_all, group_mean, qw, kw, cos, sin, conv_w, conv_b, dt_bias, tm):
    b, s, d = x.shape
    tiles_per_row = s // tm
    n_tiles = b * tiles_per_row
    main_width = 3 * ATTN_WIDTH + SSD_WIDTH
    mm_tile = lambda step: jnp.minimum(step, n_tiles - 1)
    ew_tile = lambda step: jnp.maximum(step - 1, 0)
    x_next = pl.BlockSpec((None, tm, d), lambda st: (mm_tile(st + 1) // tiles_per_row,
                                                     mm_tile(st + 1) % tiles_per_row, 0))
    x_first = pl.BlockSpec((None, tm, d), lambda st: (0, 0, 0), pipeline_mode=pl.Buffered(1))
    prev = pl.BlockSpec(
        (None, CONV_HISTORY, d),
        lambda st: (mm_tile(st) // tiles_per_row,
                    jnp.maximum(mm_tile(st) % tiles_per_row * (tm // CONV_HISTORY) - 1, 0), 0))
    out = lambda width: pl.BlockSpec((None, tm, width), lambda st: (ew_tile(st) // tiles_per_row,
                                                                   ew_tile(st) % tiles_per_row, 0))
    table = pl.BlockSpec((tm, LANES), lambda st: (ew_tile(st) % tiles_per_row, 0))
    out_shapes = (
        jax.ShapeDtypeStruct((b, s, ATTN_WIDTH), BF16),
        jax.ShapeDtypeStruct((b, s, ATTN_WIDTH), BF16),
        jax.ShapeDtypeStruct((b, s, ATTN_WIDTH), BF16),
        jax.ShapeDtypeStruct((b, s, SSD_WIDTH), F32),
        jax.ShapeDtypeStruct((b, s, CONV_CHANNELS), F32),
        jax.ShapeDtypeStruct((b, s, LANES), F32),
        jax.ShapeDtypeStruct((b, DT_ROWS, s), F32),
    )
    return pl.pallas_call(
        functools.partial(_in_proj_kernel, n_tiles=n_tiles, tiles_per_row=tiles_per_row),
        grid=(n_tiles + 1,),
        in_specs=[
            x_next, x_first, prev,
            _resident((1, d)),
            _resident(w_all.shape),
            _resident(group_mean.shape),
            _resident((1, ATTN_WIDTH)),
            _resident((1, ATTN_WIDTH)),
            table, table,
            _resident(conv_w.shape), _resident(conv_b.shape), _resident(dt_bias.shape),
        ],
        out_specs=(
            out(ATTN_WIDTH), out(ATTN_WIDTH), out(ATTN_WIDTH), out(SSD_WIDTH),
            out(CONV_CHANNELS), out(LANES),
            pl.BlockSpec((None, DT_ROWS, tm), lambda st: (ew_tile(st) // tiles_per_row, 0,
                                                         ew_tile(st) % tiles_per_row)),
        ),
        out_shape=out_shapes,
        scratch_shapes=2 * [
            pltpu.VMEM((tm, d), BF16),
            pltpu.VMEM((tm, main_width), F32),
            pltpu.VMEM((CONV_HISTORY + tm, CONV_CHANNELS), F32),
            pltpu.VMEM((tm, LANES), F32),
        ],
        compiler_params=pltpu.CompilerParams(
            dimension_semantics=("arbitrary",), vmem_limit_bytes=VMEM_LIMIT),
        name="in_proj",
    )(x, x, x, attn_norm_w, w_all, group_mean, qw, kw, cos, sin, conv_w, conv_b, dt_bias)


def _attn_blocks(blocks, store):
    qb = Q_BLOCK
    n_pairs = ATTN_WIDTH // LANES
    head0 = lax.broadcasted_iota(jnp.int32, (qb, LANES), 1) < HEAD_DIM
    items = [(i, pair) for i in range(len(blocks)) for pair in range(n_pairs)]

    def scores(item):
        i, pair = item
        load_q, load_k, _, load_bias = blocks[i]
        qt, kb, bias = load_q(pair), load_k(pair), load_bias()
        out = []
        for h in range(2):
            mine = head0 if h == 0 else jnp.logical_not(head0)
            qh = jnp.where(mine, qt, jnp.zeros_like(qt))
            out.append(_dot_nt(qh, kb) + bias)
        return out

    queue = [scores(item) for item in items[:ATTN_LOOKAHEAD]]
    parts = []
    for n, (i, pair) in enumerate(items):
        if n + ATTN_LOOKAHEAD < len(items):
            queue.append(scores(items[n + ATTN_LOOKAHEAD]))
        pending = queue.pop(0)
        vb = blocks[i][2](pair)
        v_ext = jnp.concatenate([vb, jnp.ones_like(vb)], axis=1)
        heads = []
        for sc in pending:
            m = jnp.max(sc, axis=1, keepdims=True)
            r = _dot(jnp.exp2(sc - m).astype(BF16), v_ext)
            heads.append((r[:, :LANES], r[:, LANES:], jnp.broadcast_to(m, (qb, LANES))))
        parts.append([jnp.where(head0, a, b) for a, b in zip(*heads)])
        if pair == n_pairs - 1:
            store(i, *[jnp.concatenate(t, axis=1) for t in zip(*parts)])
            parts = []


def _near_bias(first):
    qb = Q_BLOCK
    row = lax.broadcasted_iota(jnp.int32, (qb, 2 * qb), 0)
    col = lax.broadcasted_iota(jnp.int32, (qb, 2 * qb), 1)
    ok = (col <= row) if first else ((col >= row) & (col <= row + qb))
    return jnp.where(ok, 0.0, NEG_BIG)


def _far_bias(blocks_back):
    qb = Q_BLOCK
    row = lax.broadcasted_iota(jnp.int32, (qb, qb), 0)
    col = lax.broadcasted_iota(jnp.int32, (qb, qb), 1)
    dist = row - col + blocks_back * qb
    in_window = (dist >= 0) & (dist <= qb)
    on_stride = (dist >= 0) & ((dist & (FAR_RATIO - 1)) == 0)
    return jnp.where(in_window & on_stride, 1.0,
                     jnp.where(in_window | on_stride, 0.0, NEG_BIG))


def _attn_kernel(q_ref, k_ref, v_ref, o_ref,
                 tmp, q_cls, k_cls, v_cls, num_far, den_far, top_far, near_bias, far_bias):
    qb = Q_BLOCK
    step = pl.program_id(1)
    rows = q_ref.shape[0]
    s = k_ref.shape[0]
    n_steps = s // rows
    n_pairs = ATTN_WIDTH // LANES
    lanes = lambda pair: pl.ds(pair * LANES, LANES)
    far = (num_far, den_far, top_far)

    def to_class_order(src_rows, dst, at):
        for p in range(n_pairs):
            tmp[p] = src_rows[:, p * LANES:(p + 1) * LANES].astype(F32)
            for c in range(FAR_DIL):
                dst[c, at:at + qb, lanes(p)] = tmp[p, pl.ds(c, qb, stride=FAR_DIL), :].astype(BF16)

    @pl.when(step == 0)
    def _():
        for i in range(n_steps):
            to_class_order(k_ref[i * rows:(i + 1) * rows, :], k_cls, i * qb)
            to_class_order(v_ref[i * rows:(i + 1) * rows, :], v_cls, i * qb)

    to_class_order(q_ref[...], q_cls, 0)
    near_bias[0] = _near_bias(False)
    near_bias[1] = _near_bias(True)
    for d in range(3):
        far_bias[d] = _far_bias(d)

    def far_store(c, *stats):
        for dst, t in zip(far, stats):
            for p in range(n_pairs):
                dst[p, pl.ds(c, qb, stride=FAR_DIL), :] = t[:, p * LANES:(p + 1) * LANES]

    for n_kb in range(1, n_steps + 1):
        @pl.when(step == n_kb - 1)
        def _(n_kb=n_kb):
            def block(c):
                load_bias = lambda: jnp.concatenate(
                    [far_bias[min(n_kb - 1 - kb, 2)] for kb in range(n_kb)], axis=1)
                return (lambda p: q_cls[c, :, lanes(p)],
                        lambda p: k_cls[c, 0:n_kb * qb, lanes(p)],
                        lambda p: v_cls[c, 0:n_kb * qb, lanes(p)], load_bias)
            _attn_blocks([block(c) for c in range(FAR_DIL)], far_store)

    is_first = (step == 0).astype(jnp.int32)

    def near_block(j):
        q_rows = pl.ds(j * qb, qb)
        if j == 0:
            start = pl.multiple_of((step * FAR_DIL - 1 + is_first) * qb, qb)
            load_bias = lambda: near_bias[is_first]
        else:
            start = pl.multiple_of((step * FAR_DIL + j - 1) * qb, qb)
            load_bias = lambda: near_bias[0]
        kv_rows = pl.ds(start, 2 * qb)
        return (lambda p: q_ref[q_rows, lanes(p)], lambda p: k_ref[kv_rows, lanes(p)],
                lambda p: v_ref[kv_rows, lanes(p)], load_bias)

    def near_store(j, num, den, top):
        q_rows = pl.ds(j * qb, qb)
        num_f, den_f, top_f = [
            jnp.concatenate([t[p, q_rows, :] for p in range(n_pairs)], axis=1) for t in far]
        both = jnp.maximum(top, top_f)
        w_near, w_far = jnp.exp2(top - both), jnp.exp2(top_f - both)
        o_ref[q_rows, :] = ((w_near * num + w_far * num_f)
                            / (w_near * den + w_far * den_f)).astype(o_ref.dtype)

    _attn_blocks([near_block(j) for j in range(rows // qb)], near_store)


def _attention(q, k, v):
    b, s, w = q.shape
    assert DILATIONS == (1, FAR_DIL, FAR_DIL * FAR_RATIO) and s == Q_BLOCK * DILATIONS[-1]
    rows = FAR_DIL * Q_BLOCK
    q_blk = pl.BlockSpec((None, rows, w), lambda i, g: (i, g, 0))
    kv_blk = pl.BlockSpec((None, s, w), lambda i, g: (i, 0, 0))
    return pl.pallas_call(
        _attn_kernel,
        grid=(b, s // rows),
        in_specs=[q_blk, kv_blk, kv_blk],
        out_specs=q_blk,
        out_shape=jax.ShapeDtypeStruct((b, s, w), BF16),
        scratch_shapes=[
            pltpu.VMEM((w // LANES, rows, LANES), F32),
            pltpu.VMEM((FAR_DIL, Q_BLOCK, w), BF16),
            pltpu.VMEM((FAR_DIL, s // FAR_DIL, w), BF16),
            pltpu.VMEM((FAR_DIL, s // FAR_DIL, w), BF16),
            pltpu.VMEM((w // LANES, rows, LANES), F32),
            pltpu.VMEM((w // LANES, rows, LANES), F32),
            pltpu.VMEM((w // LANES, rows, LANES), F32),
            pltpu.VMEM((2, Q_BLOCK, 2 * Q_BLOCK), F32),
            pltpu.VMEM((3, Q_BLOCK, Q_BLOCK), F32),
        ],
        compiler_params=pltpu.CompilerParams(
            dimension_semantics=("parallel", "arbitrary"), vmem_limit_bytes=VMEM_LIMIT),
        name="dilated_attn",
    )(q, k, v)


def _softplus(x):
    return jnp.maximum(x, 0.0) + jnp.log1p(jnp.exp(-jnp.abs(x)))


def _silu(x):
    return x * (1.0 / (1.0 + jnp.exp(-x)))


def _ssd_kernel(act_ref, gate_ref, dtc_ref, dtr_ref, alog_row_ref, alog_col_ref, dskip_ref,
                nw_ref, y_ref, state):
    L = CHUNK

    @pl.when(pl.program_id(1) == 0)
    def _():
        state[...] = jnp.zeros(state.shape, F32)

    a_row = -jnp.exp(alog_row_ref[...])
    a_col = -jnp.exp(alog_col_ref[...])
    ri = lax.broadcasted_iota(jnp.int32, (L, L), 0)
    ci = lax.broadcasted_iota(jnp.int32, (L, L), 1)
    causal = ci <= ri
    tril = jnp.where(causal, 1.0, 0.0).astype(BF16)
    triu = jnp.where(ri <= ci, 1.0, 0.0).astype(BF16)
    gs = N_SSD_GROUPS * SSD_STATE
    gw = SSD_WIDTH // N_SSD_GROUPS

    n_pairs = SSD_WIDTH // LANES
    pairs_per_group = n_pairs // N_SSD_GROUPS
    head0 = lax.broadcasted_iota(jnp.int32, (L, LANES), 1) < SSD_HEAD_DIM
    head0_row = head0[:1, :]
    spread = (lax.broadcasted_iota(jnp.int32, (LANES, SSD_WIDTH), 0)
              == lax.broadcasted_iota(jnp.int32, (LANES, SSD_WIDTH), 1) // SSD_HEAD_DIM)
    spread = jnp.where(spread, 1.0, 0.0).astype(BF16)

    states = [state[p] for p in range(n_pairs)]
    for chunk in range(act_ref.shape[0] // L):
        rows = pl.ds(chunk * L, L)
        act = act_ref[rows, :]
        xs = act[:, :SSD_WIDTH]
        xs_b = xs.astype(BF16)
        bm = act[:, SSD_WIDTH:SSD_WIDTH + gs]
        cm = act[:, SSD_WIDTH + gs:SSD_WIDTH + 2 * gs]

        dt_c = dtc_ref[rows, :]
        dt_r = dtr_ref[:, chunk * L:(chunk + 1) * L]
        acs_c = _split_dot(tril, dt_c * a_row, 3, lhs_split=False)
        acs_r = _split_dot(dt_r * a_col, triu, 3)
        e_wide = _split_dot(jnp.exp(acs_c), spread, 2)

        ys = []
        for g in range(N_SSD_GROUPS):
            b_g = bm[:, g * SSD_STATE:(g + 1) * SSD_STATE]
            c_gb = cm[:, g * SSD_STATE:(g + 1) * SSD_STATE].astype(BF16)
            cb = _dot_nt(c_gb, b_g.astype(BF16))
            b_gt = b_g.T
            for q in range(pairs_per_group):
                p = g * pairs_per_group + q
                x_pair = xs_b[:, p * LANES:(p + 1) * LANES]
                zero = jnp.zeros_like(x_pair)
                x_split = jnp.concatenate([jnp.where(head0, x_pair, zero),
                                           jnp.where(head0, zero, x_pair)], axis=0)
                w_diag, w_state, last = [], [], []
                for h in (2 * p, 2 * p + 1):
                    acs_col = acs_c[:, h:h + 1]
                    acs_row = acs_r[h:h + 1, :]
                    dt_row = dt_r[h:h + 1, :]
                    seg = jnp.exp(jnp.where(causal, acs_col - acs_row, NEG_BIG))
                    w_diag.append((cb * seg * dt_row).astype(BF16))
                    acs_last = acs_row[:, L - 1:L]
                    w_state.append((b_gt * (jnp.exp(acs_last - acs_row) * dt_row)).astype(BF16))
                    last.append(jnp.exp(acs_last))
                y_pair = (_dot(jnp.concatenate(w_diag, axis=1), x_split)
                          + e_wide[:, p * LANES:(p + 1) * LANES]
                          * _dot(c_gb, states[p].astype(BF16)))
                states[p] = (jnp.where(head0_row, last[0], last[1]) * states[p]
                             + _dot(jnp.concatenate(w_state, axis=1), x_split))
                ys.append(y_pair)
        y = (jnp.concatenate(ys, axis=1) + dskip_ref[...] * xs) * gate_ref[rows, :]

        outs = []
        for g in range(N_SSD_GROUPS):
            yg = y[:, g * gw:(g + 1) * gw]
            outs.append(yg * lax.rsqrt(jnp.mean(yg * yg, axis=-1, keepdims=True) + EPS))
        y_ref[rows, :] = (jnp.concatenate(outs, axis=1) * nw_ref[...]).astype(y_ref.dtype)

    for p in range(n_pairs):
        state[p] = states[p]


def _ssd(act, gate, dtc, dtr, alog_row, alog_col, dskip, nw):
    b, s, _ = act.shape
    rows = SSD_CHUNKS_PER_STEP * CHUNK
    tok = lambda width: pl.BlockSpec((None, rows, width), lambda i, c: (i, c, 0))
    return pl.pallas_call(
        _ssd_kernel,
        grid=(b, s // rows),
        in_specs=[
            tok(CONV_CHANNELS), tok(SSD_WIDTH), tok(LANES),
            pl.BlockSpec((None, DT_ROWS, rows), lambda i, c: (i, 0, c)),
            _resident(alog_row.shape), _resident(alog_col.shape),
            _resident(dskip.shape), _resident(nw.shape),
        ],
        out_specs=tok(SSD_WIDTH),
        out_shape=jax.ShapeDtypeStruct((b, s, SSD_WIDTH), BF16),
        scratch_shapes=[pltpu.VMEM((SSD_WIDTH // LANES, SSD_STATE, LANES), F32)],
        compiler_params=pltpu.CompilerParams(
            dimension_semantics=("parallel", "arbitrary"), vmem_limit_bytes=VMEM_LIMIT),
        name="ssd",
    )(act, gate, dtc, dtr, alog_row, alog_col, dskip, nw)


def _out_mlp_kernel(x_ref, oa_ref, ys_ref, woa_ref, wos_ref, nw_ref, wup_ref, wdn_ref, o_ref):
    x1 = x_ref[...] + _dot(oa_ref[...], woa_ref[...]) + _dot(ys_ref[...], wos_ref[...])
    hm = x1 * lax.rsqrt(jnp.mean(x1 * x1, axis=-1, keepdims=True) + EPS) * nw_ref[...]
    up = _dot(hm.astype(BF16), wup_ref[...])
    act = jnp.square(jnp.maximum(up, 0.0)).astype(BF16)
    o_ref[...] = x1 + _dot(act, wdn_ref[...])


def _out_mlp(x, o_attn, y_ssd, wo_a, wo_s, mlp_norm_w, w_up, w_down, tm):
    t, d = x.shape
    tok = lambda width: pl.BlockSpec((tm, width), lambda i: (i, 0))
    return pl.pallas_call(
        _out_mlp_kernel,
        grid=(t // tm,),
        in_specs=[
            tok(d), tok(ATTN_WIDTH), tok(SSD_WIDTH),
            _resident(wo_a.shape), _resident(wo_s.shape), _resident((1, d)),
            _resident(w_up.shape), _resident(w_down.shape),
        ],
        out_specs=tok(d),
        out_shape=jax.ShapeDtypeStruct((t, d), F32),
        compiler_params=pltpu.CompilerParams(
            dimension_semantics=("parallel",), vmem_limit_bytes=VMEM_LIMIT),
        name="out_mlp",
    )(x, o_attn, y_ssd, wo_a, wo_s, mlp_norm_w, w_up, w_down)


def _rope_tables(s):
    half = HEAD_DIM // 2
    inv_freq = ROPE_THETA ** (-jnp.arange(half, dtype=F32) / half)
    lane_freq = jnp.tile(inv_freq, LANES // half)
    sign = jnp.tile(jnp.repeat(jnp.array([-1.0, 1.0], F32), half), LANES // HEAD_DIM)
    ang = jnp.arange(s, dtype=F32)[:, None] * lane_freq[None, :]
    return jnp.cos(ang), jnp.sin(ang) * sign[None, :]


def _pad_lanes(row):
    return jnp.pad(row, ((0, 0), (0, LANES - row.shape[1])))


def _layer(x, attn_norm_w, w_in, q_norm_w, k_norm_w, conv_w, conv_b, dt_bias, a_log, d_skip,
           ssd_norm_w, w_out, mlp_norm_w, w_up, w_down):
    b, s, d = x.shape
    w_all = jnp.pad(w_in, ((0, 0), (0, LANES - N_SSD_HEADS))).astype(BF16)
    head_of = jnp.arange(ATTN_WIDTH, dtype=jnp.int32) // HEAD_DIM
    group_mean = jnp.where(head_of[:, None] == head_of[None, :], 1.0 / HEAD_DIM, 0.0).astype(BF16)
    cos, sin = _rope_tables(s)
    q, k, v, gate, act, dtc, dtr = _in_proj(
        x, attn_norm_w[None, :], w_all, group_mean,
        jnp.tile(q_norm_w, N_ATTN_HEADS)[None, :], jnp.tile(k_norm_w, N_ATTN_HEADS)[None, :],
        cos, sin, conv_w, conv_b[None, :], _pad_lanes(dt_bias[None, :]), tm=PROJ_TOKENS)

    o_attn = _attention(q, k, v)

    col = lambda p: jnp.pad(p[:, None], ((0, DT_ROWS - N_SSD_HEADS), (0, 0)))
    y_ssd = _ssd(act, gate, dtc, dtr, _pad_lanes(a_log[None, :]), col(a_log),
                 jnp.repeat(d_skip, SSD_HEAD_DIM)[None, :], ssd_norm_w[None, :])

    out = _out_mlp(x.reshape(b * s, d), o_attn.reshape(b * s, ATTN_WIDTH),
                   y_ssd.reshape(b * s, SSD_WIDTH),
                   w_out[:ATTN_WIDTH].astype(BF16), w_out[ATTN_WIDTH:].astype(BF16),
                   mlp_norm_w[None, :], w_up.astype(BF16), w_down.astype(BF16), tm=MLP_TOKENS)
    return out.reshape(b, s, d)


def kernel(x, attn_norm_w, w_in, q_norm_w, k_norm_w, conv_w, conv_b, dt_bias, a_log, d_skip,
           ssd_norm_w, w_out, mlp_norm_w, w_up, w_down):
    for i in range(attn_norm_w.shape[0]):
        x = _layer(x, attn_norm_w[i], w_in[i], q_norm_w[i], k_norm_w[i], conv_w[i], conv_b[i],
                   dt_bias[i], a_log[i], d_skip[i], ssd_norm_w[i], w_out[i], mlp_norm_w[i],
                   w_up[i], w_down[i])
    return x
```

```python
import functools

import jax
import jax.numpy as jnp
from jax import lax
from jax.experimental import pallas as pl
from jax.experimental.pallas import tpu as pltpu

F32 = jnp.float32
BF16 = jnp.bfloat16

HEAD_DIM = 64
N_ATTN_HEADS = 8
ATTN_WIDTH = N_ATTN_HEADS * HEAD_DIM
ROPE_THETA = 10000.0
Q_BLOCK = 128
DILATIONS = (1, 4, 16)
FAR_DIL = 4
FAR_RATIO = 4
ATTN_LOOKAHEAD = 5
SSD_HEAD_DIM = 64
N_SSD_HEADS = 8
SSD_WIDTH = N_SSD_HEADS * SSD_HEAD_DIM
N_SSD_GROUPS = 2
HEADS_PER_GROUP = N_SSD_HEADS // N_SSD_GROUPS
SSD_STATE = 128
CONV_WIDTH = 4
CHUNK = 128
SSD_CHUNKS_PER_STEP = 16
CONV_CHANNELS = SSD_WIDTH + 2 * N_SSD_GROUPS * SSD_STATE
CONV_HISTORY = 16
EPS = 1e-6
NEG_BIG = -1e30
SCORE_SCALE = HEAD_DIM ** -0.5 * 1.4426950408889634

LANES = 128
DT_ROWS = 16
MAIN_WIDTH = 3 * ATTN_WIDTH + SSD_WIDTH + CONV_CHANNELS
VMEM_LIMIT = 56 * 1024 * 1024
PROJ_TOKENS = 512
MLP_TOKENS = 512


def _dot(a, b):
    return jnp.dot(a, b, preferred_element_type=F32)


def _dot_nt(a, b):
    return lax.dot_general(a, b, (((1,), (1,)), ((), ())), preferred_element_type=F32)


def _split_dot(a, b, parts, lhs_split=True):
    src = a if lhs_split else b
    acc = None
    rem = src
    for _ in range(parts):
        piece = rem.astype(BF16)
        rem = rem - piece.astype(F32)
        term = _dot(piece, b) if lhs_split else _dot(a, piece)
        acc = term if acc is None else acc + term
    return acc


def _resident(shape):
    zeros = (0,) * len(shape)
    return pl.BlockSpec(shape, lambda *_: zeros, pipeline_mode=pl.Buffered(1))


def _in_proj_kernel(x_ref, xprev_ref, nw_ref, w_ref, wdt_ref, gm_ref, qw_ref, kw_ref,
                    cos_ref, sin_ref, cw_ref, cb_ref, dtb_ref,
                    q_ref, k_ref, v_ref, gate_ref, act_ref, dtc_ref, dtr_ref, ubuf):
    tm = x_ref.shape[0]

    def normed(x):
        y = x * lax.rsqrt(jnp.mean(x * x, axis=-1, keepdims=True) + EPS) * nw_ref[...]
        return y.astype(BF16)

    hb = normed(x_ref[...])

    lane = lax.broadcasted_iota(jnp.int32, (tm, ATTN_WIDTH), 1)
    first_half = (lane % HEAD_DIM) < (HEAD_DIM // 2)
    cos = jnp.concatenate([cos_ref[...]] * (ATTN_WIDTH // LANES), axis=1)
    sin = jnp.concatenate([sin_ref[...]] * (ATTN_WIDTH // LANES), axis=1)

    def head_norm_rope(t, w):
        ms = _dot((t * t).astype(BF16), gm_ref[...])
        n = t * lax.rsqrt(ms + EPS) * w
        partner = jnp.where(first_half,
                            pltpu.roll(n, ATTN_WIDTH - HEAD_DIM // 2, 1),
                            pltpu.roll(n, HEAD_DIM // 2, 1))
        return n * cos + partner * sin

    o = 0
    q = _dot(hb, w_ref[:, o:o + ATTN_WIDTH]); o += ATTN_WIDTH
    q_ref[...] = (head_norm_rope(q, qw_ref[...]) * SCORE_SCALE).astype(BF16)
    k = _dot(hb, w_ref[:, o:o + ATTN_WIDTH]); o += ATTN_WIDTH
    k_ref[...] = head_norm_rope(k, kw_ref[...]).astype(BF16)
    v_ref[...] = _dot(hb, w_ref[:, o:o + ATTN_WIDTH]).astype(BF16); o += ATTN_WIDTH
    gate_ref[...] = _silu(_dot(hb, w_ref[:, o:o + SSD_WIDTH])); o += SSD_WIDTH

    w_xbc = w_ref[:, o:o + CONV_CHANNELS]
    u = _dot(hb, w_xbc)
    hist = _dot(normed(xprev_ref[...]), w_xbc)
    ubuf[0:CONV_HISTORY, :] = jnp.where(pl.program_id(0) == 0, 0.0, hist)
    ubuf[CONV_HISTORY:CONV_HISTORY + tm, :] = u
    conv = cb_ref[...] + cw_ref[CONV_WIDTH - 1:CONV_WIDTH, :] * u
    for tap in range(CONV_WIDTH - 1):
        first = CONV_HISTORY - (CONV_WIDTH - 1 - tap)
        conv = conv + cw_ref[tap:tap + 1, :] * ubuf[first:first + tm, :]
    act_ref[...] = _silu(conv)

    dtc = _softplus(_dot(hb, wdt_ref[...]) + dtb_ref[...])
    dtc_ref[...] = dtc
    dtr_ref[...] = dtc.T[:DT_ROWS, :]


def _in_proj(x, attn_norm_w, w_main, w_dt, group_mean, qw, kw, cos, sin, conv_w, conv_b, dt_bias,
             tm):
    b, s, d = x.shape
    grid = (s // tm, b)
    tok = lambda width: pl.BlockSpec((None, tm, width), lambda j, i: (i, j, 0))
    prev = pl.BlockSpec((None, CONV_HISTORY, d),
                        lambda j, i: (i, jnp.maximum(j * (tm // CONV_HISTORY) - 1, 0), 0))
    out_shapes = (
        jax.ShapeDtypeStruct((b, s, ATTN_WIDTH), BF16),
        jax.ShapeDtypeStruct((b, s, ATTN_WIDTH), BF16),
        jax.ShapeDtypeStruct((b, s, ATTN_WIDTH), BF16),
        jax.ShapeDtypeStruct((b, s, SSD_WIDTH), F32),
        jax.ShapeDtypeStruct((b, s, CONV_CHANNELS), F32),
        jax.ShapeDtypeStruct((b, s, LANES), F32),
        jax.ShapeDtypeStruct((b, DT_ROWS, s), F32),
    )
    return pl.pallas_call(
        _in_proj_kernel,
        grid=grid,
        in_specs=[
            tok(d),
            prev,
            _resident((1, d)),
            _resident(w_main.shape),
            _resident(w_dt.shape),
            _resident(group_mean.shape),
            _resident((1, ATTN_WIDTH)),
            _resident((1, ATTN_WIDTH)),
            pl.BlockSpec((tm, LANES), lambda j, i: (j, 0)),
            pl.BlockSpec((tm, LANES), lambda j, i: (j, 0)),
            _resident(conv_w.shape), _resident(conv_b.shape), _resident(dt_bias.shape),
        ],
        out_specs=(
            tok(ATTN_WIDTH), tok(ATTN_WIDTH), tok(ATTN_WIDTH), tok(SSD_WIDTH),
            tok(CONV_CHANNELS), tok(LANES),
            pl.BlockSpec((None, DT_ROWS, tm), lambda j, i: (i, 0, j)),
        ),
        out_shape=out_shapes,
        scratch_shapes=[pltpu.VMEM((CONV_HISTORY + tm, CONV_CHANNELS), F32)],
        compiler_params=pltpu.CompilerParams(
            dimension_semantics=("parallel", "parallel"), vmem_limit_bytes=VMEM_LIMIT),
        name="in_proj",
    )(x, x, attn_norm_w, w_main, w_dt, group_mean, qw, kw, cos, sin, conv_w, conv_b, dt_bias)


def _attn_blocks(blocks, store):
    qb = Q_BLOCK
    n_pairs = ATTN_WIDTH // LANES
    head0 = lax.broadcasted_iota(jnp.int32, (qb, LANES), 1) < HEAD_DIM
    items = [(i, pair) for i in range(len(blocks)) for pair in range(n_pairs)]

    def scores(item):
        i, pair = item
        load_q, load_k, _, load_bias = blocks[i]
        qt, kb, bias = load_q(pair), load_k(pair), load_bias()
        out = []
        for h in range(2):
            mine = head0 if h == 0 else jnp.logical_not(head0)
            qh = jnp.where(mine, qt, jnp.zeros_like(qt))
            out.append(_dot_nt(qh, kb) + bias)
        return out

    queue = [scores(item) for item in items[:ATTN_LOOKAHEAD]]
    parts = []
    for n, (i, pair) in enumerate(items):
        if n + ATTN_LOOKAHEAD < len(items):
            queue.append(scores(items[n + ATTN_LOOKAHEAD]))
        pending = queue.pop(0)
        vb = blocks[i][2](pair)
        v_ext = jnp.concatenate([vb, jnp.ones_like(vb)], axis=1)
        heads = []
        for sc in pending:
            m = jnp.max(sc, axis=1, keepdims=True)
            r = _dot(jnp.exp2(sc - m).astype(BF16), v_ext)
            heads.append((r[:, :LANES], r[:, LANES:], jnp.broadcast_to(m, (qb, LANES))))
        parts.append([jnp.where(head0, a, b) for a, b in zip(*heads)])
        if pair == n_pairs - 1:
            store(i, *[jnp.concatenate(t, axis=1) for t in zip(*parts)])
            parts = []


def _near_bias(first):
    qb = Q_BLOCK
    row = lax.broadcasted_iota(jnp.int32, (qb, 2 * qb), 0)
    col = lax.broadcasted_iota(jnp.int32, (qb, 2 * qb), 1)
    ok = (col <= row) if first else ((col >= row) & (col <= row + qb))
    return jnp.where(ok, 0.0, NEG_BIG)


def _far_bias(blocks_back):
    qb = Q_BLOCK
    row = lax.broadcasted_iota(jnp.int32, (qb, qb), 0)
    col = lax.broadcasted_iota(jnp.int32, (qb, qb), 1)
    dist = row - col + blocks_back * qb
    in_window = (dist >= 0) & (dist <= qb)
    on_stride = (dist >= 0) & ((dist & (FAR_RATIO - 1)) == 0)
    return jnp.where(in_window & on_stride, 1.0,
                     jnp.where(in_window | on_stride, 0.0, NEG_BIG))


def _attn_kernel(q_ref, k_ref, v_ref, o_ref,
                 tmp, q_cls, k_cls, v_cls, num_far, den_far, top_far, near_bias, far_bias):
    qb = Q_BLOCK
    step = pl.program_id(1)
    rows = q_ref.shape[0]
    s = k_ref.shape[0]
    n_steps = s // rows
    n_pairs = ATTN_WIDTH // LANES
    lanes = lambda pair: pl.ds(pair * LANES, LANES)
    far = (num_far, den_far, top_far)

    def to_class_order(src_rows, dst, at):
        for p in range(n_pairs):
            tmp[p] = src_rows[:, p * LANES:(p + 1) * LANES].astype(F32)
            for c in range(FAR_DIL):
                dst[c, at:at + qb, lanes(p)] = tmp[p, pl.ds(c, qb, stride=FAR_DIL), :].astype(BF16)

    @pl.when(step == 0)
    def _():
        for i in range(n_steps):
            to_class_order(k_ref[i * rows:(i + 1) * rows, :], k_cls, i * qb)
            to_class_order(v_ref[i * rows:(i + 1) * rows, :], v_cls, i * qb)

    to_class_order(q_ref[...], q_cls, 0)
    near_bias[0] = _near_bias(False)
    near_bias[1] = _near_bias(True)
    for d in range(3):
        far_bias[d] = _far_bias(d)

    def far_store(c, *stats):
        for dst, t in zip(far, stats):
            for p in range(n_pairs):
                dst[p, pl.ds(c, qb, stride=FAR_DIL), :] = t[:, p * LANES:(p + 1) * LANES]

    for n_kb in range(1, n_steps + 1):
        @pl.when(step == n_kb - 1)
        def _(n_kb=n_kb):
            def block(c):
                load_bias = lambda: jnp.concatenate(
                    [far_bias[min(n_kb - 1 - kb, 2)] for kb in range(n_kb)], axis=1)
                return (lambda p: q_cls[c, :, lanes(p)],
                        lambda p: k_cls[c, 0:n_kb * qb, lanes(p)],
                        lambda p: v_cls[c, 0:n_kb * qb, lanes(p)], load_bias)
            _attn_blocks([block(c) for c in range(FAR_DIL)], far_store)

    is_first = (step == 0).astype(jnp.int32)

    def near_block(j):
        q_rows = pl.ds(j * qb, qb)
        if j == 0:
            start = pl.multiple_of((step * FAR_DIL - 1 + is_first) * qb, qb)
            load_bias = lambda: near_bias[is_first]
        else:
            start = pl.multiple_of((step * FAR_DIL + j - 1) * qb, qb)
            load_bias = lambda: near_bias[0]
        kv_rows = pl.ds(start, 2 * qb)
        return (lambda p: q_ref[q_rows, lanes(p)], lambda p: k_ref[kv_rows, lanes(p)],
                lambda p: v_ref[kv_rows, lanes(p)], load_bias)

    def near_store(j, num, den, top):
        q_rows = pl.ds(j * qb, qb)
        num_f, den_f, top_f = [
            jnp.concatenate([t[p, q_rows, :] for p in range(n_pairs)], axis=1) for t in far]
        both = jnp.maximum(top, top_f)
        w_near, w_far = jnp.exp2(top - both), jnp.exp2(top_f - both)
        o_ref[q_rows, :] = ((w_near * num + w_far * num_f)
                            / (w_near * den + w_far * den_f)).astype(o_ref.dtype)

    _attn_blocks([near_block(j) for j in range(rows // qb)], near_store)


def _attention(q, k, v):
    b, s, w = q.shape
    assert DILATIONS == (1, FAR_DIL, FAR_DIL * FAR_RATIO) and s == Q_BLOCK * DILATIONS[-1]
    rows = FAR_DIL * Q_BLOCK
    q_blk = pl.BlockSpec((None, rows, w), lambda i, g: (i, g, 0))
    kv_blk = pl.BlockSpec((None, s, w), lambda i, g: (i, 0, 0))
    return pl.pallas_call(
        _attn_kernel,
        grid=(b, s // rows),
        in_specs=[q_blk, kv_blk, kv_blk],
        out_specs=q_blk,
        out_shape=jax.ShapeDtypeStruct((b, s, w), BF16),
        scratch_shapes=[
            pltpu.VMEM((w // LANES, rows, LANES), F32),
            pltpu.VMEM((FAR_DIL, Q_BLOCK, w), BF16),
            pltpu.VMEM((FAR_DIL, s // FAR_DIL, w), BF16),
            pltpu.VMEM((FAR_DIL, s // FAR_DIL, w), BF16),
            pltpu.VMEM((w // LANES, rows, LANES), F32),
            pltpu.VMEM((w // LANES, rows, LANES), F32),
            pltpu.VMEM((w // LANES, rows, LANES), F32),
            pltpu.VMEM((2, Q_BLOCK, 2 * Q_BLOCK), F32),
            pltpu.VMEM((3, Q_BLOCK, Q_BLOCK), F32),
        ],
        compiler_params=pltpu.CompilerParams(
            dimension_semantics=("parallel", "arbitrary"), vmem_limit_bytes=VMEM_LIMIT),
        name="dilated_attn",
    )(q, k, v)


def _softplus(x):
    return jnp.maximum(x, 0.0) + jnp.log1p(jnp.exp(-jnp.abs(x)))


def _silu(x):
    return x * (1.0 / (1.0 + jnp.exp(-x)))


def _ssd_kernel(act_ref, gate_ref, dtc_ref, dtr_ref, alog_row_ref, alog_col_ref, dskip_ref,
                nw_ref, y_ref, state):
    L = CHUNK

    @pl.when(pl.program_id(1) == 0)
    def _():
        state[...] = jnp.zeros(state.shape, F32)

    a_row = -jnp.exp(alog_row_ref[...])
    a_col = -jnp.exp(alog_col_ref[...])
    ri = lax.broadcasted_iota(jnp.int32, (L, L), 0)
    ci = lax.broadcasted_iota(jnp.int32, (L, L), 1)
    causal = ci <= ri
    tril = jnp.where(causal, 1.0, 0.0).astype(BF16)
    triu = jnp.where(ri <= ci, 1.0, 0.0).astype(BF16)
    gs = N_SSD_GROUPS * SSD_STATE
    gw = SSD_WIDTH // N_SSD_GROUPS

    n_pairs = SSD_WIDTH // LANES
    pairs_per_group = n_pairs // N_SSD_GROUPS
    head0 = lax.broadcasted_iota(jnp.int32, (L, LANES), 1) < SSD_HEAD_DIM
    head0_row = head0[:1, :]
    spread = (lax.broadcasted_iota(jnp.int32, (LANES, SSD_WIDTH), 0)
              == lax.broadcasted_iota(jnp.int32, (LANES, SSD_WIDTH), 1) // SSD_HEAD_DIM)
    spread = jnp.where(spread, 1.0, 0.0).astype(BF16)

    states = [state[p] for p in range(n_pairs)]
    for chunk in range(act_ref.shape[0] // L):
        rows = pl.ds(chunk * L, L)
        act = act_ref[rows, :]
        xs = act[:, :SSD_WIDTH]
        xs_b = xs.astype(BF16)
        bm = act[:, SSD_WIDTH:SSD_WIDTH + gs]
        cm = act[:, SSD_WIDTH + gs:SSD_WIDTH + 2 * gs]

        dt_c = dtc_ref[rows, :]
        dt_r = dtr_ref[:, chunk * L:(chunk + 1) * L]
        acs_c = _split_dot(tril, dt_c * a_row, 3, lhs_split=False)
        acs_r = _split_dot(dt_r * a_col, triu, 3)
        e_wide = _split_dot(jnp.exp(acs_c), spread, 2)

        ys = []
        for g in range(N_SSD_GROUPS):
            b_g = bm[:, g * SSD_STATE:(g + 1) * SSD_STATE]
            c_gb = cm[:, g * SSD_STATE:(g + 1) * SSD_STATE].astype(BF16)
            cb = _dot_nt(c_gb, b_g.astype(BF16))
            b_gt = b_g.T
            for q in range(pairs_per_group):
                p = g * pairs_per_group + q
                x_pair = xs_b[:, p * LANES:(p + 1) * LANES]
                zero = jnp.zeros_like(x_pair)
                x_split = jnp.concatenate([jnp.where(head0, x_pair, zero),
                                           jnp.where(head0, zero, x_pair)], axis=0)
                w_diag, w_state, last = [], [], []
                for h in (2 * p, 2 * p + 1):
                    acs_col = acs_c[:, h:h + 1]
                    acs_row = acs_r[h:h + 1, :]
                    dt_row = dt_r[h:h + 1, :]
                    seg = jnp.exp(jnp.where(causal, acs_col - acs_row, NEG_BIG))
                    w_diag.append((cb * seg * dt_row).astype(BF16))
                    acs_last = acs_row[:, L - 1:L]
                    w_state.append((b_gt * (jnp.exp(acs_last - acs_row) * dt_row)).astype(BF16))
                    last.append(jnp.exp(acs_last))
                y_pair = (_dot(jnp.concatenate(w_diag, axis=1), x_split)
                          + e_wide[:, p * LANES:(p + 1) * LANES]
                          * _dot(c_gb, states[p].astype(BF16)))
                states[p] = (jnp.where(head0_row, last[0], last[1]) * states[p]
                             + _dot(jnp.concatenate(w_state, axis=1), x_split))
                ys.append(y_pair)
        y = (jnp.concatenate(ys, axis=1) + dskip_ref[...] * xs) * gate_ref[rows, :]

        outs = []
        for g in range(N_SSD_GROUPS):
            yg = y[:, g * gw:(g + 1) * gw]
            outs.append(yg * lax.rsqrt(jnp.mean(yg * yg, axis=-1, keepdims=True) + EPS))
        y_ref[rows, :] = (jnp.concatenate(outs, axis=1) * nw_ref[...]).astype(y_ref.dtype)

    for p in range(n_pairs):
        state[p] = states[p]


def _ssd(act, gate, dtc, dtr, alog_row, alog_col, dskip, nw):
    b, s, _ = act.shape
    rows = SSD_CHUNKS_PER_STEP * CHUNK
    tok = lambda width: pl.BlockSpec((None, rows, width), lambda i, c: (i, c, 0))
    return pl.pallas_call(
        _ssd_kernel,
        grid=(b, s // rows),
        in_specs=[
            tok(CONV_CHANNELS), tok(SSD_WIDTH), tok(LANES),
            pl.BlockSpec((None, DT_ROWS, rows), lambda i, c: (i, 0, c)),
            _resident(alog_row.shape), _resident(alog_col.shape),
            _resident(dskip.shape), _resident(nw.shape),
        ],
        out_specs=tok(SSD_WIDTH),
        out_shape=jax.ShapeDtypeStruct((b, s, SSD_WIDTH), BF16),
        scratch_shapes=[pltpu.VMEM((SSD_WIDTH // LANES, SSD_STATE, LANES), F32)],
        compiler_params=pltpu.CompilerParams(
            dimension_semantics=("parallel", "arbitrary"), vmem_limit_bytes=VMEM_LIMIT),
        name="ssd",
    )(act, gate, dtc, dtr, alog_row, alog_col, dskip, nw)


def _out_mlp_kernel(x_ref, oa_ref, ys_ref, woa_ref, wos_ref, nw_ref, wup_ref, wdn_ref, o_ref):
    x1 = x_ref[...] + _dot(oa_ref[...], woa_ref[...]) + _dot(ys_ref[...], wos_ref[...])
    hm = x1 * lax.rsqrt(jnp.mean(x1 * x1, axis=-1, keepdims=True) + EPS) * nw_ref[...]
    up = _dot(hm.astype(BF16), wup_ref[...])
    act = jnp.square(jnp.maximum(up, 0.0)).astype(BF16)
    o_ref[...] = x1 + _dot(act, wdn_ref[...])


def _out_mlp(x, o_attn, y_ssd, wo_a, wo_s, mlp_norm_w, w_up, w_down, tm):
    t, d = x.shape
    tok = lambda width: pl.BlockSpec((tm, width), lambda i: (i, 0))
    return pl.pallas_call(
        _out_mlp_kernel,
        grid=(t // tm,),
        in_specs=[
            tok(d), tok(ATTN_WIDTH), tok(SSD_WIDTH),
            _resident(wo_a.shape), _resident(wo_s.shape), _resident((1, d)),
            _resident(w_up.shape), _resident(w_down.shape),
        ],
        out_specs=tok(d),
        out_shape=jax.ShapeDtypeStruct((t, d), F32),
        compiler_params=pltpu.CompilerParams(
            dimension_semantics=("parallel",), vmem_limit_bytes=VMEM_LIMIT),
        name="out_mlp",
    )(x, o_attn, y_ssd, wo_a, wo_s, mlp_norm_w, w_up, w_down)


def _rope_tables(s):
    half = HEAD_DIM // 2
    inv_freq = ROPE_THETA ** (-jnp.arange(half, dtype=F32) / half)
    lane_freq = jnp.tile(inv_freq, LANES // half)
    sign = jnp.tile(jnp.repeat(jnp.array([-1.0, 1.0], F32), half), LANES // HEAD_DIM)
    ang = jnp.arange(s, dtype=F32)[:, None] * lane_freq[None, :]
    return jnp.cos(ang), jnp.sin(ang) * sign[None, :]


def _pad_lanes(row):
    return jnp.pad(row, ((0, 0), (0, LANES - row.shape[1])))


def _layer(x, attn_norm_w, w_in, q_norm_w, k_norm_w, conv_w, conv_b, dt_bias, a_log, d_skip,
           ssd_norm_w, w_out, mlp_norm_w, w_up, w_down):
    b, s, d = x.shape
    w_main = w_in[:, :MAIN_WIDTH].astype(BF16)
    w_dt = w_in[:, MAIN_WIDTH:]
    w_dt_c = jnp.pad(w_dt, ((0, 0), (0, LANES - N_SSD_HEADS))).astype(BF16)
    head_of = jnp.arange(ATTN_WIDTH, dtype=jnp.int32) // HEAD_DIM
    group_mean = jnp.where(head_of[:, None] == head_of[None, :], 1.0 / HEAD_DIM, 0.0).astype(BF16)
    cos, sin = _rope_tables(s)
    q, k, v, gate, act, dtc, dtr = _in_proj(
        x, attn_norm_w[None, :], w_main, w_dt_c, group_mean,
        jnp.tile(q_norm_w, N_ATTN_HEADS)[None, :], jnp.tile(k_norm_w, N_ATTN_HEADS)[None, :],
        cos, sin, conv_w, conv_b[None, :], _pad_lanes(dt_bias[None, :]), tm=PROJ_TOKENS)

    o_attn = _attention(q, k, v)

    col = lambda p: jnp.pad(p[:, None], ((0, DT_ROWS - N_SSD_HEADS), (0, 0)))
    y_ssd = _ssd(act, gate, dtc, dtr, _pad_lanes(a_log[None, :]), col(a_log),
                 jnp.repeat(d_skip, SSD_HEAD_DIM)[None, :], ssd_norm_w[None, :])

    out = _out_mlp(x.reshape(b * s, d), o_attn.reshape(b * s, ATTN_WIDTH),
                   y_ssd.reshape(b * s, SSD_WIDTH),
                   w_out[:ATTN_WIDTH].astype(BF16), w_out[ATTN_WIDTH:].astype(BF16),
                   mlp_norm_w[None, :], w_up.astype(BF16), w_down.astype(BF16), tm=MLP_TOKENS)
    return out.reshape(b, s, d)


def kernel(x, attn_norm_w, w_in, q_norm_w, k_norm_w, conv_w, conv_b, dt_bias, a_log, d_skip,
           ssd_norm_w, w_out, mlp_norm_w, w_up, w_down):
    for i in range(attn_norm_w.shape[0]):
        x = _layer(x, attn_norm_w[i], w_in[i], q_norm_w[i], k_norm_w[i], conv_w[i], conv_b[i],
                   dt_bias[i], a_log[i], d_skip[i], ssd_norm_w[i], w_out[i], mlp_norm_w[i],
                   w_up[i], w_down[i])
    return x
```

```python
import functools

import jax
import jax.numpy as jnp
from jax import lax
from jax.experimental import pallas as pl
from jax.experimental.pallas import tpu as pltpu

F32 = jnp.float32
BF16 = jnp.bfloat16

HEAD_DIM = 64
N_ATTN_HEADS = 8
ATTN_WIDTH = N_ATTN_HEADS * HEAD_DIM
ROPE_THETA = 10000.0
Q_BLOCK = 128
DILATIONS = (1, 4, 16)
FAR_DIL = 4
FAR_RATIO = 4
ATTN_LOOKAHEAD = 5
SSD_HEAD_DIM = 64
N_SSD_HEADS = 8
SSD_WIDTH = N_SSD_HEADS * SSD_HEAD_DIM
N_SSD_GROUPS = 2
HEADS_PER_GROUP = N_SSD_HEADS // N_SSD_GROUPS
SSD_STATE = 128
CONV_WIDTH = 4
CHUNK = 128
SSD_CHUNKS_PER_STEP = 16
CONV_CHANNELS = SSD_WIDTH + 2 * N_SSD_GROUPS * SSD_STATE
CONV_HISTORY = 16
EPS = 1e-6
NEG_BIG = -1e30
SCORE_SCALE = HEAD_DIM ** -0.5 * 1.4426950408889634

LANES = 128
DT_ROWS = 16
MAIN_WIDTH = 3 * ATTN_WIDTH + SSD_WIDTH + CONV_CHANNELS
VMEM_LIMIT = 56 * 1024 * 1024
PROJ_TOKENS = 512
MLP_TOKENS = 512


def _dot(a, b):
    return jnp.dot(a, b, preferred_element_type=F32)


def _dot_nt(a, b):
    return lax.dot_general(a, b, (((1,), (1,)), ((), ())), preferred_element_type=F32)


def _split_dot(a, b, parts, lhs_split=True):
    src = a if lhs_split else b
    acc = None
    rem = src
    for _ in range(parts):
        piece = rem.astype(BF16)
        rem = rem - piece.astype(F32)
        term = _dot(piece, b) if lhs_split else _dot(a, piece)
        acc = term if acc is None else acc + term
    return acc


def _resident(shape):
    zeros = (0,) * len(shape)
    return pl.BlockSpec(shape, lambda *_: zeros, pipeline_mode=pl.Buffered(1))


def _in_proj_kernel(x_ref, xprev_ref, nw_ref, w_ref, wdt_ref, gm_ref, qw_ref, kw_ref,
                    cos_ref, sin_ref, cw_ref, cb_ref, dtb_ref,
                    q_ref, k_ref, v_ref, gate_ref, act_ref, dtc_ref, dtr_ref, ubuf):
    tm = x_ref.shape[0]

    def normed(x):
        y = x * lax.rsqrt(jnp.mean(x * x, axis=-1, keepdims=True) + EPS) * nw_ref[...]
        return y.astype(BF16)

    hb = normed(x_ref[...])

    lane = lax.broadcasted_iota(jnp.int32, (tm, ATTN_WIDTH), 1)
    first_half = (lane % HEAD_DIM) < (HEAD_DIM // 2)
    cos = jnp.concatenate([cos_ref[...]] * (ATTN_WIDTH // LANES), axis=1)
    sin = jnp.concatenate([sin_ref[...]] * (ATTN_WIDTH // LANES), axis=1)

    def head_norm_rope(t, w):
        ms = _dot((t * t).astype(BF16), gm_ref[...])
        n = t * lax.rsqrt(ms + EPS) * w
        partner = jnp.where(first_half,
                            pltpu.roll(n, ATTN_WIDTH - HEAD_DIM // 2, 1),
                            pltpu.roll(n, HEAD_DIM // 2, 1))
        return n * cos + partner * sin

    o = 0
    q = _dot_nt(hb, w_ref[o:o + ATTN_WIDTH, :]); o += ATTN_WIDTH
    q_ref[...] = (head_norm_rope(q, qw_ref[...]) * SCORE_SCALE).astype(BF16)
    k = _dot_nt(hb, w_ref[o:o + ATTN_WIDTH, :]); o += ATTN_WIDTH
    k_ref[...] = head_norm_rope(k, kw_ref[...]).astype(BF16)
    v_ref[...] = _dot_nt(hb, w_ref[o:o + ATTN_WIDTH, :]).astype(BF16); o += ATTN_WIDTH
    gate_ref[...] = _silu(_dot_nt(hb, w_ref[o:o + SSD_WIDTH, :])); o += SSD_WIDTH

    w_xbc = w_ref[o:o + CONV_CHANNELS, :]
    u = _dot_nt(hb, w_xbc)
    hist = _dot_nt(normed(xprev_ref[...]), w_xbc)
    ubuf[0:CONV_HISTORY, :] = jnp.where(pl.program_id(0) == 0, 0.0, hist)
    ubuf[CONV_HISTORY:CONV_HISTORY + tm, :] = u
    conv = cb_ref[...] + cw_ref[CONV_WIDTH - 1:CONV_WIDTH, :] * u
    for tap in range(CONV_WIDTH - 1):
        first = CONV_HISTORY - (CONV_WIDTH - 1 - tap)
        conv = conv + cw_ref[tap:tap + 1, :] * ubuf[first:first + tm, :]
    act_ref[...] = _silu(conv)

    dtc = _softplus(_dot_nt(hb, wdt_ref[...]) + dtb_ref[...])
    dtc_ref[...] = dtc
    dtr_ref[...] = dtc.T[:DT_ROWS, :]


def _in_proj(x, attn_norm_w, w_main, w_dt, group_mean, qw, kw, cos, sin, conv_w, conv_b, dt_bias,
             tm):
    b, s, d = x.shape
    grid = (s // tm, b)
    tok = lambda width: pl.BlockSpec((None, tm, width), lambda j, i: (i, j, 0))
    prev = pl.BlockSpec((None, CONV_HISTORY, d),
                        lambda j, i: (i, jnp.maximum(j * (tm // CONV_HISTORY) - 1, 0), 0))
    out_shapes = (
        jax.ShapeDtypeStruct((b, s, ATTN_WIDTH), BF16),
        jax.ShapeDtypeStruct((b, s, ATTN_WIDTH), BF16),
        jax.ShapeDtypeStruct((b, s, ATTN_WIDTH), BF16),
        jax.ShapeDtypeStruct((b, s, SSD_WIDTH), F32),
        jax.ShapeDtypeStruct((b, s, CONV_CHANNELS), F32),
        jax.ShapeDtypeStruct((b, s, LANES), F32),
        jax.ShapeDtypeStruct((b, DT_ROWS, s), F32),
    )
    return pl.pallas_call(
        _in_proj_kernel,
        grid=grid,
        in_specs=[
            tok(d),
            prev,
            _resident((1, d)),
            _resident(w_main.shape),
            _resident(w_dt.shape),
            _resident(group_mean.shape),
            _resident((1, ATTN_WIDTH)),
            _resident((1, ATTN_WIDTH)),
            pl.BlockSpec((tm, LANES), lambda j, i: (j, 0)),
            pl.BlockSpec((tm, LANES), lambda j, i: (j, 0)),
            _resident(conv_w.shape), _resident(conv_b.shape), _resident(dt_bias.shape),
        ],
        out_specs=(
            tok(ATTN_WIDTH), tok(ATTN_WIDTH), tok(ATTN_WIDTH), tok(SSD_WIDTH),
            tok(CONV_CHANNELS), tok(LANES),
            pl.BlockSpec((None, DT_ROWS, tm), lambda j, i: (i, 0, j)),
        ),
        out_shape=out_shapes,
        scratch_shapes=[pltpu.VMEM((CONV_HISTORY + tm, CONV_CHANNELS), F32)],
        compiler_params=pltpu.CompilerParams(
            dimension_semantics=("parallel", "parallel"), vmem_limit_bytes=VMEM_LIMIT),
        name="in_proj",
    )(x, x, attn_norm_w, w_main, w_dt, group_mean, qw, kw, cos, sin, conv_w, conv_b, dt_bias)


def _attn_blocks(blocks, store):
    qb = Q_BLOCK
    n_pairs = ATTN_WIDTH // LANES
    head0 = lax.broadcasted_iota(jnp.int32, (qb, LANES), 1) < HEAD_DIM
    items = [(i, pair) for i in range(len(blocks)) for pair in range(n_pairs)]

    def scores(item):
        i, pair = item
        load_q, load_k, _, load_bias = blocks[i]
        qt, kb, bias = load_q(pair), load_k(pair), load_bias()
        out = []
        for h in range(2):
            mine = head0 if h == 0 else jnp.logical_not(head0)
            qh = jnp.where(mine, qt, jnp.zeros_like(qt))
            out.append(_dot_nt(qh, kb) + bias)
        return out

    queue = [scores(item) for item in items[:ATTN_LOOKAHEAD]]
    parts = []
    for n, (i, pair) in enumerate(items):
        if n + ATTN_LOOKAHEAD < len(items):
            queue.append(scores(items[n + ATTN_LOOKAHEAD]))
        pending = queue.pop(0)
        vb = blocks[i][2](pair)
        v_ext = jnp.concatenate([vb, jnp.ones_like(vb)], axis=1)
        heads = []
        for sc in pending:
            m = jnp.max(sc, axis=1, keepdims=True)
            r = _dot(jnp.exp2(sc - m).astype(BF16), v_ext)
            heads.append((r[:, :LANES], r[:, LANES:], jnp.broadcast_to(m, (qb, LANES))))
        parts.append([jnp.where(head0, a, b) for a, b in zip(*heads)])
        if pair == n_pairs - 1:
            store(i, *[jnp.concatenate(t, axis=1) for t in zip(*parts)])
            parts = []


def _near_bias(first):
    qb = Q_BLOCK
    row = lax.broadcasted_iota(jnp.int32, (qb, 2 * qb), 0)
    col = lax.broadcasted_iota(jnp.int32, (qb, 2 * qb), 1)
    ok = (col <= row) if first else ((col >= row) & (col <= row + qb))
    return jnp.where(ok, 0.0, NEG_BIG)


def _far_bias(blocks_back):
    qb = Q_BLOCK
    row = lax.broadcasted_iota(jnp.int32, (qb, qb), 0)
    col = lax.broadcasted_iota(jnp.int32, (qb, qb), 1)
    dist = row - col + blocks_back * qb
    in_window = (dist >= 0) & (dist <= qb)
    on_stride = (dist >= 0) & ((dist & (FAR_RATIO - 1)) == 0)
    return jnp.where(in_window & on_stride, 1.0,
                     jnp.where(in_window | on_stride, 0.0, NEG_BIG))


def _attn_kernel(q_ref, k_ref, v_ref, o_ref,
                 tmp, q_cls, k_cls, v_cls, num_far, den_far, top_far, near_bias, far_bias):
    qb = Q_BLOCK
    step = pl.program_id(1)
    rows = q_ref.shape[0]
    s = k_ref.shape[0]
    n_steps = s // rows
    n_pairs = ATTN_WIDTH // LANES
    lanes = lambda pair: pl.ds(pair * LANES, LANES)
    far = (num_far, den_far, top_far)

    def to_class_order(src_rows, dst, at):
        for p in range(n_pairs):
            tmp[p] = src_rows[:, p * LANES:(p + 1) * LANES].astype(F32)
            for c in range(FAR_DIL):
                dst[c, at:at + qb, lanes(p)] = tmp[p, pl.ds(c, qb, stride=FAR_DIL), :].astype(BF16)

    @pl.when(step == 0)
    def _():
        for i in range(n_steps):
            to_class_order(k_ref[i * rows:(i + 1) * rows, :], k_cls, i * qb)
            to_class_order(v_ref[i * rows:(i + 1) * rows, :], v_cls, i * qb)

    to_class_order(q_ref[...], q_cls, 0)
    near_bias[0] = _near_bias(False)
    near_bias[1] = _near_bias(True)
    for d in range(3):
        far_bias[d] = _far_bias(d)

    def far_store(c, *stats):
        for dst, t in zip(far, stats):
            for p in range(n_pairs):
                dst[p, pl.ds(c, qb, stride=FAR_DIL), :] = t[:, p * LANES:(p + 1) * LANES]

    for n_kb in range(1, n_steps + 1):
        @pl.when(step == n_kb - 1)
        def _(n_kb=n_kb):
            def block(c):
                load_bias = lambda: jnp.concatenate(
                    [far_bias[min(n_kb - 1 - kb, 2)] for kb in range(n_kb)], axis=1)
                return (lambda p: q_cls[c, :, lanes(p)],
                        lambda p: k_cls[c, 0:n_kb * qb, lanes(p)],
                        lambda p: v_cls[c, 0:n_kb * qb, lanes(p)], load_bias)
            _attn_blocks([block(c) for c in range(FAR_DIL)], far_store)

    is_first = (step == 0).astype(jnp.int32)

    def near_block(j):
        q_rows = pl.ds(j * qb, qb)
        if j == 0:
            start = pl.multiple_of((step * FAR_DIL - 1 + is_first) * qb, qb)
            load_bias = lambda: near_bias[is_first]
        else:
            start = pl.multiple_of((step * FAR_DIL + j - 1) * qb, qb)
            load_bias = lambda: near_bias[0]
        kv_rows = pl.ds(start, 2 * qb)
        return (lambda p: q_ref[q_rows, lanes(p)], lambda p: k_ref[kv_rows, lanes(p)],
                lambda p: v_ref[kv_rows, lanes(p)], load_bias)

    def near_store(j, num, den, top):
        q_rows = pl.ds(j * qb, qb)
        num_f, den_f, top_f = [
            jnp.concatenate([t[p, q_rows, :] for p in range(n_pairs)], axis=1) for t in far]
        both = jnp.maximum(top, top_f)
        w_near, w_far = jnp.exp2(top - both), jnp.exp2(top_f - both)
        o_ref[q_rows, :] = ((w_near * num + w_far * num_f)
                            / (w_near * den + w_far * den_f)).astype(o_ref.dtype)

    _attn_blocks([near_block(j) for j in range(rows // qb)], near_store)


def _attention(q, k, v):
    b, s, w = q.shape
    assert DILATIONS == (1, FAR_DIL, FAR_DIL * FAR_RATIO) and s == Q_BLOCK * DILATIONS[-1]
    rows = FAR_DIL * Q_BLOCK
    q_blk = pl.BlockSpec((None, rows, w), lambda i, g: (i, g, 0))
    kv_blk = pl.BlockSpec((None, s, w), lambda i, g: (i, 0, 0))
    return pl.pallas_call(
        _attn_kernel,
        grid=(b, s // rows),
        in_specs=[q_blk, kv_blk, kv_blk],
        out_specs=q_blk,
        out_shape=jax.ShapeDtypeStruct((b, s, w), BF16),
        scratch_shapes=[
            pltpu.VMEM((w // LANES, rows, LANES), F32),
            pltpu.VMEM((FAR_DIL, Q_BLOCK, w), BF16),
            pltpu.VMEM((FAR_DIL, s // FAR_DIL, w), BF16),
            pltpu.VMEM((FAR_DIL, s // FAR_DIL, w), BF16),
            pltpu.VMEM((w // LANES, rows, LANES), F32),
            pltpu.VMEM((w // LANES, rows, LANES), F32),
            pltpu.VMEM((w // LANES, rows, LANES), F32),
            pltpu.VMEM((2, Q_BLOCK, 2 * Q_BLOCK), F32),
            pltpu.VMEM((3, Q_BLOCK, Q_BLOCK), F32),
        ],
        compiler_params=pltpu.CompilerParams(
            dimension_semantics=("parallel", "arbitrary"), vmem_limit_bytes=VMEM_LIMIT),
        name="dilated_attn",
    )(q, k, v)


def _softplus(x):
    return jnp.maximum(x, 0.0) + jnp.log1p(jnp.exp(-jnp.abs(x)))


def _silu(x):
    return x * (1.0 / (1.0 + jnp.exp(-x)))


def _ssd_kernel(act_ref, gate_ref, dtc_ref, dtr_ref, alog_row_ref, alog_col_ref, dskip_ref,
                nw_ref, y_ref, state):
    L = CHUNK

    @pl.when(pl.program_id(1) == 0)
    def _():
        state[...] = jnp.zeros(state.shape, F32)

    a_row = -jnp.exp(alog_row_ref[...])
    a_col = -jnp.exp(alog_col_ref[...])
    ri = lax.broadcasted_iota(jnp.int32, (L, L), 0)
    ci = lax.broadcasted_iota(jnp.int32, (L, L), 1)
    causal = ci <= ri
    tril = jnp.where(causal, 1.0, 0.0).astype(BF16)
    triu = jnp.where(ri <= ci, 1.0, 0.0).astype(BF16)
    gs = N_SSD_GROUPS * SSD_STATE
    gw = SSD_WIDTH // N_SSD_GROUPS

    n_pairs = SSD_WIDTH // LANES
    pairs_per_group = n_pairs // N_SSD_GROUPS
    head0 = lax.broadcasted_iota(jnp.int32, (L, LANES), 1) < SSD_HEAD_DIM
    head0_row = head0[:1, :]
    spread = (lax.broadcasted_iota(jnp.int32, (LANES, SSD_WIDTH), 0)
              == lax.broadcasted_iota(jnp.int32, (LANES, SSD_WIDTH), 1) // SSD_HEAD_DIM)
    spread = jnp.where(spread, 1.0, 0.0).astype(BF16)

    states = [state[p] for p in range(n_pairs)]
    for chunk in range(act_ref.shape[0] // L):
        rows = pl.ds(chunk * L, L)
        act = act_ref[rows, :]
        xs = act[:, :SSD_WIDTH]
        xs_b = xs.astype(BF16)
        bm = act[:, SSD_WIDTH:SSD_WIDTH + gs]
        cm = act[:, SSD_WIDTH + gs:SSD_WIDTH + 2 * gs]

        dt_c = dtc_ref[rows, :]
        dt_r = dtr_ref[:, chunk * L:(chunk + 1) * L]
        acs_c = _split_dot(tril, dt_c * a_row, 3, lhs_split=False)
        acs_r = _split_dot(dt_r * a_col, triu, 3)
        e_wide = _split_dot(jnp.exp(acs_c), spread, 2)

        ys = []
        for g in range(N_SSD_GROUPS):
            b_g = bm[:, g * SSD_STATE:(g + 1) * SSD_STATE]
            c_gb = cm[:, g * SSD_STATE:(g + 1) * SSD_STATE].astype(BF16)
            cb = _dot_nt(c_gb, b_g.astype(BF16))
            b_gt = b_g.T
            for q in range(pairs_per_group):
                p = g * pairs_per_group + q
                x_pair = xs_b[:, p * LANES:(p + 1) * LANES]
                zero = jnp.zeros_like(x_pair)
                x_split = jnp.concatenate([jnp.where(head0, x_pair, zero),
                                           jnp.where(head0, zero, x_pair)], axis=0)
                w_diag, w_state, last = [], [], []
                for h in (2 * p, 2 * p + 1):
                    acs_col = acs_c[:, h:h + 1]
                    acs_row = acs_r[h:h + 1, :]
                    dt_row = dt_r[h:h + 1, :]
                    seg = jnp.exp(jnp.where(causal, acs_col - acs_row, NEG_BIG))
                    w_diag.append((cb * seg * dt_row).astype(BF16))
                    acs_last = acs_row[:, L - 1:L]
                    w_state.append((b_gt * (jnp.exp(acs_last - acs_row) * dt_row)).astype(BF16))
                    last.append(jnp.exp(acs_last))
                y_pair = (_dot(jnp.concatenate(w_diag, axis=1), x_split)
                          + e_wide[:, p * LANES:(p + 1) * LANES]
                          * _dot(c_gb, states[p].astype(BF16)))
                states[p] = (jnp.where(head0_row, last[0], last[1]) * states[p]
                             + _dot(jnp.concatenate(w_state, axis=1), x_split))
                ys.append(y_pair)
        y = (jnp.concatenate(ys, axis=1) + dskip_ref[...] * xs) * gate_ref[rows, :]

        outs = []
        for g in range(N_SSD_GROUPS):
            yg = y[:, g * gw:(g + 1) * gw]
            outs.append(yg * lax.rsqrt(jnp.mean(yg * yg, axis=-1, keepdims=True) + EPS))
        y_ref[rows, :] = (jnp.concatenate(outs, axis=1) * nw_ref[...]).astype(y_ref.dtype)

    for p in range(n_pairs):
        state[p] = states[p]


def _ssd(act, gate, dtc, dtr, alog_row, alog_col, dskip, nw):
    b, s, _ = act.shape
    rows = SSD_CHUNKS_PER_STEP * CHUNK
    tok = lambda width: pl.BlockSpec((None, rows, width), lambda i, c: (i, c, 0))
    return pl.pallas_call(
        _ssd_kernel,
        grid=(b, s // rows),
        in_specs=[
            tok(CONV_CHANNELS), tok(SSD_WIDTH), tok(LANES),
            pl.BlockSpec((None, DT_ROWS, rows), lambda i, c: (i, 0, c)),
            _resident(alog_row.shape), _resident(alog_col.shape),
            _resident(dskip.shape), _resident(nw.shape),
        ],
        out_specs=tok(SSD_WIDTH),
        out_shape=jax.ShapeDtypeStruct((b, s, SSD_WIDTH), BF16),
        scratch_shapes=[pltpu.VMEM((SSD_WIDTH // LANES, SSD_STATE, LANES), F32)],
        compiler_params=pltpu.CompilerParams(
            dimension_semantics=("parallel", "arbitrary"), vmem_limit_bytes=VMEM_LIMIT),
        name="ssd",
    )(act, gate, dtc, dtr, alog_row, alog_col, dskip, nw)


def _out_mlp_kernel(x_ref, oa_ref, ys_ref, woa_ref, wos_ref, nw_ref, wup_ref, wdn_ref, o_ref):
    x1 = x_ref[...] + _dot(oa_ref[...], woa_ref[...]) + _dot(ys_ref[...], wos_ref[...])
    hm = x1 * lax.rsqrt(jnp.mean(x1 * x1, axis=-1, keepdims=True) + EPS) * nw_ref[...]
    up = _dot(hm.astype(BF16), wup_ref[...])
    act = jnp.square(jnp.maximum(up, 0.0)).astype(BF16)
    o_ref[...] = x1 + _dot(act, wdn_ref[...])


def _out_mlp(x, o_attn, y_ssd, wo_a, wo_s, mlp_norm_w, w_up, w_down, tm):
    t, d = x.shape
    tok = lambda width: pl.BlockSpec((tm, width), lambda i: (i, 0))
    return pl.pallas_call(
        _out_mlp_kernel,
        grid=(t // tm,),
        in_specs=[
            tok(d), tok(ATTN_WIDTH), tok(SSD_WIDTH),
            _resident(wo_a.shape), _resident(wo_s.shape), _resident((1, d)),
            _resident(w_up.shape), _resident(w_down.shape),
        ],
        out_specs=tok(d),
        out_shape=jax.ShapeDtypeStruct((t, d), F32),
        compiler_params=pltpu.CompilerParams(
            dimension_semantics=("parallel",), vmem_limit_bytes=VMEM_LIMIT),
        name="out_mlp",
    )(x, o_attn, y_ssd, wo_a, wo_s, mlp_norm_w, w_up, w_down)


def _rope_tables(s):
    half = HEAD_DIM // 2
    inv_freq = ROPE_THETA ** (-jnp.arange(half, dtype=F32) / half)
    lane_freq = jnp.tile(inv_freq, LANES // half)
    sign = jnp.tile(jnp.repeat(jnp.array([-1.0, 1.0], F32), half), LANES // HEAD_DIM)
    ang = jnp.arange(s, dtype=F32)[:, None] * lane_freq[None, :]
    return jnp.cos(ang), jnp.sin(ang) * sign[None, :]


def _pad_lanes(row):
    return jnp.pad(row, ((0, 0), (0, LANES - row.shape[1])))


def _layer(x, attn_norm_w, w_in, q_norm_w, k_norm_w, conv_w, conv_b, dt_bias, a_log, d_skip,
           ssd_norm_w, w_out, mlp_norm_w, w_up, w_down):
    b, s, d = x.shape
    w_t = w_in.T
    w_main = w_t[:MAIN_WIDTH].astype(BF16)
    w_dt_c = jnp.pad(w_t[MAIN_WIDTH:], ((0, LANES - N_SSD_HEADS), (0, 0))).astype(BF16)
    head_of = jnp.arange(ATTN_WIDTH, dtype=jnp.int32) // HEAD_DIM
    group_mean = jnp.where(head_of[:, None] == head_of[None, :], 1.0 / HEAD_DIM, 0.0).astype(BF16)
    cos, sin = _rope_tables(s)
    q, k, v, gate, act, dtc, dtr = _in_proj(
        x, attn_norm_w[None, :], w_main, w_dt_c, group_mean,
        jnp.tile(q_norm_w, N_ATTN_HEADS)[None, :], jnp.tile(k_norm_w, N_ATTN_HEADS)[None, :],
        cos, sin, conv_w, conv_b[None, :], _pad_lanes(dt_bias[None, :]), tm=PROJ_TOKENS)

    o_attn = _attention(q, k, v)

    col = lambda p: jnp.pad(p[:, None], ((0, DT_ROWS - N_SSD_HEADS), (0, 0)))
    y_ssd = _ssd(act, gate, dtc, dtr, _pad_lanes(a_log[None, :]), col(a_log),
                 jnp.repeat(d_skip, SSD_HEAD_DIM)[None, :], ssd_norm_w[None, :])

    out = _out_mlp(x.reshape(b * s, d), o_attn.reshape(b * s, ATTN_WIDTH),
                   y_ssd.reshape(b * s, SSD_WIDTH),
                   w_out[:ATTN_WIDTH].astype(BF16), w_out[ATTN_WIDTH:].astype(BF16),
                   mlp_norm_w[None, :], w_up.astype(BF16), w_down.astype(BF16), tm=MLP_TOKENS)
    return out.reshape(b, s, d)


def kernel(x, attn_norm_w, w_in, q_norm_w, k_norm_w, conv_w, conv_b, dt_bias, a_log, d_skip,
           ssd_norm_w, w_out, mlp_norm_w, w_up, w_down):
    for i in range(attn_norm_w.shape[0]):
        x = _layer(x, attn_norm_w[i], w_in[i], q_norm_w[i], k_norm_w[i], conv_w[i], conv_b[i],
                   dt_bias[i], a_log[i], d_skip[i], ssd_norm_w[i], w_out[i], mlp_norm_w[i],
                   w_up[i], w_down[i])
    return x
```

```python
import functools

import jax
import jax.numpy as jnp
from jax import lax
from jax.experimental import pallas as pl
from jax.experimental.pallas import tpu as pltpu

F32 = jnp.float32
BF16 = jnp.bfloat16

HEAD_DIM = 64
N_ATTN_HEADS = 8
ATTN_WIDTH = N_ATTN_HEADS * HEAD_DIM
ROPE_THETA = 10000.0
Q_BLOCK = 128
DILATIONS = (1, 4, 16)
FAR_DIL = 4
FAR_RATIO = 4
ATTN_LOOKAHEAD = 5
SSD_HEAD_DIM = 64
N_SSD_HEADS = 8
SSD_WIDTH = N_SSD_HEADS * SSD_HEAD_DIM
N_SSD_GROUPS = 2
HEADS_PER_GROUP = N_SSD_HEADS // N_SSD_GROUPS
SSD_STATE = 128
CONV_WIDTH = 4
CHUNK = 128
SSD_CHUNKS_PER_STEP = 16
CONV_CHANNELS = SSD_WIDTH + 2 * N_SSD_GROUPS * SSD_STATE
CONV_HISTORY = 16
EPS = 1e-6
NEG_BIG = -1e30
SCORE_SCALE = HEAD_DIM ** -0.5 * 1.4426950408889634

LANES = 128
DT_ROWS = 16
MAIN_WIDTH = 3 * ATTN_WIDTH + SSD_WIDTH + CONV_CHANNELS
VMEM_LIMIT = 56 * 1024 * 1024
PROJ_TOKENS = 1024
MLP_TOKENS = 512


def _dot(a, b):
    return jnp.dot(a, b, preferred_element_type=F32)


def _dot_nt(a, b):
    return lax.dot_general(a, b, (((1,), (1,)), ((), ())), preferred_element_type=F32)


def _split_dot(a, b, parts, lhs_split=True):
    src = a if lhs_split else b
    acc = None
    rem = src
    for _ in range(parts):
        piece = rem.astype(BF16)
        rem = rem - piece.astype(F32)
        term = _dot(piece, b) if lhs_split else _dot(a, piece)
        acc = term if acc is None else acc + term
    return acc


def _resident(shape):
    zeros = (0,) * len(shape)
    return pl.BlockSpec(shape, lambda *_: zeros, pipeline_mode=pl.Buffered(1))


def _in_proj_kernel(x_ref, xprev_ref, nw_ref, w_ref, wdt_ref, gm_ref, qw_ref, kw_ref,
                    cos_ref, sin_ref, cw_ref, cb_ref, dtb_ref,
                    q_ref, k_ref, v_ref, gate_ref, act_ref, dtc_ref, dtr_ref, ubuf):
    tm = x_ref.shape[0]

    def normed(x):
        y = x * lax.rsqrt(jnp.mean(x * x, axis=-1, keepdims=True) + EPS) * nw_ref[...]
        return y.astype(BF16)

    hb = normed(x_ref[...])

    lane = lax.broadcasted_iota(jnp.int32, (tm, ATTN_WIDTH), 1)
    first_half = (lane % HEAD_DIM) < (HEAD_DIM // 2)
    cos = jnp.concatenate([cos_ref[...]] * (ATTN_WIDTH // LANES), axis=1)
    sin = jnp.concatenate([sin_ref[...]] * (ATTN_WIDTH // LANES), axis=1)

    def head_norm_rope(t, w):
        ms = _dot((t * t).astype(BF16), gm_ref[...])
        n = t * lax.rsqrt(ms + EPS) * w
        partner = jnp.where(first_half,
                            pltpu.roll(n, ATTN_WIDTH - HEAD_DIM // 2, 1),
                            pltpu.roll(n, HEAD_DIM // 2, 1))
        return n * cos + partner * sin

    o = 0
    q = _dot_nt(hb, w_ref[o:o + ATTN_WIDTH, :]); o += ATTN_WIDTH
    q_ref[...] = (head_norm_rope(q, qw_ref[...]) * SCORE_SCALE).astype(BF16)
    k = _dot_nt(hb, w_ref[o:o + ATTN_WIDTH, :]); o += ATTN_WIDTH
    k_ref[...] = head_norm_rope(k, kw_ref[...]).astype(BF16)
    v_ref[...] = _dot_nt(hb, w_ref[o:o + ATTN_WIDTH, :]).astype(BF16); o += ATTN_WIDTH
    gate_ref[...] = _silu(_dot_nt(hb, w_ref[o:o + SSD_WIDTH, :])); o += SSD_WIDTH

    w_xbc = w_ref[o:o + CONV_CHANNELS, :]
    u = _dot_nt(hb, w_xbc)
    hist = _dot_nt(normed(xprev_ref[...]), w_xbc)
    ubuf[0:CONV_HISTORY, :] = jnp.where(pl.program_id(0) == 0, 0.0, hist)
    ubuf[CONV_HISTORY:CONV_HISTORY + tm, :] = u
    conv = cb_ref[...] + cw_ref[CONV_WIDTH - 1:CONV_WIDTH, :] * u
    for tap in range(CONV_WIDTH - 1):
        first = CONV_HISTORY - (CONV_WIDTH - 1 - tap)
        conv = conv + cw_ref[tap:tap + 1, :] * ubuf[first:first + tm, :]
    act_ref[...] = _silu(conv)

    dtc = _softplus(_dot_nt(hb, wdt_ref[...]) + dtb_ref[...])
    dtc_ref[...] = dtc
    dtr_ref[...] = dtc.T[:DT_ROWS, :]


def _in_proj(x, attn_norm_w, w_main, w_dt, group_mean, qw, kw, cos, sin, conv_w, conv_b, dt_bias,
             tm):
    b, s, d = x.shape
    grid = (s // tm, b)
    tok = lambda width: pl.BlockSpec((None, tm, width), lambda j, i: (i, j, 0))
    prev = pl.BlockSpec((None, CONV_HISTORY, d),
                        lambda j, i: (i, jnp.maximum(j * (tm // CONV_HISTORY) - 1, 0), 0))
    out_shapes = (
        jax.ShapeDtypeStruct((b, s, ATTN_WIDTH), BF16),
        jax.ShapeDtypeStruct((b, s, ATTN_WIDTH), BF16),
        jax.ShapeDtypeStruct((b, s, ATTN_WIDTH), BF16),
        jax.ShapeDtypeStruct((b, s, SSD_WIDTH), F32),
        jax.ShapeDtypeStruct((b, s, CONV_CHANNELS), F32),
        jax.ShapeDtypeStruct((b, s, LANES), F32),
        jax.ShapeDtypeStruct((b, DT_ROWS, s), F32),
    )
    return pl.pallas_call(
        _in_proj_kernel,
        grid=grid,
        in_specs=[
            tok(d),
            prev,
            _resident((1, d)),
            _resident(w_main.shape),
            _resident(w_dt.shape),
            _resident(group_mean.shape),
            _resident((1, ATTN_WIDTH)),
            _resident((1, ATTN_WIDTH)),
            pl.BlockSpec((tm, LANES), lambda j, i: (j, 0)),
            pl.BlockSpec((tm, LANES), lambda j, i: (j, 0)),
            _resident(conv_w.shape), _resident(conv_b.shape), _resident(dt_bias.shape),
        ],
        out_specs=(
            tok(ATTN_WIDTH), tok(ATTN_WIDTH), tok(ATTN_WIDTH), tok(SSD_WIDTH),
            tok(CONV_CHANNELS), tok(LANES),
            pl.BlockSpec((None, DT_ROWS, tm), lambda j, i: (i, 0, j)),
        ),
        out_shape=out_shapes,
        scratch_shapes=[pltpu.VMEM((CONV_HISTORY + tm, CONV_CHANNELS), F32)],
        compiler_params=pltpu.CompilerParams(
            dimension_semantics=("parallel", "parallel"), vmem_limit_bytes=VMEM_LIMIT),
        name="in_proj",
    )(x, x, attn_norm_w, w_main, w_dt, group_mean, qw, kw, cos, sin, conv_w, conv_b, dt_bias)


def _attn_blocks(blocks, store):
    qb = Q_BLOCK
    n_pairs = ATTN_WIDTH // LANES
    head0 = lax.broadcasted_iota(jnp.int32, (qb, LANES), 1) < HEAD_DIM
    items = [(i, pair) for i in range(len(blocks)) for pair in range(n_pairs)]

    def scores(item):
        i, pair = item
        load_q, load_k, _, load_bias = blocks[i]
        qt, kb, bias = load_q(pair), load_k(pair), load_bias()
        out = []
        for h in range(2):
            mine = head0 if h == 0 else jnp.logical_not(head0)
            qh = jnp.where(mine, qt, jnp.zeros_like(qt))
            out.append(_dot_nt(qh, kb) + bias)
        return out

    queue = [scores(item) for item in items[:ATTN_LOOKAHEAD]]
    parts = []
    for n, (i, pair) in enumerate(items):
        if n + ATTN_LOOKAHEAD < len(items):
            queue.append(scores(items[n + ATTN_LOOKAHEAD]))
        pending = queue.pop(0)
        vb = blocks[i][2](pair)
        v_ext = jnp.concatenate([vb, jnp.ones_like(vb)], axis=1)
        heads = []
        for sc in pending:
            m = jnp.max(sc, axis=1, keepdims=True)
            r = _dot(jnp.exp2(sc - m).astype(BF16), v_ext)
            heads.append((r[:, :LANES], r[:, LANES:], jnp.broadcast_to(m, (qb, LANES))))
        parts.append([jnp.where(head0, a, b) for a, b in zip(*heads)])
        if pair == n_pairs - 1:
            store(i, *[jnp.concatenate(t, axis=1) for t in zip(*parts)])
            parts = []


def _near_bias(first):
    qb = Q_BLOCK
    row = lax.broadcasted_iota(jnp.int32, (qb, 2 * qb), 0)
    col = lax.broadcasted_iota(jnp.int32, (qb, 2 * qb), 1)
    ok = (col <= row) if first else ((col >= row) & (col <= row + qb))
    return jnp.where(ok, 0.0, NEG_BIG)


def _far_bias(blocks_back):
    qb = Q_BLOCK
    row = lax.broadcasted_iota(jnp.int32, (qb, qb), 0)
    col = lax.broadcasted_iota(jnp.int32, (qb, qb), 1)
    dist = row - col + blocks_back * qb
    in_window = (dist >= 0) & (dist <= qb)
    on_stride = (dist >= 0) & ((dist & (FAR_RATIO - 1)) == 0)
    return jnp.where(in_window & on_stride, 1.0,
                     jnp.where(in_window | on_stride, 0.0, NEG_BIG))


def _attn_kernel(q_ref, k_ref, v_ref, o_ref,
                 tmp, q_cls, k_cls, v_cls, num_far, den_far, top_far, near_bias, far_bias):
    qb = Q_BLOCK
    step = pl.program_id(1)
    rows = q_ref.shape[0]
    s = k_ref.shape[0]
    n_steps = s // rows
    n_pairs = ATTN_WIDTH // LANES
    lanes = lambda pair: pl.ds(pair * LANES, LANES)
    far = (num_far, den_far, top_far)

    def to_class_order(src_rows, dst, at):
        for p in range(n_pairs):
            tmp[p] = src_rows[:, p * LANES:(p + 1) * LANES].astype(F32)
            for c in range(FAR_DIL):
                dst[c, at:at + qb, lanes(p)] = tmp[p, pl.ds(c, qb, stride=FAR_DIL), :].astype(BF16)

    @pl.when(step == 0)
    def _():
        for i in range(n_steps):
            to_class_order(k_ref[i * rows:(i + 1) * rows, :], k_cls, i * qb)
            to_class_order(v_ref[i * rows:(i + 1) * rows, :], v_cls, i * qb)

    to_class_order(q_ref[...], q_cls, 0)
    near_bias[0] = _near_bias(False)
    near_bias[1] = _near_bias(True)
    for d in range(3):
        far_bias[d] = _far_bias(d)

    def far_store(c, *stats):
        for dst, t in zip(far, stats):
            for p in range(n_pairs):
                dst[p, pl.ds(c, qb, stride=FAR_DIL), :] = t[:, p * LANES:(p + 1) * LANES]

    for n_kb in range(1, n_steps + 1):
        @pl.when(step == n_kb - 1)
        def _(n_kb=n_kb):
            def block(c):
                load_bias = lambda: jnp.concatenate(
                    [far_bias[min(n_kb - 1 - kb, 2)] for kb in range(n_kb)], axis=1)
                return (lambda p: q_cls[c, :, lanes(p)],
                        lambda p: k_cls[c, 0:n_kb * qb, lanes(p)],
                        lambda p: v_cls[c, 0:n_kb * qb, lanes(p)], load_bias)
            _attn_blocks([block(c) for c in range(FAR_DIL)], far_store)

    is_first = (step == 0).astype(jnp.int32)

    def near_block(j):
        q_rows = pl.ds(j * qb, qb)
        if j == 0:
            start = pl.multiple_of((step * FAR_DIL - 1 + is_first) * qb, qb)
            load_bias = lambda: near_bias[is_first]
        else:
            start = pl.multiple_of((step * FAR_DIL + j - 1) * qb, qb)
            load_bias = lambda: near_bias[0]
        kv_rows = pl.ds(start, 2 * qb)
        return (lambda p: q_ref[q_rows, lanes(p)], lambda p: k_ref[kv_rows, lanes(p)],
                lambda p: v_ref[kv_rows, lanes(p)], load_bias)

    def near_store(j, num, den, top):
        q_rows = pl.ds(j * qb, qb)
        num_f, den_f, top_f = [
            jnp.concatenate([t[p, q_rows, :] for p in range(n_pairs)], axis=1) for t in far]
        both = jnp.maximum(top, top_f)
        w_near, w_far = jnp.exp2(top - both), jnp.exp2(top_f - both)
        o_ref[q_rows, :] = ((w_near * num + w_far * num_f)
                            / (w_near * den + w_far * den_f)).astype(o_ref.dtype)

    _attn_blocks([near_block(j) for j in range(rows // qb)], near_store)


def _attention(q, k, v):
    b, s, w = q.shape
    assert DILATIONS == (1, FAR_DIL, FAR_DIL * FAR_RATIO) and s == Q_BLOCK * DILATIONS[-1]
    rows = FAR_DIL * Q_BLOCK
    q_blk = pl.BlockSpec((None, rows, w), lambda i, g: (i, g, 0))
    kv_blk = pl.BlockSpec((None, s, w), lambda i, g: (i, 0, 0))
    return pl.pallas_call(
        _attn_kernel,
        grid=(b, s // rows),
        in_specs=[q_blk, kv_blk, kv_blk],
        out_specs=q_blk,
        out_shape=jax.ShapeDtypeStruct((b, s, w), BF16),
        scratch_shapes=[
            pltpu.VMEM((w // LANES, rows, LANES), F32),
            pltpu.VMEM((FAR_DIL, Q_BLOCK, w), BF16),
            pltpu.VMEM((FAR_DIL, s // FAR_DIL, w), BF16),
            pltpu.VMEM((FAR_DIL, s // FAR_DIL, w), BF16),
            pltpu.VMEM((w // LANES, rows, LANES), F32),
            pltpu.VMEM((w // LANES, rows, LANES), F32),
            pltpu.VMEM((w // LANES, rows, LANES), F32),
            pltpu.VMEM((2, Q_BLOCK, 2 * Q_BLOCK), F32),
            pltpu.VMEM((3, Q_BLOCK, Q_BLOCK), F32),
        ],
        compiler_params=pltpu.CompilerParams(
            dimension_semantics=("parallel", "arbitrary"), vmem_limit_bytes=VMEM_LIMIT),
        name="dilated_attn",
    )(q, k, v)


def _softplus(x):
    return jnp.maximum(x, 0.0) + jnp.log1p(jnp.exp(-jnp.abs(x)))


def _silu(x):
    return x * (1.0 / (1.0 + jnp.exp(-x)))


def _ssd_kernel(act_ref, gate_ref, dtc_ref, dtr_ref, alog_row_ref, alog_col_ref, dskip_ref,
                nw_ref, y_ref, state):
    L = CHUNK

    @pl.when(pl.program_id(1) == 0)
    def _():
        state[...] = jnp.zeros(state.shape, F32)

    a_row = -jnp.exp(alog_row_ref[...])
    a_col = -jnp.exp(alog_col_ref[...])
    ri = lax.broadcasted_iota(jnp.int32, (L, L), 0)
    ci = lax.broadcasted_iota(jnp.int32, (L, L), 1)
    causal = ci <= ri
    tril = jnp.where(causal, 1.0, 0.0).astype(BF16)
    triu = jnp.where(ri <= ci, 1.0, 0.0).astype(BF16)
    gs = N_SSD_GROUPS * SSD_STATE
    gw = SSD_WIDTH // N_SSD_GROUPS

    n_pairs = SSD_WIDTH // LANES
    pairs_per_group = n_pairs // N_SSD_GROUPS
    head0 = lax.broadcasted_iota(jnp.int32, (L, LANES), 1) < SSD_HEAD_DIM
    head0_row = head0[:1, :]
    spread = (lax.broadcasted_iota(jnp.int32, (LANES, SSD_WIDTH), 0)
              == lax.broadcasted_iota(jnp.int32, (LANES, SSD_WIDTH), 1) // SSD_HEAD_DIM)
    spread = jnp.where(spread, 1.0, 0.0).astype(BF16)

    states = [state[p] for p in range(n_pairs)]
    for chunk in range(act_ref.shape[0] // L):
        rows = pl.ds(chunk * L, L)
        act = act_ref[rows, :]
        xs = act[:, :SSD_WIDTH]
        xs_b = xs.astype(BF16)
        bm = act[:, SSD_WIDTH:SSD_WIDTH + gs]
        cm = act[:, SSD_WIDTH + gs:SSD_WIDTH + 2 * gs]

        dt_c = dtc_ref[rows, :]
        dt_r = dtr_ref[:, chunk * L:(chunk + 1) * L]
        acs_c = _split_dot(tril, dt_c * a_row, 3, lhs_split=False)
        acs_r = _split_dot(dt_r * a_col, triu, 3)
        e_wide = _split_dot(jnp.exp(acs_c), spread, 2)

        ys = []
        for g in range(N_SSD_GROUPS):
            b_g = bm[:, g * SSD_STATE:(g + 1) * SSD_STATE]
            c_gb = cm[:, g * SSD_STATE:(g + 1) * SSD_STATE].astype(BF16)
            cb = _dot_nt(c_gb, b_g.astype(BF16))
            b_gt = b_g.T
            for q in range(pairs_per_group):
                p = g * pairs_per_group + q
                x_pair = xs_b[:, p * LANES:(p + 1) * LANES]
                zero = jnp.zeros_like(x_pair)
                x_split = jnp.concatenate([jnp.where(head0, x_pair, zero),
                                           jnp.where(head0, zero, x_pair)], axis=0)
                w_diag, w_state, last = [], [], []
                for h in (2 * p, 2 * p + 1):
                    acs_col = acs_c[:, h:h + 1]
                    acs_row = acs_r[h:h + 1, :]
                    dt_row = dt_r[h:h + 1, :]
                    seg = jnp.exp(jnp.where(causal, acs_col - acs_row, NEG_BIG))
                    w_diag.append((cb * seg * dt_row).astype(BF16))
                    acs_last = acs_row[:, L - 1:L]
                    w_state.append((b_gt * (jnp.exp(acs_last - acs_row) * dt_row)).astype(BF16))
                    last.append(jnp.exp(acs_last))
                y_pair = (_dot(jnp.concatenate(w_diag, axis=1), x_split)
                          + e_wide[:, p * LANES:(p + 1) * LANES]
                          * _dot(c_gb, states[p].astype(BF16)))
                states[p] = (jnp.where(head0_row, last[0], last[1]) * states[p]
                             + _dot(jnp.concatenate(w_state, axis=1), x_split))
                ys.append(y_pair)
        y = (jnp.concatenate(ys, axis=1) + dskip_ref[...] * xs) * gate_ref[rows, :]

        outs = []
        for g in range(N_SSD_GROUPS):
            yg = y[:, g * gw:(g + 1) * gw]
            outs.append(yg * lax.rsqrt(jnp.mean(yg * yg, axis=-1, keepdims=True) + EPS))
        y_ref[rows, :] = (jnp.concatenate(outs, axis=1) * nw_ref[...]).astype(y_ref.dtype)

    for p in range(n_pairs):
        state[p] = states[p]


def _ssd(act, gate, dtc, dtr, alog_row, alog_col, dskip, nw):
    b, s, _ = act.shape
    rows = SSD_CHUNKS_PER_STEP * CHUNK
    tok = lambda width: pl.BlockSpec((None, rows, width), lambda i, c: (i, c, 0))
    return pl.pallas_call(
        _ssd_kernel,
        grid=(b, s // rows),
        in_specs=[
            tok(CONV_CHANNELS), tok(SSD_WIDTH), tok(LANES),
            pl.BlockSpec((None, DT_ROWS, rows), lambda i, c: (i, 0, c)),
            _resident(alog_row.shape), _resident(alog_col.shape),
            _resident(dskip.shape), _resident(nw.shape),
        ],
        out_specs=tok(SSD_WIDTH),
        out_shape=jax.ShapeDtypeStruct((b, s, SSD_WIDTH), BF16),
        scratch_shapes=[pltpu.VMEM((SSD_WIDTH // LANES, SSD_STATE, LANES), F32)],
        compiler_params=pltpu.CompilerParams(
            dimension_semantics=("parallel", "arbitrary"), vmem_limit_bytes=VMEM_LIMIT),
        name="ssd",
    )(act, gate, dtc, dtr, alog_row, alog_col, dskip, nw)


def _out_mlp_kernel(x_ref, oa_ref, ys_ref, woa_ref, wos_ref, nw_ref, wup_ref, wdn_ref, o_ref):
    x1 = x_ref[...] + _dot(oa_ref[...], woa_ref[...]) + _dot(ys_ref[...], wos_ref[...])
    hm = x1 * lax.rsqrt(jnp.mean(x1 * x1, axis=-1, keepdims=True) + EPS) * nw_ref[...]
    up = _dot(hm.astype(BF16), wup_ref[...])
    act = jnp.square(jnp.maximum(up, 0.0)).astype(BF16)
    o_ref[...] = x1 + _dot(act, wdn_ref[...])


def _out_mlp(x, o_attn, y_ssd, wo_a, wo_s, mlp_norm_w, w_up, w_down, tm):
    t, d = x.shape
    tok = lambda width: pl.BlockSpec((tm, width), lambda i: (i, 0))
    return pl.pallas_call(
        _out_mlp_kernel,
        grid=(t // tm,),
        in_specs=[
            tok(d), tok(ATTN_WIDTH), tok(SSD_WIDTH),
            _resident(wo_a.shape), _resident(wo_s.shape), _resident((1, d)),
            _resident(w_up.shape), _resident(w_down.shape),
        ],
        out_specs=tok(d),
        out_shape=jax.ShapeDtypeStruct((t, d), F32),
        compiler_params=pltpu.CompilerParams(
            dimension_semantics=("parallel",), vmem_limit_bytes=VMEM_LIMIT),
        name="out_mlp",
    )(x, o_attn, y_ssd, wo_a, wo_s, mlp_norm_w, w_up, w_down)


def _rope_tables(s):
    half = HEAD_DIM // 2
    inv_freq = ROPE_THETA ** (-jnp.arange(half, dtype=F32) / half)
    lane_freq = jnp.tile(inv_freq, LANES // half)
    sign = jnp.tile(jnp.repeat(jnp.array([-1.0, 1.0], F32), half), LANES // HEAD_DIM)
    ang = jnp.arange(s, dtype=F32)[:, None] * lane_freq[None, :]
    return jnp.cos(ang), jnp.sin(ang) * sign[None, :]


def _pad_lanes(row):
    return jnp.pad(row, ((0, 0), (0, LANES - row.shape[1])))


def _layer(x, attn_norm_w, w_in, q_norm_w, k_norm_w, conv_w, conv_b, dt_bias, a_log, d_skip,
           ssd_norm_w, w_out, mlp_norm_w, w_up, w_down):
    b, s, d = x.shape
    w_t = w_in.T
    w_main = w_t[:MAIN_WIDTH].astype(BF16)
    w_dt_c = jnp.pad(w_t[MAIN_WIDTH:], ((0, LANES - N_SSD_HEADS), (0, 0))).astype(BF16)
    head_of = jnp.arange(ATTN_WIDTH, dtype=jnp.int32) // HEAD_DIM
    group_mean = jnp.where(head_of[:, None] == head_of[None, :], 1.0 / HEAD_DIM, 0.0).astype(BF16)
    cos, sin = _rope_tables(s)
    q, k, v, gate, act, dtc, dtr = _in_proj(
        x, attn_norm_w[None, :], w_main, w_dt_c, group_mean,
        jnp.tile(q_norm_w, N_ATTN_HEADS)[None, :], jnp.tile(k_norm_w, N_ATTN_HEADS)[None, :],
        cos, sin, conv_w, conv_b[None, :], _pad_lanes(dt_bias[None, :]), tm=PROJ_TOKENS)

    o_attn = _attention(q, k, v)

    col = lambda p: jnp.pad(p[:, None], ((0, DT_ROWS - N_SSD_HEADS), (0, 0)))
    y_ssd = _ssd(act, gate, dtc, dtr, _pad_lanes(a_log[None, :]), col(a_log),
                 jnp.repeat(d_skip, SSD_HEAD_DIM)[None, :], ssd_norm_w[None, :])

    out = _out_mlp(x.reshape(b * s, d), o_attn.reshape(b * s, ATTN_WIDTH),
                   y_ssd.reshape(b * s, SSD_WIDTH),
                   w_out[:ATTN_WIDTH].astype(BF16), w_out[ATTN_WIDTH:].astype(BF16),
                   mlp_norm_w[None, :], w_up.astype(BF16), w_down.astype(BF16), tm=MLP_TOKENS)
    return out.reshape(b, s, d)


def kernel(x, attn_norm_w, w_in, q_norm_w, k_norm_w, conv_w, conv_b, dt_bias, a_log, d_skip,
           ssd_norm_w, w_out, mlp_norm_w, w_up, w_down):
    for i in range(attn_norm_w.shape[0]):
        x = _layer(x, attn_norm_w[i], w_in[i], q_norm_w[i], k_norm_w[i], conv_w[i], conv_b[i],
                   dt_bias[i], a_log[i], d_skip[i], ssd_norm_w[i], w_out[i], mlp_norm_w[i],
                   w_up[i], w_down[i])
    return x
```

```python
import functools

import jax
import jax.numpy as jnp
from jax import lax
from jax.experimental import pallas as pl
from jax.experimental.pallas import tpu as pltpu

F32 = jnp.float32
BF16 = jnp.bfloat16

HEAD_DIM = 64
N_ATTN_HEADS = 8
ATTN_WIDTH = N_ATTN_HEADS * HEAD_DIM
ROPE_THETA = 10000.0
Q_BLOCK = 128
DILATIONS = (1, 4, 16)
FAR_DIL = 4
FAR_RATIO = 4
ATTN_LOOKAHEAD = 5
SSD_HEAD_DIM = 64
N_SSD_HEADS = 8
SSD_WIDTH = N_SSD_HEADS * SSD_HEAD_DIM
N_SSD_GROUPS = 2
HEADS_PER_GROUP = N_SSD_HEADS // N_SSD_GROUPS
SSD_STATE = 128
CONV_WIDTH = 4
CHUNK = 128
SSD_CHUNKS_PER_STEP = 16
CONV_CHANNELS = SSD_WIDTH + 2 * N_SSD_GROUPS * SSD_STATE
CONV_HISTORY = 16
EPS = 1e-6
NEG_BIG = -1e30
SCORE_SCALE = HEAD_DIM ** -0.5 * 1.4426950408889634

LANES = 128
DT_ROWS = 16
MAIN_WIDTH = 3 * ATTN_WIDTH + SSD_WIDTH + CONV_CHANNELS
VMEM_LIMIT = 56 * 1024 * 1024
PROJ_TOKENS = 1024
PROJ_SLAB = 512
PROJ_LOOKAHEAD = 1
MLP_TOKENS = 512


def _dot(a, b):
    return jnp.dot(a, b, preferred_element_type=F32)


def _dot_nt(a, b):
    return lax.dot_general(a, b, (((1,), (1,)), ((), ())), preferred_element_type=F32)


def _split_dot(a, b, parts, lhs_split=True):
    src = a if lhs_split else b
    acc = None
    rem = src
    for _ in range(parts):
        piece = rem.astype(BF16)
        rem = rem - piece.astype(F32)
        term = _dot(piece, b) if lhs_split else _dot(a, piece)
        acc = term if acc is None else acc + term
    return acc


def _resident(shape):
    zeros = (0,) * len(shape)
    return pl.BlockSpec(shape, lambda *_: zeros, pipeline_mode=pl.Buffered(1))


def _in_proj_kernel(x_ref, xprev_ref, nw_ref, w_ref, wdt_ref, gm_ref, qw_ref, kw_ref,
                    cos_ref, sin_ref, cw_ref, cb_ref, dtb_ref,
                    q_ref, k_ref, v_ref, gate_ref, act_ref, dtc_ref, dtr_ref, ubuf):
    tm = x_ref.shape[0]
    slab = PROJ_SLAB

    def normed(x):
        y = x * lax.rsqrt(jnp.mean(x * x, axis=-1, keepdims=True) + EPS) * nw_ref[...]
        return y.astype(BF16)

    lane = lax.broadcasted_iota(jnp.int32, (slab, ATTN_WIDTH), 1)
    first_half = (lane % HEAD_DIM) < (HEAD_DIM // 2)

    def head_norm_rope(t, w, rows):
        cos = jnp.concatenate([cos_ref[rows, :]] * (ATTN_WIDTH // LANES), axis=1)
        sin = jnp.concatenate([sin_ref[rows, :]] * (ATTN_WIDTH // LANES), axis=1)
        ms = _dot((t * t).astype(BF16), gm_ref[...])
        n = t * lax.rsqrt(ms + EPS) * w
        partner = jnp.where(first_half,
                            pltpu.roll(n, ATTN_WIDTH - HEAD_DIM // 2, 1),
                            pltpu.roll(n, HEAD_DIM // 2, 1))
        return n * cos + partner * sin

    o_k, o_v, o_z, o_xbc = ATTN_WIDTH, 2 * ATTN_WIDTH, 3 * ATTN_WIDTH, 3 * ATTN_WIDTH + SSD_WIDTH

    def stages(i):
        rows = pl.ds(i * slab, slab)
        hb = normed(x_ref[rows, :])

        def q_tail(q):
            q_ref[rows, :] = (head_norm_rope(q, qw_ref[...], rows) * SCORE_SCALE).astype(BF16)

        def k_tail(k):
            k_ref[rows, :] = head_norm_rope(k, kw_ref[...], rows).astype(BF16)

        def v_tail(v):
            v_ref[rows, :] = v.astype(BF16)

        def z_tail(z):
            gate_ref[rows, :] = _silu(z)

        def conv_tail(u):
            if i == 0:
                hist = _dot_nt(normed(xprev_ref[...]), w_ref[o_xbc:o_xbc + CONV_CHANNELS, :])
                ubuf[0:CONV_HISTORY, :] = jnp.where(pl.program_id(0) == 0, 0.0, hist)
            at = CONV_HISTORY + i * slab
            ubuf[at:at + slab, :] = u
            conv = cb_ref[...] + cw_ref[CONV_WIDTH - 1:CONV_WIDTH, :] * u
            for tap in range(CONV_WIDTH - 1):
                first = at - (CONV_WIDTH - 1 - tap)
                conv = conv + cw_ref[tap:tap + 1, :] * ubuf[first:first + slab, :]
            act_ref[rows, :] = _silu(conv)

        def dt_tail(raw):
            dtc = _softplus(raw + dtb_ref[...])
            dtc_ref[rows, :] = dtc
            dtr_ref[:, rows] = dtc.T[:DT_ROWS, :]

        proj = lambda lo, width: (lambda: _dot_nt(hb, w_ref[lo:lo + width, :]))
        return [(proj(0, ATTN_WIDTH), q_tail), (proj(o_k, ATTN_WIDTH), k_tail),
                (proj(o_v, ATTN_WIDTH), v_tail), (proj(o_z, SSD_WIDTH), z_tail),
                (proj(o_xbc, CONV_CHANNELS), conv_tail),
                (lambda: _dot_nt(hb, wdt_ref[...]), dt_tail)]

    per_slab = [stages(i) for i in range(tm // slab)]
    order = [stage for group in zip(*per_slab) for stage in group]
    queue = [matmul() for matmul, _ in order[:PROJ_LOOKAHEAD]]
    for n, (_, tail) in enumerate(order):
        if n + PROJ_LOOKAHEAD < len(order):
            queue.append(order[n + PROJ_LOOKAHEAD][0]())
        tail(queue.pop(0))


def _in_proj(x, attn_norm_w, w_main, w_dt, group_mean, qw, kw, cos, sin, conv_w, conv_b, dt_bias,
             tm):
    b, s, d = x.shape
    grid = (s // tm, b)
    tok = lambda width: pl.BlockSpec((None, tm, width), lambda j, i: (i, j, 0))
    prev = pl.BlockSpec((None, CONV_HISTORY, d),
                        lambda j, i: (i, jnp.maximum(j * (tm // CONV_HISTORY) - 1, 0), 0))
    out_shapes = (
        jax.ShapeDtypeStruct((b, s, ATTN_WIDTH), BF16),
        jax.ShapeDtypeStruct((b, s, ATTN_WIDTH), BF16),
        jax.ShapeDtypeStruct((b, s, ATTN_WIDTH), BF16),
        jax.ShapeDtypeStruct((b, s, SSD_WIDTH), F32),
        jax.ShapeDtypeStruct((b, s, CONV_CHANNELS), F32),
        jax.ShapeDtypeStruct((b, s, LANES), F32),
        jax.ShapeDtypeStruct((b, DT_ROWS, s), F32),
    )
    return pl.pallas_call(
        _in_proj_kernel,
        grid=grid,
        in_specs=[
            tok(d),
            prev,
            _resident((1, d)),
            _resident(w_main.shape),
            _resident(w_dt.shape),
            _resident(group_mean.shape),
            _resident((1, ATTN_WIDTH)),
            _resident((1, ATTN_WIDTH)),
            pl.BlockSpec((tm, LANES), lambda j, i: (j, 0)),
            pl.BlockSpec((tm, LANES), lambda j, i: (j, 0)),
            _resident(conv_w.shape), _resident(conv_b.shape), _resident(dt_bias.shape),
        ],
        out_specs=(
            tok(ATTN_WIDTH), tok(ATTN_WIDTH), tok(ATTN_WIDTH), tok(SSD_WIDTH),
            tok(CONV_CHANNELS), tok(LANES),
            pl.BlockSpec((None, DT_ROWS, tm), lambda j, i: (i, 0, j)),
        ),
        out_shape=out_shapes,
        scratch_shapes=[pltpu.VMEM((CONV_HISTORY + tm, CONV_CHANNELS), F32)],
        compiler_params=pltpu.CompilerParams(
            dimension_semantics=("parallel", "parallel"), vmem_limit_bytes=VMEM_LIMIT),
        name="in_proj",
    )(x, x, attn_norm_w, w_main, w_dt, group_mean, qw, kw, cos, sin, conv_w, conv_b, dt_bias)


def _attn_blocks(blocks, store):
    qb = Q_BLOCK
    n_pairs = ATTN_WIDTH // LANES
    head0 = lax.broadcasted_iota(jnp.int32, (qb, LANES), 1) < HEAD_DIM
    items = [(i, pair) for i in range(len(blocks)) for pair in range(n_pairs)]

    def scores(item):
        i, pair = item
        load_q, load_k, _, load_bias = blocks[i]
        qt, kb, bias = load_q(pair), load_k(pair), load_bias()
        out = []
        for h in range(2):
            mine = head0 if h == 0 else jnp.logical_not(head0)
            qh = jnp.where(mine, qt, jnp.zeros_like(qt))
            out.append(_dot_nt(qh, kb) + bias)
        return out

    queue = [scores(item) for item in items[:ATTN_LOOKAHEAD]]
    parts = []
    for n, (i, pair) in enumerate(items):
        if n + ATTN_LOOKAHEAD < len(items):
            queue.append(scores(items[n + ATTN_LOOKAHEAD]))
        pending = queue.pop(0)
        vb = blocks[i][2](pair)
        v_ext = jnp.concatenate([vb, jnp.ones_like(vb)], axis=1)
        heads = []
        for sc in pending:
            m = jnp.max(sc, axis=1, keepdims=True)
            r = _dot(jnp.exp2(sc - m).astype(BF16), v_ext)
            heads.append((r[:, :LANES], r[:, LANES:], jnp.broadcast_to(m, (qb, LANES))))
        parts.append([jnp.where(head0, a, b) for a, b in zip(*heads)])
        if pair == n_pairs - 1:
            store(i, *[jnp.concatenate(t, axis=1) for t in zip(*parts)])
            parts = []


def _near_bias(first):
    qb = Q_BLOCK
    row = lax.broadcasted_iota(jnp.int32, (qb, 2 * qb), 0)
    col = lax.broadcasted_iota(jnp.int32, (qb, 2 * qb), 1)
    ok = (col <= row) if first else ((col >= row) & (col <= row + qb))
    return jnp.where(ok, 0.0, NEG_BIG)


def _far_bias(blocks_back):
    qb = Q_BLOCK
    row = lax.broadcasted_iota(jnp.int32, (qb, qb), 0)
    col = lax.broadcasted_iota(jnp.int32, (qb, qb), 1)
    dist = row - col + blocks_back * qb
    in_window = (dist >= 0) & (dist <= qb)
    on_stride = (dist >= 0) & ((dist & (FAR_RATIO - 1)) == 0)
    return jnp.where(in_window & on_stride, 1.0,
                     jnp.where(in_window | on_stride, 0.0, NEG_BIG))


def _attn_kernel(q_ref, k_ref, v_ref, o_ref,
                 tmp, q_cls, k_cls, v_cls, num_far, den_far, top_far, near_bias, far_bias):
    qb = Q_BLOCK
    step = pl.program_id(1)
    rows = q_ref.shape[0]
    s = k_ref.shape[0]
    n_steps = s // rows
    n_pairs = ATTN_WIDTH // LANES
    lanes = lambda pair: pl.ds(pair * LANES, LANES)
    far = (num_far, den_far, top_far)

    def to_class_order(src_rows, dst, at):
        for p in range(n_pairs):
            tmp[p] = src_rows[:, p * LANES:(p + 1) * LANES].astype(F32)
            for c in range(FAR_DIL):
                dst[c, at:at + qb, lanes(p)] = tmp[p, pl.ds(c, qb, stride=FAR_DIL), :].astype(BF16)

    @pl.when(step == 0)
    def _():
        for i in range(n_steps):
            to_class_order(k_ref[i * rows:(i + 1) * rows, :], k_cls, i * qb)
            to_class_order(v_ref[i * rows:(i + 1) * rows, :], v_cls, i * qb)

    to_class_order(q_ref[...], q_cls, 0)
    near_bias[0] = _near_bias(False)
    near_bias[1] = _near_bias(True)
    for d in range(3):
        far_bias[d] = _far_bias(d)

    def far_store(c, *stats):
        for dst, t in zip(far, stats):
            for p in range(n_pairs):
                dst[p, pl.ds(c, qb, stride=FAR_DIL), :] = t[:, p * LANES:(p + 1) * LANES]

    for n_kb in range(1, n_steps + 1):
        @pl.when(step == n_kb - 1)
        def _(n_kb=n_kb):
            def block(c):
                load_bias = lambda: jnp.concatenate(
                    [far_bias[min(n_kb - 1 - kb, 2)] for kb in range(n_kb)], axis=1)
                return (lambda p: q_cls[c, :, lanes(p)],
                        lambda p: k_cls[c, 0:n_kb * qb, lanes(p)],
                        lambda p: v_cls[c, 0:n_kb * qb, lanes(p)], load_bias)
            _attn_blocks([block(c) for c in range(FAR_DIL)], far_store)

    is_first = (step == 0).astype(jnp.int32)

    def near_block(j):
        q_rows = pl.ds(j * qb, qb)
        if j == 0:
            start = pl.multiple_of((step * FAR_DIL - 1 + is_first) * qb, qb)
            load_bias = lambda: near_bias[is_first]
        else:
            start = pl.multiple_of((step * FAR_DIL + j - 1) * qb, qb)
            load_bias = lambda: near_bias[0]
        kv_rows = pl.ds(start, 2 * qb)
        return (lambda p: q_ref[q_rows, lanes(p)], lambda p: k_ref[kv_rows, lanes(p)],
                lambda p: v_ref[kv_rows, lanes(p)], load_bias)

    def near_store(j, num, den, top):
        q_rows = pl.ds(j * qb, qb)
        num_f, den_f, top_f = [
            jnp.concatenate([t[p, q_rows, :] for p in range(n_pairs)], axis=1) for t in far]
        both = jnp.maximum(top, top_f)
        w_near, w_far = jnp.exp2(top - both), jnp.exp2(top_f - both)
        o_ref[q_rows, :] = ((w_near * num + w_far * num_f)
                            / (w_near * den + w_far * den_f)).astype(o_ref.dtype)

    _attn_blocks([near_block(j) for j in range(rows // qb)], near_store)


def _attention(q, k, v):
    b, s, w = q.shape
    assert DILATIONS == (1, FAR_DIL, FAR_DIL * FAR_RATIO) and s == Q_BLOCK * DILATIONS[-1]
    rows = FAR_DIL * Q_BLOCK
    q_blk = pl.BlockSpec((None, rows, w), lambda i, g: (i, g, 0))
    kv_blk = pl.BlockSpec((None, s, w), lambda i, g: (i, 0, 0))
    return pl.pallas_call(
        _attn_kernel,
        grid=(b, s // rows),
        in_specs=[q_blk, kv_blk, kv_blk],
        out_specs=q_blk,
        out_shape=jax.ShapeDtypeStruct((b, s, w), BF16),
        scratch_shapes=[
            pltpu.VMEM((w // LANES, rows, LANES), F32),
            pltpu.VMEM((FAR_DIL, Q_BLOCK, w), BF16),
            pltpu.VMEM((FAR_DIL, s // FAR_DIL, w), BF16),
            pltpu.VMEM((FAR_DIL, s // FAR_DIL, w), BF16),
            pltpu.VMEM((w // LANES, rows, LANES), F32),
            pltpu.VMEM((w // LANES, rows, LANES), F32),
            pltpu.VMEM((w // LANES, rows, LANES), F32),
            pltpu.VMEM((2, Q_BLOCK, 2 * Q_BLOCK), F32),
            pltpu.VMEM((3, Q_BLOCK, Q_BLOCK), F32),
        ],
        compiler_params=pltpu.CompilerParams(
            dimension_semantics=("parallel", "arbitrary"), vmem_limit_bytes=VMEM_LIMIT),
        name="dilated_attn",
    )(q, k, v)


def _softplus(x):
    return jnp.maximum(x, 0.0) + jnp.log1p(jnp.exp(-jnp.abs(x)))


def _silu(x):
    return x * (1.0 / (1.0 + jnp.exp(-x)))


def _ssd_kernel(act_ref, gate_ref, dtc_ref, dtr_ref, alog_row_ref, alog_col_ref, dskip_ref,
                nw_ref, y_ref, state):
    L = CHUNK

    @pl.when(pl.program_id(1) == 0)
    def _():
        state[...] = jnp.zeros(state.shape, F32)

    a_row = -jnp.exp(alog_row_ref[...])
    a_col = -jnp.exp(alog_col_ref[...])
    ri = lax.broadcasted_iota(jnp.int32, (L, L), 0)
    ci = lax.broadcasted_iota(jnp.int32, (L, L), 1)
    causal = ci <= ri
    tril = jnp.where(causal, 1.0, 0.0).astype(BF16)
    triu = jnp.where(ri <= ci, 1.0, 0.0).astype(BF16)
    gs = N_SSD_GROUPS * SSD_STATE
    gw = SSD_WIDTH // N_SSD_GROUPS

    n_pairs = SSD_WIDTH // LANES
    pairs_per_group = n_pairs // N_SSD_GROUPS
    head0 = lax.broadcasted_iota(jnp.int32, (L, LANES), 1) < SSD_HEAD_DIM
    head0_row = head0[:1, :]
    spread = (lax.broadcasted_iota(jnp.int32, (LANES, SSD_WIDTH), 0)
              == lax.broadcasted_iota(jnp.int32, (LANES, SSD_WIDTH), 1) // SSD_HEAD_DIM)
    spread = jnp.where(spread, 1.0, 0.0).astype(BF16)

    states = [state[p] for p in range(n_pairs)]
    for chunk in range(act_ref.shape[0] // L):
        rows = pl.ds(chunk * L, L)
        act = act_ref[rows, :]
        xs = act[:, :SSD_WIDTH]
        xs_b = xs.astype(BF16)
        bm = act[:, SSD_WIDTH:SSD_WIDTH + gs]
        cm = act[:, SSD_WIDTH + gs:SSD_WIDTH + 2 * gs]

        dt_c = dtc_ref[rows, :]
        dt_r = dtr_ref[:, chunk * L:(chunk + 1) * L]
        acs_c = _split_dot(tril, dt_c * a_row, 3, lhs_split=False)
        acs_r = _split_dot(dt_r * a_col, triu, 3)
        e_wide = _split_dot(jnp.exp(acs_c), spread, 2)

        ys = []
        for g in range(N_SSD_GROUPS):
            b_g = bm[:, g * SSD_STATE:(g + 1) * SSD_STATE]
            c_gb = cm[:, g * SSD_STATE:(g + 1) * SSD_STATE].astype(BF16)
            cb = _dot_nt(c_gb, b_g.astype(BF16))
            b_gt = b_g.T
            for q in range(pairs_per_group):
                p = g * pairs_per_group + q
                x_pair = xs_b[:, p * LANES:(p + 1) * LANES]
                zero = jnp.zeros_like(x_pair)
                x_split = jnp.concatenate([jnp.where(head0, x_pair, zero),
                                           jnp.where(head0, zero, x_pair)], axis=0)
                w_diag, w_state, last = [], [], []
                for h in (2 * p, 2 * p + 1):
                    acs_col = acs_c[:, h:h + 1]
                    acs_row = acs_r[h:h + 1, :]
                    dt_row = dt_r[h:h + 1, :]
                    seg = jnp.exp(jnp.where(causal, acs_col - acs_row, NEG_BIG))
                    w_diag.append((cb * seg * dt_row).astype(BF16))
                    acs_last = acs_row[:, L - 1:L]
                    w_state.append((b_gt * (jnp.exp(acs_last - acs_row) * dt_row)).astype(BF16))
                    last.append(jnp.exp(acs_last))
                y_pair = (_dot(jnp.concatenate(w_diag, axis=1), x_split)
                          + e_wide[:, p * LANES:(p + 1) * LANES]
                          * _dot(c_gb, states[p].astype(BF16)))
                states[p] = (jnp.where(head0_row, last[0], last[1]) * states[p]
                             + _dot(jnp.concatenate(w_state, axis=1), x_split))
                ys.append(y_pair)
        y = (jnp.concatenate(ys, axis=1) + dskip_ref[...] * xs) * gate_ref[rows, :]

        outs = []
        for g in range(N_SSD_GROUPS):
            yg = y[:, g * gw:(g + 1) * gw]
            outs.append(yg * lax.rsqrt(jnp.mean(yg * yg, axis=-1, keepdims=True) + EPS))
        y_ref[rows, :] = (jnp.concatenate(outs, axis=1) * nw_ref[...]).astype(y_ref.dtype)

    for p in range(n_pairs):
        state[p] = states[p]


def _ssd(act, gate, dtc, dtr, alog_row, alog_col, dskip, nw):
    b, s, _ = act.shape
    rows = SSD_CHUNKS_PER_STEP * CHUNK
    tok = lambda width: pl.BlockSpec((None, rows, width), lambda i, c: (i, c, 0))
    return pl.pallas_call(
        _ssd_kernel,
        grid=(b, s // rows),
        in_specs=[
            tok(CONV_CHANNELS), tok(SSD_WIDTH), tok(LANES),
            pl.BlockSpec((None, DT_ROWS, rows), lambda i, c: (i, 0, c)),
            _resident(alog_row.shape), _resident(alog_col.shape),
            _resident(dskip.shape), _resident(nw.shape),
        ],
        out_specs=tok(SSD_WIDTH),
        out_shape=jax.ShapeDtypeStruct((b, s, SSD_WIDTH), BF16),
        scratch_shapes=[pltpu.VMEM((SSD_WIDTH // LANES, SSD_STATE, LANES), F32)],
        compiler_params=pltpu.CompilerParams(
            dimension_semantics=("parallel", "arbitrary"), vmem_limit_bytes=VMEM_LIMIT),
        name="ssd",
    )(act, gate, dtc, dtr, alog_row, alog_col, dskip, nw)


def _out_mlp_kernel(x_ref, oa_ref, ys_ref, woa_ref, wos_ref, nw_ref, wup_ref, wdn_ref, o_ref):
    x1 = x_ref[...] + _dot(oa_ref[...], woa_ref[...]) + _dot(ys_ref[...], wos_ref[...])
    hm = x1 * lax.rsqrt(jnp.mean(x1 * x1, axis=-1, keepdims=True) + EPS) * nw_ref[...]
    up = _dot(hm.astype(BF16), wup_ref[...])
    act = jnp.square(jnp.maximum(up, 0.0)).astype(BF16)
    o_ref[...] = x1 + _dot(act, wdn_ref[...])


def _out_mlp(x, o_attn, y_ssd, wo_a, wo_s, mlp_norm_w, w_up, w_down, tm):
    t, d = x.shape
    tok = lambda width: pl.BlockSpec((tm, width), lambda i: (i, 0))
    return pl.pallas_call(
        _out_mlp_kernel,
        grid=(t // tm,),
        in_specs=[
            tok(d), tok(ATTN_WIDTH), tok(SSD_WIDTH),
            _resident(wo_a.shape), _resident(wo_s.shape), _resident((1, d)),
            _resident(w_up.shape), _resident(w_down.shape),
        ],
        out_specs=tok(d),
        out_shape=jax.ShapeDtypeStruct((t, d), F32),
        compiler_params=pltpu.CompilerParams(
            dimension_semantics=("parallel",), vmem_limit_bytes=VMEM_LIMIT),
        name="out_mlp",
    )(x, o_attn, y_ssd, wo_a, wo_s, mlp_norm_w, w_up, w_down)


def _rope_tables(s):
    half = HEAD_DIM // 2
    inv_freq = ROPE_THETA ** (-jnp.arange(half, dtype=F32) / half)
    lane_freq = jnp.tile(inv_freq, LANES // half)
    sign = jnp.tile(jnp.repeat(jnp.array([-1.0, 1.0], F32), half), LANES // HEAD_DIM)
    ang = jnp.arange(s, dtype=F32)[:, None] * lane_freq[None, :]
    return jnp.cos(ang), jnp.sin(ang) * sign[None, :]


def _pad_lanes(row):
    return jnp.pad(row, ((0, 0), (0, LANES - row.shape[1])))


def _layer(x, attn_norm_w, w_in, q_norm_w, k_norm_w, conv_w, conv_b, dt_bias, a_log, d_skip,
           ssd_norm_w, w_out, mlp_norm_w, w_up, w_down):
    b, s, d = x.shape
    w_t = w_in.T
    w_main = w_t[:MAIN_WIDTH].astype(BF16)
    w_dt_c = jnp.pad(w_t[MAIN_WIDTH:], ((0, LANES - N_SSD_HEADS), (0, 0))).astype(BF16)
    head_of = jnp.arange(ATTN_WIDTH, dtype=jnp.int32) // HEAD_DIM
    group_mean = jnp.where(head_of[:, None] == head_of[None, :], 1.0 / HEAD_DIM, 0.0).astype(BF16)
    cos, sin = _rope_tables(s)
    q, k, v, gate, act, dtc, dtr = _in_proj(
        x, attn_norm_w[None, :], w_main, w_dt_c, group_mean,
        jnp.tile(q_norm_w, N_ATTN_HEADS)[None, :], jnp.tile(k_norm_w, N_ATTN_HEADS)[None, :],
        cos, sin, conv_w, conv_b[None, :], _pad_lanes(dt_bias[None, :]), tm=PROJ_TOKENS)

    o_attn = _attention(q, k, v)

    col = lambda p: jnp.pad(p[:, None], ((0, DT_ROWS - N_SSD_HEADS), (0, 0)))
    y_ssd = _ssd(act, gate, dtc, dtr, _pad_lanes(a_log[None, :]), col(a_log),
                 jnp.repeat(d_skip, SSD_HEAD_DIM)[None, :], ssd_norm_w[None, :])

    out = _out_mlp(x.reshape(b * s, d), o_attn.reshape(b * s, ATTN_WIDTH),
                   y_ssd.reshape(b * s, SSD_WIDTH),
                   w_out[:ATTN_WIDTH].astype(BF16), w_out[ATTN_WIDTH:].astype(BF16),
                   mlp_norm_w[None, :], w_up.astype(BF16), w_down.astype(BF16), tm=MLP_TOKENS)
    return out.reshape(b, s, d)


def kernel(x, attn_norm_w, w_in, q_norm_w, k_norm_w, conv_w, conv_b, dt_bias, a_log, d_skip,
           ssd_norm_w, w_out, mlp_norm_w, w_up, w_down):
    for i in range(attn_norm_w.shape[0]):
        x = _layer(x, attn_norm_w[i], w_in[i], q_norm_w[i], k_norm_w[i], conv_w[i], conv_b[i],
                   dt_bias[i], a_log[i], d_skip[i], ssd_norm_w[i], w_out[i], mlp_norm_w[i],
                   w_up[i], w_down[i])
    return x
```

```python
import functools

import jax
import jax.numpy as jnp
from jax import lax
from jax.experimental import pallas as pl
from jax.experimental.pallas import tpu as pltpu

F32 = jnp.float32
BF16 = jnp.bfloat16

HEAD_DIM = 64
N_ATTN_HEADS = 8
ATTN_WIDTH = N_ATTN_HEADS * HEAD_DIM
ROPE_THETA = 10000.0
Q_BLOCK = 128
DILATIONS = (1, 4, 16)
FAR_DIL = 4
FAR_RATIO = 4
ATTN_LOOKAHEAD = 5
SSD_HEAD_DIM = 64
N_SSD_HEADS = 8
SSD_WIDTH = N_SSD_HEADS * SSD_HEAD_DIM
N_SSD_GROUPS = 2
HEADS_PER_GROUP = N_SSD_HEADS // N_SSD_GROUPS
SSD_STATE = 128
CONV_WIDTH = 4
CHUNK = 128
SSD_CHUNKS_PER_STEP = 16
CONV_CHANNELS = SSD_WIDTH + 2 * N_SSD_GROUPS * SSD_STATE
CONV_HISTORY = 16
CONV_GROUP = 256
EPS = 1e-6
NEG_BIG = -1e30
SCORE_SCALE = HEAD_DIM ** -0.5 * 1.4426950408889634

LANES = 128
DT_ROWS = 16
MAIN_WIDTH = 3 * ATTN_WIDTH + SSD_WIDTH + CONV_CHANNELS
VMEM_LIMIT = 56 * 1024 * 1024
PROJ_TOKENS = 1024
MLP_TOKENS = 512


def _dot(a, b):
    return jnp.dot(a, b, preferred_element_type=F32)


def _dot_nt(a, b):
    return lax.dot_general(a, b, (((1,), (1,)), ((), ())), preferred_element_type=F32)


def _split_dot(a, b, parts, lhs_split=True):
    src = a if lhs_split else b
    acc = None
    rem = src
    for _ in range(parts):
        piece = rem.astype(BF16)
        rem = rem - piece.astype(F32)
        term = _dot(piece, b) if lhs_split else _dot(a, piece)
        acc = term if acc is None else acc + term
    return acc


def _resident(shape):
    zeros = (0,) * len(shape)
    return pl.BlockSpec(shape, lambda *_: zeros, pipeline_mode=pl.Buffered(1))


def _in_proj_kernel(x_ref, xprev_ref, nw_ref, w_ref, wdt_ref, gm_ref, qw_ref, kw_ref,
                    cos_ref, sin_ref, cw_ref, cb_ref, dtb_ref,
                    q_ref, k_ref, v_ref, gate_ref, act_ref, dtc_ref, dtr_ref, ubuf):
    tm = x_ref.shape[0]

    def normed(x):
        y = x * lax.rsqrt(jnp.mean(x * x, axis=-1, keepdims=True) + EPS) * nw_ref[...]
        return y.astype(BF16)

    hb = normed(x_ref[...])

    lane = lax.broadcasted_iota(jnp.int32, (tm, ATTN_WIDTH), 1)
    first_half = (lane % HEAD_DIM) < (HEAD_DIM // 2)
    cos = jnp.concatenate([cos_ref[...]] * (ATTN_WIDTH // LANES), axis=1)
    sin = jnp.concatenate([sin_ref[...]] * (ATTN_WIDTH // LANES), axis=1)

    def head_norm_rope(t, w):
        ms = _dot((t * t).astype(BF16), gm_ref[...])
        n = t * lax.rsqrt(ms + EPS) * w
        partner = jnp.where(first_half,
                            pltpu.roll(n, ATTN_WIDTH - HEAD_DIM // 2, 1),
                            pltpu.roll(n, HEAD_DIM // 2, 1))
        return n * cos + partner * sin

    o = 0
    q = _dot_nt(hb, w_ref[o:o + ATTN_WIDTH, :]); o += ATTN_WIDTH
    q_ref[...] = (head_norm_rope(q, qw_ref[...]) * SCORE_SCALE).astype(BF16)
    k = _dot_nt(hb, w_ref[o:o + ATTN_WIDTH, :]); o += ATTN_WIDTH
    k_ref[...] = head_norm_rope(k, kw_ref[...]).astype(BF16)
    v_ref[...] = _dot_nt(hb, w_ref[o:o + ATTN_WIDTH, :]).astype(BF16); o += ATTN_WIDTH
    gate_ref[...] = _silu(_dot_nt(hb, w_ref[o:o + SSD_WIDTH, :])); o += SSD_WIDTH

    hb_prev = normed(xprev_ref[...])
    for c in range(CONV_CHANNELS // CONV_GROUP):
        lanes = pl.ds(c * CONV_GROUP, CONV_GROUP)
        w_c = w_ref[o + c * CONV_GROUP:o + (c + 1) * CONV_GROUP, :]
        u = _dot_nt(hb, w_c)
        hist = _dot_nt(hb_prev, w_c)
        ubuf[0:CONV_HISTORY, lanes] = jnp.where(pl.program_id(0) == 0, 0.0, hist)
        ubuf[CONV_HISTORY:CONV_HISTORY + tm, lanes] = u
        conv = cb_ref[:, lanes] + cw_ref[CONV_WIDTH - 1:CONV_WIDTH, lanes] * u
        for tap in range(CONV_WIDTH - 1):
            first = CONV_HISTORY - (CONV_WIDTH - 1 - tap)
            conv = conv + cw_ref[tap:tap + 1, lanes] * ubuf[first:first + tm, lanes]
        act_ref[:, lanes] = _silu(conv)

    dtc = _softplus(_dot_nt(hb, wdt_ref[...]) + dtb_ref[...])
    dtc_ref[...] = dtc
    dtr_ref[...] = dtc.T[:DT_ROWS, :]


def _in_proj(x, attn_norm_w, w_main, w_dt, group_mean, qw, kw, cos, sin, conv_w, conv_b, dt_bias,
             tm):
    b, s, d = x.shape
    grid = (s // tm, b)
    tok = lambda width: pl.BlockSpec((None, tm, width), lambda j, i: (i, j, 0))
    prev = pl.BlockSpec((None, CONV_HISTORY, d),
                        lambda j, i: (i, jnp.maximum(j * (tm // CONV_HISTORY) - 1, 0), 0))
    out_shapes = (
        jax.ShapeDtypeStruct((b, s, ATTN_WIDTH), BF16),
        jax.ShapeDtypeStruct((b, s, ATTN_WIDTH), BF16),
        jax.ShapeDtypeStruct((b, s, ATTN_WIDTH), BF16),
        jax.ShapeDtypeStruct((b, s, SSD_WIDTH), F32),
        jax.ShapeDtypeStruct((b, s, CONV_CHANNELS), F32),
        jax.ShapeDtypeStruct((b, s, LANES), F32),
        jax.ShapeDtypeStruct((b, DT_ROWS, s), F32),
    )
    return pl.pallas_call(
        _in_proj_kernel,
        grid=grid,
        in_specs=[
            tok(d),
            prev,
            _resident((1, d)),
            _resident(w_main.shape),
            _resident(w_dt.shape),
            _resident(group_mean.shape),
            _resident((1, ATTN_WIDTH)),
            _resident((1, ATTN_WIDTH)),
            pl.BlockSpec((tm, LANES), lambda j, i: (j, 0)),
            pl.BlockSpec((tm, LANES), lambda j, i: (j, 0)),
            _resident(conv_w.shape), _resident(conv_b.shape), _resident(dt_bias.shape),
        ],
        out_specs=(
            tok(ATTN_WIDTH), tok(ATTN_WIDTH), tok(ATTN_WIDTH), tok(SSD_WIDTH),
            tok(CONV_CHANNELS), tok(LANES),
            pl.BlockSpec((None, DT_ROWS, tm), lambda j, i: (i, 0, j)),
        ),
        out_shape=out_shapes,
        scratch_shapes=[pltpu.VMEM((CONV_HISTORY + tm, CONV_CHANNELS), F32)],
        compiler_params=pltpu.CompilerParams(
            dimension_semantics=("parallel", "parallel"), vmem_limit_bytes=VMEM_LIMIT),
        name="in_proj",
    )(x, x, attn_norm_w, w_main, w_dt, group_mean, qw, kw, cos, sin, conv_w, conv_b, dt_bias)


def _attn_blocks(blocks, store):
    qb = Q_BLOCK
    n_pairs = ATTN_WIDTH // LANES
    head0 = lax.broadcasted_iota(jnp.int32, (qb, LANES), 1) < HEAD_DIM
    items = [(i, pair) for i in range(len(blocks)) for pair in range(n_pairs)]

    def scores(item):
        i, pair = item
        load_q, load_k, _, load_bias = blocks[i]
        qt, kb, bias = load_q(pair), load_k(pair), load_bias()
        out = []
        for h in range(2):
            mine = head0 if h == 0 else jnp.logical_not(head0)
            qh = jnp.where(mine, qt, jnp.zeros_like(qt))
            out.append(_dot_nt(qh, kb) + bias)
        return out

    queue = [scores(item) for item in items[:ATTN_LOOKAHEAD]]
    parts = []
    for n, (i, pair) in enumerate(items):
        if n + ATTN_LOOKAHEAD < len(items):
            queue.append(scores(items[n + ATTN_LOOKAHEAD]))
        pending = queue.pop(0)
        vb = blocks[i][2](pair)
        v_ext = jnp.concatenate([vb, jnp.ones_like(vb)], axis=1)
        heads = []
        for sc in pending:
            m = jnp.max(sc, axis=1, keepdims=True)
            r = _dot(jnp.exp2(sc - m).astype(BF16), v_ext)
            heads.append((r[:, :LANES], r[:, LANES:], jnp.broadcast_to(m, (qb, LANES))))
        parts.append([jnp.where(head0, a, b) for a, b in zip(*heads)])
        if pair == n_pairs - 1:
            store(i, *[jnp.concatenate(t, axis=1) for t in zip(*parts)])
            parts = []


def _near_bias(first):
    qb = Q_BLOCK
    row = lax.broadcasted_iota(jnp.int32, (qb, 2 * qb), 0)
    col = lax.broadcasted_iota(jnp.int32, (qb, 2 * qb), 1)
    ok = (col <= row) if first else ((col >= row) & (col <= row + qb))
    return jnp.where(ok, 0.0, NEG_BIG)


def _far_bias(blocks_back):
    qb = Q_BLOCK
    row = lax.broadcasted_iota(jnp.int32, (qb, qb), 0)
    col = lax.broadcasted_iota(jnp.int32, (qb, qb), 1)
    dist = row - col + blocks_back * qb
    in_window = (dist >= 0) & (dist <= qb)
    on_stride = (dist >= 0) & ((dist & (FAR_RATIO - 1)) == 0)
    return jnp.where(in_window & on_stride, 1.0,
                     jnp.where(in_window | on_stride, 0.0, NEG_BIG))


def _attn_kernel(q_ref, k_ref, v_ref, o_ref,
                 tmp, q_cls, k_cls, v_cls, num_far, den_far, top_far, near_bias, far_bias):
    qb = Q_BLOCK
    step = pl.program_id(1)
    rows = q_ref.shape[0]
    s = k_ref.shape[0]
    n_steps = s // rows
    n_pairs = ATTN_WIDTH // LANES
    lanes = lambda pair: pl.ds(pair * LANES, LANES)
    far = (num_far, den_far, top_far)

    def to_class_order(src_rows, dst, at):
        for p in range(n_pairs):
            tmp[p] = src_rows[:, p * LANES:(p + 1) * LANES].astype(F32)
            for c in range(FAR_DIL):
                dst[c, at:at + qb, lanes(p)] = tmp[p, pl.ds(c, qb, stride=FAR_DIL), :].astype(BF16)

    @pl.when(step == 0)
    def _():
        for i in range(n_steps):
            to_class_order(k_ref[i * rows:(i + 1) * rows, :], k_cls, i * qb)
            to_class_order(v_ref[i * rows:(i + 1) * rows, :], v_cls, i * qb)

    to_class_order(q_ref[...], q_cls, 0)
    near_bias[0] = _near_bias(False)
    near_bias[1] = _near_bias(True)
    for d in range(3):
        far_bias[d] = _far_bias(d)

    def far_store(c, *stats):
        for dst, t in zip(far, stats):
            for p in range(n_pairs):
                dst[p, pl.ds(c, qb, stride=FAR_DIL), :] = t[:, p * LANES:(p + 1) * LANES]

    for n_kb in range(1, n_steps + 1):
        @pl.when(step == n_kb - 1)
        def _(n_kb=n_kb):
            def block(c):
                load_bias = lambda: jnp.concatenate(
                    [far_bias[min(n_kb - 1 - kb, 2)] for kb in range(n_kb)], axis=1)
                return (lambda p: q_cls[c, :, lanes(p)],
                        lambda p: k_cls[c, 0:n_kb * qb, lanes(p)],
                        lambda p: v_cls[c, 0:n_kb * qb, lanes(p)], load_bias)
            _attn_blocks([block(c) for c in range(FAR_DIL)], far_store)

    is_first = (step == 0).astype(jnp.int32)

    def near_block(j):
        q_rows = pl.ds(j * qb, qb)
        if j == 0:
            start = pl.multiple_of((step * FAR_DIL - 1 + is_first) * qb, qb)
            load_bias = lambda: near_bias[is_first]
        else:
            start = pl.multiple_of((step * FAR_DIL + j - 1) * qb, qb)
            load_bias = lambda: near_bias[0]
        kv_rows = pl.ds(start, 2 * qb)
        return (lambda p: q_ref[q_rows, lanes(p)], lambda p: k_ref[kv_rows, lanes(p)],
                lambda p: v_ref[kv_rows, lanes(p)], load_bias)

    def near_store(j, num, den, top):
        q_rows = pl.ds(j * qb, qb)
        num_f, den_f, top_f = [
            jnp.concatenate([t[p, q_rows, :] for p in range(n_pairs)], axis=1) for t in far]
        both = jnp.maximum(top, top_f)
        w_near, w_far = jnp.exp2(top - both), jnp.exp2(top_f - both)
        o_ref[q_rows, :] = ((w_near * num + w_far * num_f)
                            / (w_near * den + w_far * den_f)).astype(o_ref.dtype)

    _attn_blocks([near_block(j) for j in range(rows // qb)], near_store)


def _attention(q, k, v):
    b, s, w = q.shape
    assert DILATIONS == (1, FAR_DIL, FAR_DIL * FAR_RATIO) and s == Q_BLOCK * DILATIONS[-1]
    rows = FAR_DIL * Q_BLOCK
    q_blk = pl.BlockSpec((None, rows, w), lambda i, g: (i, g, 0))
    kv_blk = pl.BlockSpec((None, s, w), lambda i, g: (i, 0, 0))
    return pl.pallas_call(
        _attn_kernel,
        grid=(b, s // rows),
        in_specs=[q_blk, kv_blk, kv_blk],
        out_specs=q_blk,
        out_shape=jax.ShapeDtypeStruct((b, s, w), BF16),
        scratch_shapes=[
            pltpu.VMEM((w // LANES, rows, LANES), F32),
            pltpu.VMEM((FAR_DIL, Q_BLOCK, w), BF16),
            pltpu.VMEM((FAR_DIL, s // FAR_DIL, w), BF16),
            pltpu.VMEM((FAR_DIL, s // FAR_DIL, w), BF16),
            pltpu.VMEM((w // LANES, rows, LANES), F32),
            pltpu.VMEM((w // LANES, rows, LANES), F32),
            pltpu.VMEM((w // LANES, rows, LANES), F32),
            pltpu.VMEM((2, Q_BLOCK, 2 * Q_BLOCK), F32),
            pltpu.VMEM((3, Q_BLOCK, Q_BLOCK), F32),
        ],
        compiler_params=pltpu.CompilerParams(
            dimension_semantics=("parallel", "arbitrary"), vmem_limit_bytes=VMEM_LIMIT),
        name="dilated_attn",
    )(q, k, v)


def _softplus(x):
    return jnp.maximum(x, 0.0) + jnp.log1p(jnp.exp(-jnp.abs(x)))


def _silu(x):
    return x * (1.0 / (1.0 + jnp.exp(-x)))


def _ssd_kernel(act_ref, gate_ref, dtc_ref, dtr_ref, alog_row_ref, alog_col_ref, dskip_ref,
                nw_ref, y_ref, state):
    L = CHUNK

    @pl.when(pl.program_id(1) == 0)
    def _():
        state[...] = jnp.zeros(state.shape, F32)

    a_row = -jnp.exp(alog_row_ref[...])
    a_col = -jnp.exp(alog_col_ref[...])
    ri = lax.broadcasted_iota(jnp.int32, (L, L), 0)
    ci = lax.broadcasted_iota(jnp.int32, (L, L), 1)
    causal = ci <= ri
    tril = jnp.where(causal, 1.0, 0.0).astype(BF16)
    triu = jnp.where(ri <= ci, 1.0, 0.0).astype(BF16)
    gs = N_SSD_GROUPS * SSD_STATE
    gw = SSD_WIDTH // N_SSD_GROUPS

    n_pairs = SSD_WIDTH // LANES
    pairs_per_group = n_pairs // N_SSD_GROUPS
    head0 = lax.broadcasted_iota(jnp.int32, (L, LANES), 1) < SSD_HEAD_DIM
    head0_row = head0[:1, :]
    spread = (lax.broadcasted_iota(jnp.int32, (LANES, SSD_WIDTH), 0)
              == lax.broadcasted_iota(jnp.int32, (LANES, SSD_WIDTH), 1) // SSD_HEAD_DIM)
    spread = jnp.where(spread, 1.0, 0.0).astype(BF16)

    states = [state[p] for p in range(n_pairs)]
    for chunk in range(act_ref.shape[0] // L):
        rows = pl.ds(chunk * L, L)
        act = act_ref[rows, :]
        xs = act[:, :SSD_WIDTH]
        xs_b = xs.astype(BF16)
        bm = act[:, SSD_WIDTH:SSD_WIDTH + gs]
        cm = act[:, SSD_WIDTH + gs:SSD_WIDTH + 2 * gs]

        dt_c = dtc_ref[rows, :]
        dt_r = dtr_ref[:, chunk * L:(chunk + 1) * L]
        acs_c = _split_dot(tril, dt_c * a_row, 3, lhs_split=False)
        acs_r = _split_dot(dt_r * a_col, triu, 3)
        e_wide = _split_dot(jnp.exp(acs_c), spread, 2)

        ys = []
        for g in range(N_SSD_GROUPS):
            b_g = bm[:, g * SSD_STATE:(g + 1) * SSD_STATE]
            c_gb = cm[:, g * SSD_STATE:(g + 1) * SSD_STATE].astype(BF16)
            cb = _dot_nt(c_gb, b_g.astype(BF16))
            b_gt = b_g.T
            for q in range(pairs_per_group):
                p = g * pairs_per_group + q
                x_pair = xs_b[:, p * LANES:(p + 1) * LANES]
                zero = jnp.zeros_like(x_pair)
                x_split = jnp.concatenate([jnp.where(head0, x_pair, zero),
                                           jnp.where(head0, zero, x_pair)], axis=0)
                w_diag, w_state, last = [], [], []
                for h in (2 * p, 2 * p + 1):
                    acs_col = acs_c[:, h:h + 1]
                    acs_row = acs_r[h:h + 1, :]
                    dt_row = dt_r[h:h + 1, :]
                    seg = jnp.exp(jnp.where(causal, acs_col - acs_row, NEG_BIG))
                    w_diag.append((cb * seg * dt_row).astype(BF16))
                    acs_last = acs_row[:, L - 1:L]
                    w_state.append((b_gt * (jnp.exp(acs_last - acs_row) * dt_row)).astype(BF16))
                    last.append(jnp.exp(acs_last))
                y_pair = (_dot(jnp.concatenate(w_diag, axis=1), x_split)
                          + e_wide[:, p * LANES:(p + 1) * LANES]
                          * _dot(c_gb, states[p].astype(BF16)))
                states[p] = (jnp.where(head0_row, last[0], last[1]) * states[p]
                             + _dot(jnp.concatenate(w_state, axis=1), x_split))
                ys.append(y_pair)
        y = (jnp.concatenate(ys, axis=1) + dskip_ref[...] * xs) * gate_ref[rows, :]

        outs = []
        for g in range(N_SSD_GROUPS):
            yg = y[:, g * gw:(g + 1) * gw]
            outs.append(yg * lax.rsqrt(jnp.mean(yg * yg, axis=-1, keepdims=True) + EPS))
        y_ref[rows, :] = (jnp.concatenate(outs, axis=1) * nw_ref[...]).astype(y_ref.dtype)

    for p in range(n_pairs):
        state[p] = states[p]


def _ssd(act, gate, dtc, dtr, alog_row, alog_col, dskip, nw):
    b, s, _ = act.shape
    rows = SSD_CHUNKS_PER_STEP * CHUNK
    tok = lambda width: pl.BlockSpec((None, rows, width), lambda i, c: (i, c, 0))
    return pl.pallas_call(
        _ssd_kernel,
        grid=(b, s // rows),
        in_specs=[
            tok(CONV_CHANNELS), tok(SSD_WIDTH), tok(LANES),
            pl.BlockSpec((None, DT_ROWS, rows), lambda i, c: (i, 0, c)),
            _resident(alog_row.shape), _resident(alog_col.shape),
            _resident(dskip.shape), _resident(nw.shape),
        ],
        out_specs=tok(SSD_WIDTH),
        out_shape=jax.ShapeDtypeStruct((b, s, SSD_WIDTH), BF16),
        scratch_shapes=[pltpu.VMEM((SSD_WIDTH // LANES, SSD_STATE, LANES), F32)],
        compiler_params=pltpu.CompilerParams(
            dimension_semantics=("parallel", "arbitrary"), vmem_limit_bytes=VMEM_LIMIT),
        name="ssd",
    )(act, gate, dtc, dtr, alog_row, alog_col, dskip, nw)


def _out_mlp_kernel(x_ref, oa_ref, ys_ref, woa_ref, wos_ref, nw_ref, wup_ref, wdn_ref, o_ref):
    x1 = x_ref[...] + _dot(oa_ref[...], woa_ref[...]) + _dot(ys_ref[...], wos_ref[...])
    hm = x1 * lax.rsqrt(jnp.mean(x1 * x1, axis=-1, keepdims=True) + EPS) * nw_ref[...]
    up = _dot(hm.astype(BF16), wup_ref[...])
    act = jnp.square(jnp.maximum(up, 0.0)).astype(BF16)
    o_ref[...] = x1 + _dot(act, wdn_ref[...])


def _out_mlp(x, o_attn, y_ssd, wo_a, wo_s, mlp_norm_w, w_up, w_down, tm):
    t, d = x.shape
    tok = lambda width: pl.BlockSpec((tm, width), lambda i: (i, 0))
    return pl.pallas_call(
        _out_mlp_kernel,
        grid=(t // tm,),
        in_specs=[
            tok(d), tok(ATTN_WIDTH), tok(SSD_WIDTH),
            _resident(wo_a.shape), _resident(wo_s.shape), _resident((1, d)),
            _resident(w_up.shape), _resident(w_down.shape),
        ],
        out_specs=tok(d),
        out_shape=jax.ShapeDtypeStruct((t, d), F32),
        compiler_params=pltpu.CompilerParams(
            dimension_semantics=("parallel",), vmem_limit_bytes=VMEM_LIMIT),
        name="out_mlp",
    )(x, o_attn, y_ssd, wo_a, wo_s, mlp_norm_w, w_up, w_down)


def _rope_tables(s):
    half = HEAD_DIM // 2
    inv_freq = ROPE_THETA ** (-jnp.arange(half, dtype=F32) / half)
    lane_freq = jnp.tile(inv_freq, LANES // half)
    sign = jnp.tile(jnp.repeat(jnp.array([-1.0, 1.0], F32), half), LANES // HEAD_DIM)
    ang = jnp.arange(s, dtype=F32)[:, None] * lane_freq[None, :]
    return jnp.cos(ang), jnp.sin(ang) * sign[None, :]


def _pad_lanes(row):
    return jnp.pad(row, ((0, 0), (0, LANES - row.shape[1])))


def _layer(x, attn_norm_w, w_in, q_norm_w, k_norm_w, conv_w, conv_b, dt_bias, a_log, d_skip,
           ssd_norm_w, w_out, mlp_norm_w, w_up, w_down):
    b, s, d = x.shape
    w_t = w_in.T
    w_main = w_t[:MAIN_WIDTH].astype(BF16)
    w_dt_c = jnp.pad(w_t[MAIN_WIDTH:], ((0, LANES - N_SSD_HEADS), (0, 0))).astype(BF16)
    head_of = jnp.arange(ATTN_WIDTH, dtype=jnp.int32) // HEAD_DIM
    group_mean = jnp.where(head_of[:, None] == head_of[None, :], 1.0 / HEAD_DIM, 0.0).astype(BF16)
    cos, sin = _rope_tables(s)
    q, k, v, gate, act, dtc, dtr = _in_proj(
        x, attn_norm_w[None, :], w_main, w_dt_c, group_mean,
        jnp.tile(q_norm_w, N_ATTN_HEADS)[None, :], jnp.tile(k_norm_w, N_ATTN_HEADS)[None, :],
        cos, sin, conv_w, conv_b[None, :], _pad_lanes(dt_bias[None, :]), tm=PROJ_TOKENS)

    o_attn = _attention(q, k, v)

    col = lambda p: jnp.pad(p[:, None], ((0, DT_ROWS - N_SSD_HEADS), (0, 0)))
    y_ssd = _ssd(act, gate, dtc, dtr, _pad_lanes(a_log[None, :]), col(a_log),
                 jnp.repeat(d_skip, SSD_HEAD_DIM)[None, :], ssd_norm_w[None, :])

    out = _out_mlp(x.reshape(b * s, d), o_attn.reshape(b * s, ATTN_WIDTH),
                   y_ssd.reshape(b * s, SSD_WIDTH),
                   w_out[:ATTN_WIDTH].astype(BF16), w_out[ATTN_WIDTH:].astype(BF16),
                   mlp_norm_w[None, :], w_up.astype(BF16), w_down.astype(BF16), tm=MLP_TOKENS)
    return out.reshape(b, s, d)


def kernel(x, attn_norm_w, w_in, q_norm_w, k_norm_w, conv_w, conv_b, dt_bias, a_log, d_skip,
           ssd_norm_w, w_out, mlp_norm_w, w_up, w_down):
    for i in range(attn_norm_w.shape[0]):
        x = _layer(x, attn_norm_w[i], w_in[i], q_norm_w[i], k_norm_w[i], conv_w[i], conv_b[i],
                   dt_bias[i], a_log[i], d_skip[i], ssd_norm_w[i], w_out[i], mlp_norm_w[i],
                   w_up[i], w_down[i])
    return x
```

```python
import functools

import jax
import jax.numpy as jnp
from jax import lax
from jax.experimental import pallas as pl
from jax.experimental.pallas import tpu as pltpu

F32 = jnp.float32
BF16 = jnp.bfloat16

HEAD_DIM = 64
N_ATTN_HEADS = 8
ATTN_WIDTH = N_ATTN_HEADS * HEAD_DIM
ROPE_THETA = 10000.0
Q_BLOCK = 128
DILATIONS = (1, 4, 16)
FAR_DIL = 4
FAR_RATIO = 4
ATTN_LOOKAHEAD = 5
SSD_HEAD_DIM = 64
N_SSD_HEADS = 8
SSD_WIDTH = N_SSD_HEADS * SSD_HEAD_DIM
N_SSD_GROUPS = 2
HEADS_PER_GROUP = N_SSD_HEADS // N_SSD_GROUPS
SSD_STATE = 128
CONV_WIDTH = 4
CHUNK = 128
SSD_CHUNKS_PER_STEP = 16
CONV_CHANNELS = SSD_WIDTH + 2 * N_SSD_GROUPS * SSD_STATE
CONV_HISTORY = 16
ROPE_ROWS = 512
CONV_GROUP = 256
EPS = 1e-6
NEG_BIG = -1e30
SCORE_SCALE = HEAD_DIM ** -0.5 * 1.4426950408889634

LANES = 128
DT_ROWS = 16
MAIN_WIDTH = 3 * ATTN_WIDTH + SSD_WIDTH + CONV_CHANNELS
VMEM_LIMIT = 56 * 1024 * 1024
PROJ_TOKENS = 1024
MLP_TOKENS = 512


def _dot(a, b):
    return jnp.dot(a, b, preferred_element_type=F32)


def _dot_nt(a, b):
    return lax.dot_general(a, b, (((1,), (1,)), ((), ())), preferred_element_type=F32)


def _split_dot(a, b, parts, lhs_split=True):
    src = a if lhs_split else b
    acc = None
    rem = src
    for _ in range(parts):
        piece = rem.astype(BF16)
        rem = rem - piece.astype(F32)
        term = _dot(piece, b) if lhs_split else _dot(a, piece)
        acc = term if acc is None else acc + term
    return acc


def _resident(shape):
    zeros = (0,) * len(shape)
    return pl.BlockSpec(shape, lambda *_: zeros, pipeline_mode=pl.Buffered(1))


def _in_proj_kernel(x_ref, xprev_ref, nw_ref, w_ref, wdt_ref, gm_ref, qw_ref, kw_ref,
                    cos_ref, sin_ref, cw_ref, cb_ref, dtb_ref,
                    q_ref, k_ref, v_ref, gate_ref, act_ref, dtc_ref, dtr_ref, ubuf):
    tm = x_ref.shape[0]

    def normed(x):
        y = x * lax.rsqrt(jnp.mean(x * x, axis=-1, keepdims=True) + EPS) * nw_ref[...]
        return y.astype(BF16)

    hb = normed(x_ref[...])

    lane = lax.broadcasted_iota(jnp.int32, (ROPE_ROWS, ATTN_WIDTH), 1)
    first_half = (lane % HEAD_DIM) < (HEAD_DIM // 2)

    def head_norm_rope(t, w, out_ref, scale):
        for r in range(tm // ROPE_ROWS):
            rows = pl.ds(r * ROPE_ROWS, ROPE_ROWS)
            tr = t[r * ROPE_ROWS:(r + 1) * ROPE_ROWS, :]
            cos = jnp.concatenate([cos_ref[rows, :]] * (ATTN_WIDTH // LANES), axis=1)
            sin = jnp.concatenate([sin_ref[rows, :]] * (ATTN_WIDTH // LANES), axis=1)
            ms = _dot((tr * tr).astype(BF16), gm_ref[...])
            n = tr * lax.rsqrt(ms + EPS) * w
            partner = jnp.where(first_half,
                                pltpu.roll(n, ATTN_WIDTH - HEAD_DIM // 2, 1),
                                pltpu.roll(n, HEAD_DIM // 2, 1))
            out_ref[rows, :] = ((n * cos + partner * sin) * scale).astype(BF16)

    o = 0
    q = _dot_nt(hb, w_ref[o:o + ATTN_WIDTH, :]); o += ATTN_WIDTH
    head_norm_rope(q, qw_ref[...], q_ref, SCORE_SCALE)
    k = _dot_nt(hb, w_ref[o:o + ATTN_WIDTH, :]); o += ATTN_WIDTH
    head_norm_rope(k, kw_ref[...], k_ref, 1.0)
    v_ref[...] = _dot_nt(hb, w_ref[o:o + ATTN_WIDTH, :]).astype(BF16); o += ATTN_WIDTH
    gate_ref[...] = _silu(_dot_nt(hb, w_ref[o:o + SSD_WIDTH, :])); o += SSD_WIDTH

    hb_prev = normed(xprev_ref[...])
    for c in range(CONV_CHANNELS // CONV_GROUP):
        lanes = pl.ds(c * CONV_GROUP, CONV_GROUP)
        w_c = w_ref[o + c * CONV_GROUP:o + (c + 1) * CONV_GROUP, :]
        u = _dot_nt(hb, w_c)
        hist = _dot_nt(hb_prev, w_c)
        ubuf[0:CONV_HISTORY, lanes] = jnp.where(pl.program_id(0) == 0, 0.0, hist)
        ubuf[CONV_HISTORY:CONV_HISTORY + tm, lanes] = u
        conv = cb_ref[:, lanes] + cw_ref[CONV_WIDTH - 1:CONV_WIDTH, lanes] * u
        for tap in range(CONV_WIDTH - 1):
            first = CONV_HISTORY - (CONV_WIDTH - 1 - tap)
            conv = conv + cw_ref[tap:tap + 1, lanes] * ubuf[first:first + tm, lanes]
        act_ref[:, lanes] = _silu(conv)

    dtc = _softplus(_dot_nt(hb, wdt_ref[...]) + dtb_ref[...])
    dtc_ref[...] = dtc
    dtr_ref[...] = dtc.T[:DT_ROWS, :]


def _in_proj(x, attn_norm_w, w_main, w_dt, group_mean, qw, kw, cos, sin, conv_w, conv_b, dt_bias,
             tm):
    b, s, d = x.shape
    grid = (s // tm, b)
    tok = lambda width: pl.BlockSpec((None, tm, width), lambda j, i: (i, j, 0))
    prev = pl.BlockSpec((None, CONV_HISTORY, d),
                        lambda j, i: (i, jnp.maximum(j * (tm // CONV_HISTORY) - 1, 0), 0))
    out_shapes = (
        jax.ShapeDtypeStruct((b, s, ATTN_WIDTH), BF16),
        jax.ShapeDtypeStruct((b, s, ATTN_WIDTH), BF16),
        jax.ShapeDtypeStruct((b, s, ATTN_WIDTH), BF16),
        jax.ShapeDtypeStruct((b, s, SSD_WIDTH), F32),
        jax.ShapeDtypeStruct((b, s, CONV_CHANNELS), F32),
        jax.ShapeDtypeStruct((b, s, LANES), F32),
        jax.ShapeDtypeStruct((b, DT_ROWS, s), F32),
    )
    return pl.pallas_call(
        _in_proj_kernel,
        grid=grid,
        in_specs=[
            tok(d),
            prev,
            _resident((1, d)),
            _resident(w_main.shape),
            _resident(w_dt.shape),
            _resident(group_mean.shape),
            _resident((1, ATTN_WIDTH)),
            _resident((1, ATTN_WIDTH)),
            pl.BlockSpec((tm, LANES), lambda j, i: (j, 0)),
            pl.BlockSpec((tm, LANES), lambda j, i: (j, 0)),
            _resident(conv_w.shape), _resident(conv_b.shape), _resident(dt_bias.shape),
        ],
        out_specs=(
            tok(ATTN_WIDTH), tok(ATTN_WIDTH), tok(ATTN_WIDTH), tok(SSD_WIDTH),
            tok(CONV_CHANNELS), tok(LANES),
            pl.BlockSpec((None, DT_ROWS, tm), lambda j, i: (i, 0, j)),
        ),
        out_shape=out_shapes,
        scratch_shapes=[pltpu.VMEM((CONV_HISTORY + tm, CONV_CHANNELS), F32)],
        compiler_params=pltpu.CompilerParams(
            dimension_semantics=("parallel", "parallel"), vmem_limit_bytes=VMEM_LIMIT),
        name="in_proj",
    )(x, x, attn_norm_w, w_main, w_dt, group_mean, qw, kw, cos, sin, conv_w, conv_b, dt_bias)


def _attn_blocks(blocks, store):
    qb = Q_BLOCK
    n_pairs = ATTN_WIDTH // LANES
    head0 = lax.broadcasted_iota(jnp.int32, (qb, LANES), 1) < HEAD_DIM
    items = [(i, pair) for i in range(len(blocks)) for pair in range(n_pairs)]

    def scores(item):
        i, pair = item
        load_q, load_k, _, load_bias = blocks[i]
        qt, kb, bias = load_q(pair), load_k(pair), load_bias()
        out = []
        for h in range(2):
            mine = head0 if h == 0 else jnp.logical_not(head0)
            qh = jnp.where(mine, qt, jnp.zeros_like(qt))
            out.append(_dot_nt(qh, kb) + bias)
        return out

    queue = [scores(item) for item in items[:ATTN_LOOKAHEAD]]
    parts = []
    for n, (i, pair) in enumerate(items):
        if n + ATTN_LOOKAHEAD < len(items):
            queue.append(scores(items[n + ATTN_LOOKAHEAD]))
        pending = queue.pop(0)
        vb = blocks[i][2](pair)
        v_ext = jnp.concatenate([vb, jnp.ones_like(vb)], axis=1)
        heads = []
        for sc in pending:
            m = jnp.max(sc, axis=1, keepdims=True)
            r = _dot(jnp.exp2(sc - m).astype(BF16), v_ext)
            heads.append((r[:, :LANES], r[:, LANES:], jnp.broadcast_to(m, (qb, LANES))))
        parts.append([jnp.where(head0, a, b) for a, b in zip(*heads)])
        if pair == n_pairs - 1:
            store(i, *[jnp.concatenate(t, axis=1) for t in zip(*parts)])
            parts = []


def _near_bias(first):
    qb = Q_BLOCK
    row = lax.broadcasted_iota(jnp.int32, (qb, 2 * qb), 0)
    col = lax.broadcasted_iota(jnp.int32, (qb, 2 * qb), 1)
    ok = (col <= row) if first else ((col >= row) & (col <= row + qb))
    return jnp.where(ok, 0.0, NEG_BIG)


def _far_bias(blocks_back):
    qb = Q_BLOCK
    row = lax.broadcasted_iota(jnp.int32, (qb, qb), 0)
    col = lax.broadcasted_iota(jnp.int32, (qb, qb), 1)
    dist = row - col + blocks_back * qb
    in_window = (dist >= 0) & (dist <= qb)
    on_stride = (dist >= 0) & ((dist & (FAR_RATIO - 1)) == 0)
    return jnp.where(in_window & on_stride, 1.0,
                     jnp.where(in_window | on_stride, 0.0, NEG_BIG))


def _attn_kernel(q_ref, k_ref, v_ref, o_ref,
                 tmp, q_cls, k_cls, v_cls, num_far, den_far, top_far, near_bias, far_bias):
    qb = Q_BLOCK
    step = pl.program_id(1)
    rows = q_ref.shape[0]
    s = k_ref.shape[0]
    n_steps = s // rows
    n_pairs = ATTN_WIDTH // LANES
    lanes = lambda pair: pl.ds(pair * LANES, LANES)
    far = (num_far, den_far, top_far)

    def to_class_order(src_rows, dst, at):
        for p in range(n_pairs):
            tmp[p] = src_rows[:, p * LANES:(p + 1) * LANES].astype(F32)
            for c in range(FAR_DIL):
                dst[c, at:at + qb, lanes(p)] = tmp[p, pl.ds(c, qb, stride=FAR_DIL), :].astype(BF16)

    @pl.when(step == 0)
    def _():
        for i in range(n_steps):
            to_class_order(k_ref[i * rows:(i + 1) * rows, :], k_cls, i * qb)
            to_class_order(v_ref[i * rows:(i + 1) * rows, :], v_cls, i * qb)

    to_class_order(q_ref[...], q_cls, 0)
    near_bias[0] = _near_bias(False)
    near_bias[1] = _near_bias(True)
    for d in range(3):
        far_bias[d] = _far_bias(d)

    def far_store(c, *stats):
        for dst, t in zip(far, stats):
            for p in range(n_pairs):
                dst[p, pl.ds(c, qb, stride=FAR_DIL), :] = t[:, p * LANES:(p + 1) * LANES]

    for n_kb in range(1, n_steps + 1):
        @pl.when(step == n_kb - 1)
        def _(n_kb=n_kb):
            def block(c):
                load_bias = lambda: jnp.concatenate(
                    [far_bias[min(n_kb - 1 - kb, 2)] for kb in range(n_kb)], axis=1)
                return (lambda p: q_cls[c, :, lanes(p)],
                        lambda p: k_cls[c, 0:n_kb * qb, lanes(p)],
                        lambda p: v_cls[c, 0:n_kb * qb, lanes(p)], load_bias)
            _attn_blocks([block(c) for c in range(FAR_DIL)], far_store)

    is_first = (step == 0).astype(jnp.int32)

    def near_block(j):
        q_rows = pl.ds(j * qb, qb)
        if j == 0:
            start = pl.multiple_of((step * FAR_DIL - 1 + is_first) * qb, qb)
            load_bias = lambda: near_bias[is_first]
        else:
            start = pl.multiple_of((step * FAR_DIL + j - 1) * qb, qb)
            load_bias = lambda: near_bias[0]
        kv_rows = pl.ds(start, 2 * qb)
        return (lambda p: q_ref[q_rows, lanes(p)], lambda p: k_ref[kv_rows, lanes(p)],
                lambda p: v_ref[kv_rows, lanes(p)], load_bias)

    def near_store(j, num, den, top):
        q_rows = pl.ds(j * qb, qb)
        num_f, den_f, top_f = [
            jnp.concatenate([t[p, q_rows, :] for p in range(n_pairs)], axis=1) for t in far]
        both = jnp.maximum(top, top_f)
        w_near, w_far = jnp.exp2(top - both), jnp.exp2(top_f - both)
        o_ref[q_rows, :] = ((w_near * num + w_far * num_f)
                            / (w_near * den + w_far * den_f)).astype(o_ref.dtype)

    _attn_blocks([near_block(j) for j in range(rows // qb)], near_store)


def _attention(q, k, v):
    b, s, w = q.shape
    assert DILATIONS == (1, FAR_DIL, FAR_DIL * FAR_RATIO) and s == Q_BLOCK * DILATIONS[-1]
    rows = FAR_DIL * Q_BLOCK
    q_blk = pl.BlockSpec((None, rows, w), lambda i, g: (i, g, 0))
    kv_blk = pl.BlockSpec((None, s, w), lambda i, g: (i, 0, 0))
    return pl.pallas_call(
        _attn_kernel,
        grid=(b, s // rows),
        in_specs=[q_blk, kv_blk, kv_blk],
        out_specs=q_blk,
        out_shape=jax.ShapeDtypeStruct((b, s, w), BF16),
        scratch_shapes=[
            pltpu.VMEM((w // LANES, rows, LANES), F32),
            pltpu.VMEM((FAR_DIL, Q_BLOCK, w), BF16),
            pltpu.VMEM((FAR_DIL, s // FAR_DIL, w), BF16),
            pltpu.VMEM((FAR_DIL, s // FAR_DIL, w), BF16),
            pltpu.VMEM((w // LANES, rows, LANES), F32),
            pltpu.VMEM((w // LANES, rows, LANES), F32),
            pltpu.VMEM((w // LANES, rows, LANES), F32),
            pltpu.VMEM((2, Q_BLOCK, 2 * Q_BLOCK), F32),
            pltpu.VMEM((3, Q_BLOCK, Q_BLOCK), F32),
        ],
        compiler_params=pltpu.CompilerParams(
            dimension_semantics=("parallel", "arbitrary"), vmem_limit_bytes=VMEM_LIMIT),
        name="dilated_attn",
    )(q, k, v)


def _softplus(x):
    return jnp.maximum(x, 0.0) + jnp.log1p(jnp.exp(-jnp.abs(x)))


def _silu(x):
    h = 0.5 * x
    return h + h * jnp.tanh(h)


def _ssd_kernel(act_ref, gate_ref, dtc_ref, dtr_ref, alog_row_ref, alog_col_ref, dskip_ref,
                nw_ref, y_ref, state):
    L = CHUNK

    @pl.when(pl.program_id(1) == 0)
    def _():
        state[...] = jnp.zeros(state.shape, F32)

    a_row = -jnp.exp(alog_row_ref[...])
    a_col = -jnp.exp(alog_col_ref[...])
    ri = lax.broadcasted_iota(jnp.int32, (L, L), 0)
    ci = lax.broadcasted_iota(jnp.int32, (L, L), 1)
    causal = ci <= ri
    tril = jnp.where(causal, 1.0, 0.0).astype(BF16)
    triu = jnp.where(ri <= ci, 1.0, 0.0).astype(BF16)
    gs = N_SSD_GROUPS * SSD_STATE
    gw = SSD_WIDTH // N_SSD_GROUPS

    n_pairs = SSD_WIDTH // LANES
    pairs_per_group = n_pairs // N_SSD_GROUPS
    head0 = lax.broadcasted_iota(jnp.int32, (L, LANES), 1) < SSD_HEAD_DIM
    head0_row = head0[:1, :]
    spread = (lax.broadcasted_iota(jnp.int32, (LANES, SSD_WIDTH), 0)
              == lax.broadcasted_iota(jnp.int32, (LANES, SSD_WIDTH), 1) // SSD_HEAD_DIM)
    spread = jnp.where(spread, 1.0, 0.0).astype(BF16)

    states = [state[p] for p in range(n_pairs)]
    for chunk in range(act_ref.shape[0] // L):
        rows = pl.ds(chunk * L, L)
        act = act_ref[rows, :]
        xs = act[:, :SSD_WIDTH]
        xs_b = xs.astype(BF16)
        bm = act[:, SSD_WIDTH:SSD_WIDTH + gs]
        cm = act[:, SSD_WIDTH + gs:SSD_WIDTH + 2 * gs]

        dt_c = dtc_ref[rows, :]
        dt_r = dtr_ref[:, chunk * L:(chunk + 1) * L]
        acs_c = _split_dot(tril, dt_c * a_row, 3, lhs_split=False)
        acs_r = _split_dot(dt_r * a_col, triu, 3)
        e_wide = _split_dot(jnp.exp(acs_c), spread, 2)

        ys = []
        for g in range(N_SSD_GROUPS):
            b_g = bm[:, g * SSD_STATE:(g + 1) * SSD_STATE]
            c_gb = cm[:, g * SSD_STATE:(g + 1) * SSD_STATE].astype(BF16)
            cb = _dot_nt(c_gb, b_g.astype(BF16))
            b_gt = b_g.T
            for q in range(pairs_per_group):
                p = g * pairs_per_group + q
                x_pair = xs_b[:, p * LANES:(p + 1) * LANES]
                zero = jnp.zeros_like(x_pair)
                x_split = jnp.concatenate([jnp.where(head0, x_pair, zero),
                                           jnp.where(head0, zero, x_pair)], axis=0)
                w_diag, w_state, last = [], [], []
                for h in (2 * p, 2 * p + 1):
                    acs_col = acs_c[:, h:h + 1]
                    acs_row = acs_r[h:h + 1, :]
                    dt_row = dt_r[h:h + 1, :]
                    seg = jnp.exp(jnp.where(causal, acs_col - acs_row, NEG_BIG))
                    w_diag.append((cb * seg * dt_row).astype(BF16))
                    acs_last = acs_row[:, L - 1:L]
                    w_state.append((b_gt * (jnp.exp(acs_last - acs_row) * dt_row)).astype(BF16))
                    last.append(jnp.exp(acs_last))
                y_pair = (_dot(jnp.concatenate(w_diag, axis=1), x_split)
                          + e_wide[:, p * LANES:(p + 1) * LANES]
                          * _dot(c_gb, states[p].astype(BF16)))
                states[p] = (jnp.where(head0_row, last[0], last[1]) * states[p]
                             + _dot(jnp.concatenate(w_state, axis=1), x_split))
                ys.append(y_pair)
        y = (jnp.concatenate(ys, axis=1) + dskip_ref[...] * xs) * gate_ref[rows, :]

        outs = []
        for g in range(N_SSD_GROUPS):
            yg = y[:, g * gw:(g + 1) * gw]
            outs.append(yg * lax.rsqrt(jnp.mean(yg * yg, axis=-1, keepdims=True) + EPS))
        y_ref[rows, :] = (jnp.concatenate(outs, axis=1) * nw_ref[...]).astype(y_ref.dtype)

    for p in range(n_pairs):
        state[p] = states[p]


def _ssd(act, gate, dtc, dtr, alog_row, alog_col, dskip, nw):
    b, s, _ = act.shape
    rows = SSD_CHUNKS_PER_STEP * CHUNK
    tok = lambda width: pl.BlockSpec((None, rows, width), lambda i, c: (i, c, 0))
    return pl.pallas_call(
        _ssd_kernel,
        grid=(b, s // rows),
        in_specs=[
            tok(CONV_CHANNELS), tok(SSD_WIDTH), tok(LANES),
            pl.BlockSpec((None, DT_ROWS, rows), lambda i, c: (i, 0, c)),
            _resident(alog_row.shape), _resident(alog_col.shape),
            _resident(dskip.shape), _resident(nw.shape),
        ],
        out_specs=tok(SSD_WIDTH),
        out_shape=jax.ShapeDtypeStruct((b, s, SSD_WIDTH), BF16),
        scratch_shapes=[pltpu.VMEM((SSD_WIDTH // LANES, SSD_STATE, LANES), F32)],
        compiler_params=pltpu.CompilerParams(
            dimension_semantics=("parallel", "arbitrary"), vmem_limit_bytes=VMEM_LIMIT),
        name="ssd",
    )(act, gate, dtc, dtr, alog_row, alog_col, dskip, nw)


def _out_mlp_kernel(x_ref, oa_ref, ys_ref, woa_ref, wos_ref, nw_ref, wup_ref, wdn_ref, o_ref):
    x1 = x_ref[...] + _dot(oa_ref[...], woa_ref[...]) + _dot(ys_ref[...], wos_ref[...])
    hm = x1 * lax.rsqrt(jnp.mean(x1 * x1, axis=-1, keepdims=True) + EPS) * nw_ref[...]
    up = _dot(hm.astype(BF16), wup_ref[...])
    act = jnp.square(jnp.maximum(up, 0.0)).astype(BF16)
    o_ref[...] = x1 + _dot(act, wdn_ref[...])


def _out_mlp(x, o_attn, y_ssd, wo_a, wo_s, mlp_norm_w, w_up, w_down, tm):
    t, d = x.shape
    tok = lambda width: pl.BlockSpec((tm, width), lambda i: (i, 0))
    return pl.pallas_call(
        _out_mlp_kernel,
        grid=(t // tm,),
        in_specs=[
            tok(d), tok(ATTN_WIDTH), tok(SSD_WIDTH),
            _resident(wo_a.shape), _resident(wo_s.shape), _resident((1, d)),
            _resident(w_up.shape), _resident(w_down.shape),
        ],
        out_specs=tok(d),
        out_shape=jax.ShapeDtypeStruct((t, d), F32),
        compiler_params=pltpu.CompilerParams(
            dimension_semantics=("parallel",), vmem_limit_bytes=VMEM_LIMIT),
        name="out_mlp",
    )(x, o_attn, y_ssd, wo_a, wo_s, mlp_norm_w, w_up, w_down)


def _rope_tables(s):
    half = HEAD_DIM // 2
    inv_freq = ROPE_THETA ** (-jnp.arange(half, dtype=F32) / half)
    lane_freq = jnp.tile(inv_freq, LANES // half)
    sign = jnp.tile(jnp.repeat(jnp.array([-1.0, 1.0], F32), half), LANES // HEAD_DIM)
    ang = jnp.arange(s, dtype=F32)[:, None] * lane_freq[None, :]
    return jnp.cos(ang), jnp.sin(ang) * sign[None, :]


def _pad_lanes(row):
    return jnp.pad(row, ((0, 0), (0, LANES - row.shape[1])))


def _layer(x, attn_norm_w, w_in, q_norm_w, k_norm_w, conv_w, conv_b, dt_bias, a_log, d_skip,
           ssd_norm_w, w_out, mlp_norm_w, w_up, w_down):
    b, s, d = x.shape
    w_t = w_in.T
    w_main = w_t[:MAIN_WIDTH].astype(BF16)
    w_dt_c = jnp.pad(w_t[MAIN_WIDTH:], ((0, LANES - N_SSD_HEADS), (0, 0))).astype(BF16)
    head_of = jnp.arange(ATTN_WIDTH, dtype=jnp.int32) // HEAD_DIM
    group_mean = jnp.where(head_of[:, None] == head_of[None, :], 1.0 / HEAD_DIM, 0.0).astype(BF16)
    cos, sin = _rope_tables(s)
    q, k, v, gate, act, dtc, dtr = _in_proj(
        x, attn_norm_w[None, :], w_main, w_dt_c, group_mean,
        jnp.tile(q_norm_w, N_ATTN_HEADS)[None, :], jnp.tile(k_norm_w, N_ATTN_HEADS)[None, :],
        cos, sin, conv_w, conv_b[None, :], _pad_lanes(dt_bias[None, :]), tm=PROJ_TOKENS)

    o_attn = _attention(q, k, v)

    col = lambda p: jnp.pad(p[:, None], ((0, DT_ROWS - N_SSD_HEADS), (0, 0)))
    y_ssd = _ssd(act, gate, dtc, dtr, _pad_lanes(a_log[None, :]), col(a_log),
                 jnp.repeat(d_skip, SSD_HEAD_DIM)[None, :], ssd_norm_w[None, :])

    out = _out_mlp(x.reshape(b * s, d), o_attn.reshape(b * s, ATTN_WIDTH),
                   y_ssd.reshape(b * s, SSD_WIDTH),
                   w_out[:ATTN_WIDTH].astype(BF16), w_out[ATTN_WIDTH:].astype(BF16),
                   mlp_norm_w[None, :], w_up.astype(BF16), w_down.astype(BF16), tm=MLP_TOKENS)
    return out.reshape(b, s, d)


def kernel(x, attn_norm_w, w_in, q_norm_w, k_norm_w, conv_w, conv_b, dt_bias, a_log, d_skip,
           ssd_norm_w, w_out, mlp_norm_w, w_up, w_down):
    for i in range(attn_norm_w.shape[0]):
        x = _layer(x, attn_norm_w[i], w_in[i], q_norm_w[i], k_norm_w[i], conv_w[i], conv_b[i],
                   dt_bias[i], a_log[i], d_skip[i], ssd_norm_w[i], w_out[i], mlp_norm_w[i],
                   w_up[i], w_down[i])
    return x
```

```python
import functools

import jax
import jax.numpy as jnp
from jax import lax
from jax.experimental import pallas as pl
from jax.experimental.pallas import tpu as pltpu

F32 = jnp.float32
BF16 = jnp.bfloat16

HEAD_DIM = 64
N_ATTN_HEADS = 8
ATTN_WIDTH = N_ATTN_HEADS * HEAD_DIM
ROPE_THETA = 10000.0
Q_BLOCK = 128
DILATIONS = (1, 4, 16)
FAR_DIL = 4
FAR_RATIO = 4
ATTN_LOOKAHEAD = 5
SSD_HEAD_DIM = 64
N_SSD_HEADS = 8
SSD_WIDTH = N_SSD_HEADS * SSD_HEAD_DIM
N_SSD_GROUPS = 2
HEADS_PER_GROUP = N_SSD_HEADS // N_SSD_GROUPS
SSD_STATE = 128
CONV_WIDTH = 4
CHUNK = 128
SSD_CHUNKS_PER_STEP = 16
SSD_LOOKAHEAD = 1
CONV_CHANNELS = SSD_WIDTH + 2 * N_SSD_GROUPS * SSD_STATE
CONV_HISTORY = 16
ROPE_ROWS = 512
CONV_GROUP = 256
EPS = 1e-6
NEG_BIG = -1e30
LOG2_E = 1.4426950408889634
SCORE_SCALE = HEAD_DIM ** -0.5 * LOG2_E

LANES = 128
DT_ROWS = 16
MAIN_WIDTH = 3 * ATTN_WIDTH + SSD_WIDTH + CONV_CHANNELS
VMEM_LIMIT = 56 * 1024 * 1024
PROJ_TOKENS = 1024
MLP_TOKENS = 512


def _dot(a, b):
    return jnp.dot(a, b, preferred_element_type=F32)


def _dot_nt(a, b):
    return lax.dot_general(a, b, (((1,), (1,)), ((), ())), preferred_element_type=F32)


def _split_dot(a, b, parts, lhs_split=True):
    src = a if lhs_split else b
    acc = None
    rem = src
    for _ in range(parts):
        piece = rem.astype(BF16)
        rem = rem - piece.astype(F32)
        term = _dot(piece, b) if lhs_split else _dot(a, piece)
        acc = term if acc is None else acc + term
    return acc


def _resident(shape):
    zeros = (0,) * len(shape)
    return pl.BlockSpec(shape, lambda *_: zeros, pipeline_mode=pl.Buffered(1))


def _in_proj_kernel(x_ref, xprev_ref, nw_ref, w_ref, wdt_ref, gm_ref, qw_ref, kw_ref,
                    cos_ref, sin_ref, cw_ref, cb_ref, dtb_ref,
                    q_ref, k_ref, v_ref, gate_ref, act_ref, dtc_ref, dtr_ref, ubuf):
    tm = x_ref.shape[0]

    def normed(x):
        y = x * lax.rsqrt(jnp.mean(x * x, axis=-1, keepdims=True) + EPS) * nw_ref[...]
        return y.astype(BF16)

    hb = normed(x_ref[...])

    lane = lax.broadcasted_iota(jnp.int32, (ROPE_ROWS, ATTN_WIDTH), 1)
    first_half = (lane % HEAD_DIM) < (HEAD_DIM // 2)

    def head_norm_rope(t, w, out_ref, scale):
        for r in range(tm // ROPE_ROWS):
            rows = pl.ds(r * ROPE_ROWS, ROPE_ROWS)
            tr = t[r * ROPE_ROWS:(r + 1) * ROPE_ROWS, :]
            cos = jnp.concatenate([cos_ref[rows, :]] * (ATTN_WIDTH // LANES), axis=1)
            sin = jnp.concatenate([sin_ref[rows, :]] * (ATTN_WIDTH // LANES), axis=1)
            ms = _dot((tr * tr).astype(BF16), gm_ref[...])
            n = tr * lax.rsqrt(ms + EPS) * w
            partner = jnp.where(first_half,
                                pltpu.roll(n, ATTN_WIDTH - HEAD_DIM // 2, 1),
                                pltpu.roll(n, HEAD_DIM // 2, 1))
            out_ref[rows, :] = ((n * cos + partner * sin) * scale).astype(BF16)

    o = 0
    q = _dot_nt(hb, w_ref[o:o + ATTN_WIDTH, :]); o += ATTN_WIDTH
    head_norm_rope(q, qw_ref[...], q_ref, SCORE_SCALE)
    k = _dot_nt(hb, w_ref[o:o + ATTN_WIDTH, :]); o += ATTN_WIDTH
    head_norm_rope(k, kw_ref[...], k_ref, 1.0)
    v_ref[...] = _dot_nt(hb, w_ref[o:o + ATTN_WIDTH, :]).astype(BF16); o += ATTN_WIDTH
    gate_ref[...] = _silu(_dot_nt(hb, w_ref[o:o + SSD_WIDTH, :])); o += SSD_WIDTH

    hb_prev = normed(xprev_ref[...])
    for c in range(CONV_CHANNELS // CONV_GROUP):
        lanes = pl.ds(c * CONV_GROUP, CONV_GROUP)
        w_c = w_ref[o + c * CONV_GROUP:o + (c + 1) * CONV_GROUP, :]
        u = _dot_nt(hb, w_c)
        hist = _dot_nt(hb_prev, w_c)
        ubuf[0:CONV_HISTORY, lanes] = jnp.where(pl.program_id(0) == 0, 0.0, hist)
        ubuf[CONV_HISTORY:CONV_HISTORY + tm, lanes] = u
        conv = cb_ref[:, lanes] + cw_ref[CONV_WIDTH - 1:CONV_WIDTH, lanes] * u
        for tap in range(CONV_WIDTH - 1):
            first = CONV_HISTORY - (CONV_WIDTH - 1 - tap)
            conv = conv + cw_ref[tap:tap + 1, lanes] * ubuf[first:first + tm, lanes]
        act_ref[:, lanes] = _silu(conv)

    dtc = _softplus(_dot_nt(hb, wdt_ref[...]) + dtb_ref[...])
    dtc_ref[...] = dtc
    dtr_ref[...] = dtc.T[:DT_ROWS, :]


def _in_proj(x, attn_norm_w, w_main, w_dt, group_mean, qw, kw, cos, sin, conv_w, conv_b, dt_bias,
             tm):
    b, s, d = x.shape
    grid = (s // tm, b)
    tok = lambda width: pl.BlockSpec((None, tm, width), lambda j, i: (i, j, 0))
    prev = pl.BlockSpec((None, CONV_HISTORY, d),
                        lambda j, i: (i, jnp.maximum(j * (tm // CONV_HISTORY) - 1, 0), 0))
    out_shapes = (
        jax.ShapeDtypeStruct((b, s, ATTN_WIDTH), BF16),
        jax.ShapeDtypeStruct((b, s, ATTN_WIDTH), BF16),
        jax.ShapeDtypeStruct((b, s, ATTN_WIDTH), BF16),
        jax.ShapeDtypeStruct((b, s, SSD_WIDTH), F32),
        jax.ShapeDtypeStruct((b, s, CONV_CHANNELS), F32),
        jax.ShapeDtypeStruct((b, s, LANES), F32),
        jax.ShapeDtypeStruct((b, DT_ROWS, s), F32),
    )
    return pl.pallas_call(
        _in_proj_kernel,
        grid=grid,
        in_specs=[
            tok(d),
            prev,
            _resident((1, d)),
            _resident(w_main.shape),
            _resident(w_dt.shape),
            _resident(group_mean.shape),
            _resident((1, ATTN_WIDTH)),
            _resident((1, ATTN_WIDTH)),
            pl.BlockSpec((tm, LANES), lambda j, i: (j, 0)),
            pl.BlockSpec((tm, LANES), lambda j, i: (j, 0)),
            _resident(conv_w.shape), _resident(conv_b.shape), _resident(dt_bias.shape),
        ],
        out_specs=(
            tok(ATTN_WIDTH), tok(ATTN_WIDTH), tok(ATTN_WIDTH), tok(SSD_WIDTH),
            tok(CONV_CHANNELS), tok(LANES),
            pl.BlockSpec((None, DT_ROWS, tm), lambda j, i: (i, 0, j)),
        ),
        out_shape=out_shapes,
        scratch_shapes=[pltpu.VMEM((CONV_HISTORY + tm, CONV_CHANNELS), F32)],
        compiler_params=pltpu.CompilerParams(
            dimension_semantics=("parallel", "parallel"), vmem_limit_bytes=VMEM_LIMIT),
        name="in_proj",
    )(x, x, attn_norm_w, w_main, w_dt, group_mean, qw, kw, cos, sin, conv_w, conv_b, dt_bias)


def _attn_blocks(blocks, store):
    qb = Q_BLOCK
    n_pairs = ATTN_WIDTH // LANES
    head0 = lax.broadcasted_iota(jnp.int32, (qb, LANES), 1) < HEAD_DIM
    items = [(i, pair) for i in range(len(blocks)) for pair in range(n_pairs)]

    def scores(item):
        i, pair = item
        load_q, load_k, _, load_bias = blocks[i]
        qt, kb, bias = load_q(pair), load_k(pair), load_bias()
        out = []
        for h in range(2):
            mine = head0 if h == 0 else jnp.logical_not(head0)
            qh = jnp.where(mine, qt, jnp.zeros_like(qt))
            out.append(_dot_nt(qh, kb) + bias)
        return out

    queue = [scores(item) for item in items[:ATTN_LOOKAHEAD]]
    parts = []
    for n, (i, pair) in enumerate(items):
        if n + ATTN_LOOKAHEAD < len(items):
            queue.append(scores(items[n + ATTN_LOOKAHEAD]))
        pending = queue.pop(0)
        vb = blocks[i][2](pair)
        v_ext = jnp.concatenate([vb, jnp.ones_like(vb)], axis=1)
        heads = []
        for sc in pending:
            m = jnp.max(sc, axis=1, keepdims=True)
            r = _dot(jnp.exp2(sc - m).astype(BF16), v_ext)
            heads.append((r[:, :LANES], r[:, LANES:], jnp.broadcast_to(m, (qb, LANES))))
        parts.append([jnp.where(head0, a, b) for a, b in zip(*heads)])
        if pair == n_pairs - 1:
            store(i, *[jnp.concatenate(t, axis=1) for t in zip(*parts)])
            parts = []


def _near_bias(first):
    qb = Q_BLOCK
    row = lax.broadcasted_iota(jnp.int32, (qb, 2 * qb), 0)
    col = lax.broadcasted_iota(jnp.int32, (qb, 2 * qb), 1)
    ok = (col <= row) if first else ((col >= row) & (col <= row + qb))
    return jnp.where(ok, 0.0, NEG_BIG)


def _far_bias(blocks_back):
    qb = Q_BLOCK
    row = lax.broadcasted_iota(jnp.int32, (qb, qb), 0)
    col = lax.broadcasted_iota(jnp.int32, (qb, qb), 1)
    dist = row - col + blocks_back * qb
    in_window = (dist >= 0) & (dist <= qb)
    on_stride = (dist >= 0) & ((dist & (FAR_RATIO - 1)) == 0)
    return jnp.where(in_window & on_stride, 1.0,
                     jnp.where(in_window | on_stride, 0.0, NEG_BIG))


def _attn_kernel(q_ref, k_ref, v_ref, o_ref,
                 tmp, q_cls, k_cls, v_cls, num_far, den_far, top_far, near_bias, far_bias):
    qb = Q_BLOCK
    step = pl.program_id(1)
    rows = q_ref.shape[0]
    s = k_ref.shape[0]
    n_steps = s // rows
    n_pairs = ATTN_WIDTH // LANES
    lanes = lambda pair: pl.ds(pair * LANES, LANES)
    far = (num_far, den_far, top_far)

    def to_class_order(src_rows, dst, at):
        for p in range(n_pairs):
            tmp[p] = src_rows[:, p * LANES:(p + 1) * LANES].astype(F32)
            for c in range(FAR_DIL):
                dst[c, at:at + qb, lanes(p)] = tmp[p, pl.ds(c, qb, stride=FAR_DIL), :].astype(BF16)

    @pl.when(step == 0)
    def _():
        for i in range(n_steps):
            to_class_order(k_ref[i * rows:(i + 1) * rows, :], k_cls, i * qb)
            to_class_order(v_ref[i * rows:(i + 1) * rows, :], v_cls, i * qb)

    to_class_order(q_ref[...], q_cls, 0)
    near_bias[0] = _near_bias(False)
    near_bias[1] = _near_bias(True)
    for d in range(3):
        far_bias[d] = _far_bias(d)

    def far_store(c, *stats):
        for dst, t in zip(far, stats):
            for p in range(n_pairs):
                dst[p, pl.ds(c, qb, stride=FAR_DIL), :] = t[:, p * LANES:(p + 1) * LANES]

    for n_kb in range(1, n_steps + 1):
        @pl.when(step == n_kb - 1)
        def _(n_kb=n_kb):
            def block(c):
                load_bias = lambda: jnp.concatenate(
                    [far_bias[min(n_kb - 1 - kb, 2)] for kb in range(n_kb)], axis=1)
                return (lambda p: q_cls[c, :, lanes(p)],
                        lambda p: k_cls[c, 0:n_kb * qb, lanes(p)],
                        lambda p: v_cls[c, 0:n_kb * qb, lanes(p)], load_bias)
            _attn_blocks([block(c) for c in range(FAR_DIL)], far_store)

    is_first = (step == 0).astype(jnp.int32)

    def near_block(j):
        q_rows = pl.ds(j * qb, qb)
        if j == 0:
            start = pl.multiple_of((step * FAR_DIL - 1 + is_first) * qb, qb)
            load_bias = lambda: near_bias[is_first]
        else:
            start = pl.multiple_of((step * FAR_DIL + j - 1) * qb, qb)
            load_bias = lambda: near_bias[0]
        kv_rows = pl.ds(start, 2 * qb)
        return (lambda p: q_ref[q_rows, lanes(p)], lambda p: k_ref[kv_rows, lanes(p)],
                lambda p: v_ref[kv_rows, lanes(p)], load_bias)

    def near_store(j, num, den, top):
        q_rows = pl.ds(j * qb, qb)
        num_f, den_f, top_f = [
            jnp.concatenate([t[p, q_rows, :] for p in range(n_pairs)], axis=1) for t in far]
        both = jnp.maximum(top, top_f)
        w_near, w_far = jnp.exp2(top - both), jnp.exp2(top_f - both)
        o_ref[q_rows, :] = ((w_near * num + w_far * num_f)
                            / (w_near * den + w_far * den_f)).astype(o_ref.dtype)

    _attn_blocks([near_block(j) for j in range(rows // qb)], near_store)


def _attention(q, k, v):
    b, s, w = q.shape
    assert DILATIONS == (1, FAR_DIL, FAR_DIL * FAR_RATIO) and s == Q_BLOCK * DILATIONS[-1]
    rows = FAR_DIL * Q_BLOCK
    q_blk = pl.BlockSpec((None, rows, w), lambda i, g: (i, g, 0))
    kv_blk = pl.BlockSpec((None, s, w), lambda i, g: (i, 0, 0))
    return pl.pallas_call(
        _attn_kernel,
        grid=(b, s // rows),
        in_specs=[q_blk, kv_blk, kv_blk],
        out_specs=q_blk,
        out_shape=jax.ShapeDtypeStruct((b, s, w), BF16),
        scratch_shapes=[
            pltpu.VMEM((w // LANES, rows, LANES), F32),
            pltpu.VMEM((FAR_DIL, Q_BLOCK, w), BF16),
            pltpu.VMEM((FAR_DIL, s // FAR_DIL, w), BF16),
            pltpu.VMEM((FAR_DIL, s // FAR_DIL, w), BF16),
            pltpu.VMEM((w // LANES, rows, LANES), F32),
            pltpu.VMEM((w // LANES, rows, LANES), F32),
            pltpu.VMEM((w // LANES, rows, LANES), F32),
            pltpu.VMEM((2, Q_BLOCK, 2 * Q_BLOCK), F32),
            pltpu.VMEM((3, Q_BLOCK, Q_BLOCK), F32),
        ],
        compiler_params=pltpu.CompilerParams(
            dimension_semantics=("parallel", "arbitrary"), vmem_limit_bytes=VMEM_LIMIT),
        name="dilated_attn",
    )(q, k, v)


def _softplus(x):
    return jnp.maximum(x, 0.0) + jnp.log1p(jnp.exp(-jnp.abs(x)))


def _silu(x):
    h = 0.5 * x
    return h + h * jnp.tanh(h)


def _ssd_kernel(act_ref, gate_ref, dtc_ref, dtr_ref, alog_row_ref, alog_col_ref, dskip_ref,
                nw_ref, y_ref, state):
    L = CHUNK

    @pl.when(pl.program_id(1) == 0)
    def _():
        state[...] = jnp.zeros(state.shape, F32)

    a_row = -jnp.exp(alog_row_ref[...]) * LOG2_E
    a_col = -jnp.exp(alog_col_ref[...]) * LOG2_E
    ri = lax.broadcasted_iota(jnp.int32, (L, L), 0)
    ci = lax.broadcasted_iota(jnp.int32, (L, L), 1)
    causal = ci <= ri
    tril = jnp.where(causal, 1.0, 0.0).astype(BF16)
    triu = jnp.where(ri <= ci, 1.0, 0.0).astype(BF16)
    gs = N_SSD_GROUPS * SSD_STATE
    gw = SSD_WIDTH // N_SSD_GROUPS

    n_pairs = SSD_WIDTH // LANES
    pairs_per_group = n_pairs // N_SSD_GROUPS
    head0 = lax.broadcasted_iota(jnp.int32, (L, LANES), 1) < SSD_HEAD_DIM
    head0_row = head0[:1, :]
    spread = (lax.broadcasted_iota(jnp.int32, (LANES, SSD_WIDTH), 0)
              == lax.broadcasted_iota(jnp.int32, (LANES, SSD_WIDTH), 1) // SSD_HEAD_DIM)
    spread = jnp.where(spread, 1.0, 0.0).astype(BF16)

    def prologue(chunk):
        rows = pl.ds(chunk * L, L)
        act = act_ref[rows, :]
        xs = act[:, :SSD_WIDTH]
        xs_b = xs.astype(BF16)
        bm = act[:, SSD_WIDTH:SSD_WIDTH + gs]
        cm = act[:, SSD_WIDTH + gs:SSD_WIDTH + 2 * gs]
        dt_c = dtc_ref[rows, :]
        dt_r = dtr_ref[:, chunk * L:(chunk + 1) * L]
        acs_c = _split_dot(tril, dt_c * a_row, 3, lhs_split=False)
        acs_r = _split_dot(dt_r * a_col, triu, 3)
        e_wide = _split_dot(jnp.exp2(acs_c), spread, 2)
        pairs = []
        for g in range(N_SSD_GROUPS):
            b_g = bm[:, g * SSD_STATE:(g + 1) * SSD_STATE]
            c_gb = cm[:, g * SSD_STATE:(g + 1) * SSD_STATE].astype(BF16)
            cb = _dot_nt(c_gb, b_g.astype(BF16))
            b_gt = b_g.T
            for q in range(pairs_per_group):
                p = g * pairs_per_group + q
                x_pair = xs_b[:, p * LANES:(p + 1) * LANES]
                zero = jnp.zeros_like(x_pair)
                x_split = jnp.concatenate([jnp.where(head0, x_pair, zero),
                                           jnp.where(head0, zero, x_pair)], axis=0)
                w_diag, w_state, last = [], [], []
                for h in (2 * p, 2 * p + 1):
                    acs_col = acs_c[:, h:h + 1]
                    acs_row = acs_r[h:h + 1, :]
                    dt_row = dt_r[h:h + 1, :]
                    seg = jnp.exp2(jnp.where(causal, acs_col - acs_row, NEG_BIG))
                    w_diag.append((cb * seg * dt_row).astype(BF16))
                    acs_last = acs_row[:, L - 1:L]
                    w_state.append((b_gt * (jnp.exp2(acs_last - acs_row) * dt_row)).astype(BF16))
                    last.append(jnp.exp2(acs_last))
                pairs.append((c_gb,
                              _dot(jnp.concatenate(w_diag, axis=1), x_split),
                              _dot(jnp.concatenate(w_state, axis=1), x_split),
                              jnp.where(head0_row, last[0], last[1])))
        return rows, xs, e_wide, pairs

    states = [state[p] for p in range(n_pairs)]
    n_chunks = act_ref.shape[0] // L
    queue = [prologue(c) for c in range(min(SSD_LOOKAHEAD, n_chunks))]
    for chunk in range(n_chunks):
        if chunk + SSD_LOOKAHEAD < n_chunks:
            queue.append(prologue(chunk + SSD_LOOKAHEAD))
        rows, xs, e_wide, pairs = queue.pop(0)
        ys = []
        for p, (c_gb, y_within, increment, decay) in enumerate(pairs):
            ys.append(y_within + e_wide[:, p * LANES:(p + 1) * LANES]
                      * _dot(c_gb, states[p].astype(BF16)))
            states[p] = decay * states[p] + increment
        y = (jnp.concatenate(ys, axis=1) + dskip_ref[...] * xs) * gate_ref[rows, :]

        outs = []
        for g in range(N_SSD_GROUPS):
            yg = y[:, g * gw:(g + 1) * gw]
            outs.append(yg * lax.rsqrt(jnp.mean(yg * yg, axis=-1, keepdims=True) + EPS))
        y_ref[rows, :] = (jnp.concatenate(outs, axis=1) * nw_ref[...]).astype(y_ref.dtype)

    for p in range(n_pairs):
        state[p] = states[p]


def _ssd(act, gate, dtc, dtr, alog_row, alog_col, dskip, nw):
    b, s, _ = act.shape
    rows = SSD_CHUNKS_PER_STEP * CHUNK
    tok = lambda width: pl.BlockSpec((None, rows, width), lambda i, c: (i, c, 0))
    return pl.pallas_call(
        _ssd_kernel,
        grid=(b, s // rows),
        in_specs=[
            tok(CONV_CHANNELS), tok(SSD_WIDTH), tok(LANES),
            pl.BlockSpec((None, DT_ROWS, rows), lambda i, c: (i, 0, c)),
            _resident(alog_row.shape), _resident(alog_col.shape),
            _resident(dskip.shape), _resident(nw.shape),
        ],
        out_specs=tok(SSD_WIDTH),
        out_shape=jax.ShapeDtypeStruct((b, s, SSD_WIDTH), BF16),
        scratch_shapes=[pltpu.VMEM((SSD_WIDTH // LANES, SSD_STATE, LANES), F32)],
        compiler_params=pltpu.CompilerParams(
            dimension_semantics=("parallel", "arbitrary"), vmem_limit_bytes=VMEM_LIMIT),
        name="ssd",
    )(act, gate, dtc, dtr, alog_row, alog_col, dskip, nw)


def _out_mlp_kernel(x_ref, oa_ref, ys_ref, woa_ref, wos_ref, nw_ref, wup_ref, wdn_ref, o_ref):
    x1 = x_ref[...] + _dot(oa_ref[...], woa_ref[...]) + _dot(ys_ref[...], wos_ref[...])
    hm = x1 * lax.rsqrt(jnp.mean(x1 * x1, axis=-1, keepdims=True) + EPS) * nw_ref[...]
    up = _dot(hm.astype(BF16), wup_ref[...])
    act = jnp.square(jnp.maximum(up, 0.0)).astype(BF16)
    o_ref[...] = x1 + _dot(act, wdn_ref[...])


def _out_mlp(x, o_attn, y_ssd, wo_a, wo_s, mlp_norm_w, w_up, w_down, tm):
    t, d = x.shape
    tok = lambda width: pl.BlockSpec((tm, width), lambda i: (i, 0))
    return pl.pallas_call(
        _out_mlp_kernel,
        grid=(t // tm,),
        in_specs=[
            tok(d), tok(ATTN_WIDTH), tok(SSD_WIDTH),
            _resident(wo_a.shape), _resident(wo_s.shape), _resident((1, d)),
            _resident(w_up.shape), _resident(w_down.shape),
        ],
        out_specs=tok(d),
        out_shape=jax.ShapeDtypeStruct((t, d), F32),
        compiler_params=pltpu.CompilerParams(
            dimension_semantics=("parallel",), vmem_limit_bytes=VMEM_LIMIT),
        name="out_mlp",
    )(x, o_attn, y_ssd, wo_a, wo_s, mlp_norm_w, w_up, w_down)


def _rope_tables(s):
    half = HEAD_DIM // 2
    inv_freq = ROPE_THETA ** (-jnp.arange(half, dtype=F32) / half)
    lane_freq = jnp.tile(inv_freq, LANES // half)
    sign = jnp.tile(jnp.repeat(jnp.array([-1.0, 1.0], F32), half), LANES // HEAD_DIM)
    ang = jnp.arange(s, dtype=F32)[:, None] * lane_freq[None, :]
    return jnp.cos(ang), jnp.sin(ang) * sign[None, :]


def _pad_lanes(row):
    return jnp.pad(row, ((0, 0), (0, LANES - row.shape[1])))


def _layer(x, attn_norm_w, w_in, q_norm_w, k_norm_w, conv_w, conv_b, dt_bias, a_log, d_skip,
           ssd_norm_w, w_out, mlp_norm_w, w_up, w_down):
    b, s, d = x.shape
    w_t = w_in.T
    w_main = w_t[:MAIN_WIDTH].astype(BF16)
    w_dt_c = jnp.pad(w_t[MAIN_WIDTH:], ((0, LANES - N_SSD_HEADS), (0, 0))).astype(BF16)
    head_of = jnp.arange(ATTN_WIDTH, dtype=jnp.int32) // HEAD_DIM
    group_mean = jnp.where(head_of[:, None] == head_of[None, :], 1.0 / HEAD_DIM, 0.0).astype(BF16)
    cos, sin = _rope_tables(s)
    q, k, v, gate, act, dtc, dtr = _in_proj(
        x, attn_norm_w[None, :], w_main, w_dt_c, group_mean,
        jnp.tile(q_norm_w, N_ATTN_HEADS)[None, :], jnp.tile(k_norm_w, N_ATTN_HEADS)[None, :],
        cos, sin, conv_w, conv_b[None, :], _pad_lanes(dt_bias[None, :]), tm=PROJ_TOKENS)

    o_attn = _attention(q, k, v)

    col = lambda p: jnp.pad(p[:, None], ((0, DT_ROWS - N_SSD_HEADS), (0, 0)))
    y_ssd = _ssd(act, gate, dtc, dtr, _pad_lanes(a_log[None, :]), col(a_log),
                 jnp.repeat(d_skip, SSD_HEAD_DIM)[None, :], ssd_norm_w[None, :])

    out = _out_mlp(x.reshape(b * s, d), o_attn.reshape(b * s, ATTN_WIDTH),
                   y_ssd.reshape(b * s, SSD_WIDTH),
                   w_out[:ATTN_WIDTH].astype(BF16), w_out[ATTN_WIDTH:].astype(BF16),
                   mlp_norm_w[None, :], w_up.astype(BF16), w_down.astype(BF16), tm=MLP_TOKENS)
    return out.reshape(b, s, d)


def kernel(x, attn_norm_w, w_in, q_norm_w, k_norm_w, conv_w, conv_b, dt_bias, a_log, d_skip,
           ssd_norm_w, w_out, mlp_norm_w, w_up, w_down):
    for i in range(attn_norm_w.shape[0]):
        x = _layer(x, attn_norm_w[i], w_in[i], q_norm_w[i], k_norm_w[i], conv_w[i], conv_b[i],
                   dt_bias[i], a_log[i], d_skip[i], ssd_norm_w[i], w_out[i], mlp_norm_w[i],
                   w_up[i], w_down[i])
    return x
```

```python
import functools

import jax
import jax.numpy as jnp
from jax import lax
from jax.experimental import pallas as pl
from jax.experimental.pallas import tpu as pltpu

F32 = jnp.float32
BF16 = jnp.bfloat16

HEAD_DIM = 64
N_ATTN_HEADS = 8
ATTN_WIDTH = N_ATTN_HEADS * HEAD_DIM
ROPE_THETA = 10000.0
Q_BLOCK = 128
DILATIONS = (1, 4, 16)
FAR_DIL = 4
FAR_RATIO = 4
ATTN_LOOKAHEAD = 5
SSD_HEAD_DIM = 64
N_SSD_HEADS = 8
SSD_WIDTH = N_SSD_HEADS * SSD_HEAD_DIM
N_SSD_GROUPS = 2
HEADS_PER_GROUP = N_SSD_HEADS // N_SSD_GROUPS
SSD_STATE = 128
CONV_WIDTH = 4
CHUNK = 128
SSD_CHUNKS_PER_STEP = 16
SSD_LOOKAHEAD = 1
CONV_CHANNELS = SSD_WIDTH + 2 * N_SSD_GROUPS * SSD_STATE
CONV_HISTORY = 16
ROPE_ROWS = 512
CONV_GROUP = 256
EPS = 1e-6
NEG_BIG = -1e30
LOG2_E = 1.4426950408889634
SCORE_SCALE = HEAD_DIM ** -0.5 * LOG2_E

LANES = 128
DT_ROWS = 16
MAIN_WIDTH = 3 * ATTN_WIDTH + SSD_WIDTH + CONV_CHANNELS
VMEM_LIMIT = 56 * 1024 * 1024
PROJ_TOKENS = 1024
MLP_TOKENS = 512


def _dot(a, b):
    return jnp.dot(a, b, preferred_element_type=F32)


def _dot_nt(a, b):
    return lax.dot_general(a, b, (((1,), (1,)), ((), ())), preferred_element_type=F32)


def _split_dot(a, b, parts, lhs_split=True):
    src = a if lhs_split else b
    acc = None
    rem = src
    for _ in range(parts):
        piece = rem.astype(BF16)
        rem = rem - piece.astype(F32)
        term = _dot(piece, b) if lhs_split else _dot(a, piece)
        acc = term if acc is None else acc + term
    return acc


def _resident(shape):
    zeros = (0,) * len(shape)
    return pl.BlockSpec(shape, lambda *_: zeros, pipeline_mode=pl.Buffered(1))


def _in_proj_kernel(x_ref, xprev_ref, nw_ref, w_ref, wdt_ref, gm_ref, qw_ref, kw_ref,
                    cos_ref, sin_ref, cw_ref, cb_ref, dtb_ref,
                    q_ref, k_ref, v_ref, gate_ref, act_ref, dtc_ref, dtr_ref, ubuf):
    tm = x_ref.shape[0]

    def normed(x):
        y = x * lax.rsqrt(jnp.mean(x * x, axis=-1, keepdims=True) + EPS) * nw_ref[...]
        return y.astype(BF16)

    hb = normed(x_ref[...])

    lane = lax.broadcasted_iota(jnp.int32, (ROPE_ROWS, ATTN_WIDTH), 1)
    first_half = (lane % HEAD_DIM) < (HEAD_DIM // 2)

    def head_norm_rope(t, w, out_ref, scale):
        for r in range(tm // ROPE_ROWS):
            rows = pl.ds(r * ROPE_ROWS, ROPE_ROWS)
            tr = t[r * ROPE_ROWS:(r + 1) * ROPE_ROWS, :]
            cos = jnp.concatenate([cos_ref[rows, :]] * (ATTN_WIDTH // LANES), axis=1)
            sin = jnp.concatenate([sin_ref[rows, :]] * (ATTN_WIDTH // LANES), axis=1)
            ms = _dot((tr * tr).astype(BF16), gm_ref[...])
            n = tr * lax.rsqrt(ms + EPS) * w
            partner = jnp.where(first_half,
                                pltpu.roll(n, ATTN_WIDTH - HEAD_DIM // 2, 1),
                                pltpu.roll(n, HEAD_DIM // 2, 1))
            out_ref[rows, :] = ((n * cos + partner * sin) * scale).astype(BF16)

    o = 0
    q = _dot_nt(hb, w_ref[o:o + ATTN_WIDTH, :]); o += ATTN_WIDTH
    head_norm_rope(q, qw_ref[...], q_ref, SCORE_SCALE)
    k = _dot_nt(hb, w_ref[o:o + ATTN_WIDTH, :]); o += ATTN_WIDTH
    head_norm_rope(k, kw_ref[...], k_ref, 1.0)
    v_ref[...] = _dot_nt(hb, w_ref[o:o + ATTN_WIDTH, :]).astype(BF16); o += ATTN_WIDTH
    gate_ref[...] = _silu(_dot_nt(hb, w_ref[o:o + SSD_WIDTH, :])); o += SSD_WIDTH

    hb_prev = normed(xprev_ref[...])
    for c in range(CONV_CHANNELS // CONV_GROUP):
        lanes = pl.ds(c * CONV_GROUP, CONV_GROUP)
        w_c = w_ref[o + c * CONV_GROUP:o + (c + 1) * CONV_GROUP, :]
        u = _dot_nt(hb, w_c)
        hist = _dot_nt(hb_prev, w_c)
        ubuf[0:CONV_HISTORY, lanes] = jnp.where(pl.program_id(0) == 0, 0.0, hist)
        ubuf[CONV_HISTORY:CONV_HISTORY + tm, lanes] = u
        conv = cb_ref[:, lanes] + cw_ref[CONV_WIDTH - 1:CONV_WIDTH, lanes] * u
        for tap in range(CONV_WIDTH - 1):
            first = CONV_HISTORY - (CONV_WIDTH - 1 - tap)
            conv = conv + cw_ref[tap:tap + 1, lanes] * ubuf[first:first + tm, lanes]
        act_ref[:, lanes] = _silu(conv)

    dtc = _softplus(_dot_nt(hb, wdt_ref[...]) + dtb_ref[...])
    dtc_ref[...] = dtc
    dtr_ref[...] = dtc.T[:DT_ROWS, :]


def _in_proj(x, attn_norm_w, w_main, w_dt, group_mean, qw, kw, cos, sin, conv_w, conv_b, dt_bias,
             tm):
    b, s, d = x.shape
    grid = (s // tm, b)
    tok = lambda width: pl.BlockSpec((None, tm, width), lambda j, i: (i, j, 0))
    prev = pl.BlockSpec((None, CONV_HISTORY, d),
                        lambda j, i: (i, jnp.maximum(j * (tm // CONV_HISTORY) - 1, 0), 0))
    out_shapes = (
        jax.ShapeDtypeStruct((b, s, ATTN_WIDTH), BF16),
        jax.ShapeDtypeStruct((b, s, ATTN_WIDTH), BF16),
        jax.ShapeDtypeStruct((b, s, ATTN_WIDTH), BF16),
        jax.ShapeDtypeStruct((b, s, SSD_WIDTH), F32),
        jax.ShapeDtypeStruct((b, s, CONV_CHANNELS), F32),
        jax.ShapeDtypeStruct((b, s, LANES), F32),
        jax.ShapeDtypeStruct((b, DT_ROWS, s), F32),
    )
    return pl.pallas_call(
        _in_proj_kernel,
        grid=grid,
        in_specs=[
            tok(d),
            prev,
            _resident((1, d)),
            _resident(w_main.shape),
            _resident(w_dt.shape),
            _resident(group_mean.shape),
            _resident((1, ATTN_WIDTH)),
            _resident((1, ATTN_WIDTH)),
            pl.BlockSpec((tm, LANES), lambda j, i: (j, 0)),
            pl.BlockSpec((tm, LANES), lambda j, i: (j, 0)),
            _resident(conv_w.shape), _resident(conv_b.shape), _resident(dt_bias.shape),
        ],
        out_specs=(
            tok(ATTN_WIDTH), tok(ATTN_WIDTH), tok(ATTN_WIDTH), tok(SSD_WIDTH),
            tok(CONV_CHANNELS), tok(LANES),
            pl.BlockSpec((None, DT_ROWS, tm), lambda j, i: (i, 0, j)),
        ),
        out_shape=out_shapes,
        scratch_shapes=[pltpu.VMEM((CONV_HISTORY + tm, CONV_CHANNELS), F32)],
        compiler_params=pltpu.CompilerParams(
            dimension_semantics=("parallel", "parallel"), vmem_limit_bytes=VMEM_LIMIT),
        name="in_proj",
    )(x, x, attn_norm_w, w_main, w_dt, group_mean, qw, kw, cos, sin, conv_w, conv_b, dt_bias)


def _attn_blocks(blocks):
    qb = Q_BLOCK
    n_pairs = ATTN_WIDTH // LANES
    head0 = lax.broadcasted_iota(jnp.int32, (qb, LANES), 1) < HEAD_DIM
    items = [(i, pair) for i in range(len(blocks)) for pair in range(n_pairs)]

    def scores(item):
        i, pair = item
        load_q, load_k, _, load_bias, _ = blocks[i]
        qt, kb, bias = load_q(pair), load_k(pair), load_bias()
        out = []
        for h in range(2):
            mine = head0 if h == 0 else jnp.logical_not(head0)
            qh = jnp.where(mine, qt, jnp.zeros_like(qt))
            out.append(_dot_nt(qh, kb) + bias)
        return out

    queue = [scores(item) for item in items[:ATTN_LOOKAHEAD]]
    parts = []
    for n, (i, pair) in enumerate(items):
        if n + ATTN_LOOKAHEAD < len(items):
            queue.append(scores(items[n + ATTN_LOOKAHEAD]))
        pending = queue.pop(0)
        vb = blocks[i][2](pair)
        v_ext = jnp.concatenate([vb, jnp.ones_like(vb)], axis=1)
        heads = []
        for sc in pending:
            m = jnp.max(sc, axis=1, keepdims=True)
            r = _dot(jnp.exp2(sc - m).astype(BF16), v_ext)
            heads.append((r[:, :LANES], r[:, LANES:], jnp.broadcast_to(m, (qb, LANES))))
        parts.append([jnp.where(head0, a, b) for a, b in zip(*heads)])
        if pair == n_pairs - 1:
            blocks[i][4](*[jnp.concatenate(t, axis=1) for t in zip(*parts)])
            parts = []


def _near_bias(first):
    qb = Q_BLOCK
    row = lax.broadcasted_iota(jnp.int32, (qb, 2 * qb), 0)
    col = lax.broadcasted_iota(jnp.int32, (qb, 2 * qb), 1)
    ok = (col <= row) if first else ((col >= row) & (col <= row + qb))
    return jnp.where(ok, 0.0, NEG_BIG)


def _far_bias(blocks_back):
    qb = Q_BLOCK
    row = lax.broadcasted_iota(jnp.int32, (qb, qb), 0)
    col = lax.broadcasted_iota(jnp.int32, (qb, qb), 1)
    dist = row - col + blocks_back * qb
    in_window = (dist >= 0) & (dist <= qb)
    on_stride = (dist >= 0) & ((dist & (FAR_RATIO - 1)) == 0)
    return jnp.where(in_window & on_stride, 1.0,
                     jnp.where(in_window | on_stride, 0.0, NEG_BIG))


def _attn_kernel(q_ref, k_ref, v_ref, o_ref,
                 tmp, q_cls, k_cls, v_cls, num_far, den_far, top_far, near_bias, far_bias):
    qb = Q_BLOCK
    step = pl.program_id(1)
    rows = q_ref.shape[0]
    s = k_ref.shape[0]
    n_steps = s // rows
    n_pairs = ATTN_WIDTH // LANES
    lanes = lambda pair: pl.ds(pair * LANES, LANES)
    far = (num_far, den_far, top_far)

    def to_class_order(src_rows, dst, at):
        for p in range(n_pairs):
            tmp[p] = src_rows[:, p * LANES:(p + 1) * LANES].astype(F32)
            for c in range(FAR_DIL):
                dst[c, at:at + qb, lanes(p)] = tmp[p, pl.ds(c, qb, stride=FAR_DIL), :].astype(BF16)

    @pl.when(step == 0)
    def _():
        for i in range(n_steps):
            to_class_order(k_ref[i * rows:(i + 1) * rows, :], k_cls, i * qb)
            to_class_order(v_ref[i * rows:(i + 1) * rows, :], v_cls, i * qb)

    to_class_order(q_ref[...], q_cls, 0)
    near_bias[0] = _near_bias(False)
    near_bias[1] = _near_bias(True)
    for d in range(3):
        far_bias[d] = _far_bias(d)

    def far_block(c, n_kb):
        def store(*stats):
            for dst, t in zip(far, stats):
                for p in range(n_pairs):
                    dst[p, pl.ds(c, qb, stride=FAR_DIL), :] = t[:, p * LANES:(p + 1) * LANES]

        load_bias = lambda: jnp.concatenate(
            [far_bias[min(n_kb - 1 - kb, 2)] for kb in range(n_kb)], axis=1)
        return (lambda p: q_cls[c, :, lanes(p)],
                lambda p: k_cls[c, 0:n_kb * qb, lanes(p)],
                lambda p: v_cls[c, 0:n_kb * qb, lanes(p)], load_bias, store)

    is_first = (step == 0).astype(jnp.int32)

    def near_block(j):
        q_rows = pl.ds(j * qb, qb)
        if j == 0:
            start = pl.multiple_of((step * FAR_DIL - 1 + is_first) * qb, qb)
            load_bias = lambda: near_bias[is_first]
        else:
            start = pl.multiple_of((step * FAR_DIL + j - 1) * qb, qb)
            load_bias = lambda: near_bias[0]
        kv_rows = pl.ds(start, 2 * qb)

        def store(num, den, top):
            num_f, den_f, top_f = [
                jnp.concatenate([t[p, q_rows, :] for p in range(n_pairs)], axis=1) for t in far]
            both = jnp.maximum(top, top_f)
            w_near, w_far = jnp.exp2(top - both), jnp.exp2(top_f - both)
            o_ref[q_rows, :] = ((w_near * num + w_far * num_f)
                                / (w_near * den + w_far * den_f)).astype(o_ref.dtype)

        return (lambda p: q_ref[q_rows, lanes(p)], lambda p: k_ref[kv_rows, lanes(p)],
                lambda p: v_ref[kv_rows, lanes(p)], load_bias, store)

    for n_kb in range(1, n_steps + 1):
        @pl.when(step == n_kb - 1)
        def _(n_kb=n_kb):
            _attn_blocks([far_block(c, n_kb) for c in range(FAR_DIL)]
                         + [near_block(j) for j in range(rows // qb)])


def _attention(q, k, v):
    b, s, w = q.shape
    assert DILATIONS == (1, FAR_DIL, FAR_DIL * FAR_RATIO) and s == Q_BLOCK * DILATIONS[-1]
    rows = FAR_DIL * Q_BLOCK
    q_blk = pl.BlockSpec((None, rows, w), lambda i, g: (i, g, 0))
    kv_blk = pl.BlockSpec((None, s, w), lambda i, g: (i, 0, 0))
    return pl.pallas_call(
        _attn_kernel,
        grid=(b, s // rows),
        in_specs=[q_blk, kv_blk, kv_blk],
        out_specs=q_blk,
        out_shape=jax.ShapeDtypeStruct((b, s, w), BF16),
        scratch_shapes=[
            pltpu.VMEM((w // LANES, rows, LANES), F32),
            pltpu.VMEM((FAR_DIL, Q_BLOCK, w), BF16),
            pltpu.VMEM((FAR_DIL, s // FAR_DIL, w), BF16),
            pltpu.VMEM((FAR_DIL, s // FAR_DIL, w), BF16),
            pltpu.VMEM((w // LANES, rows, LANES), F32),
            pltpu.VMEM((w // LANES, rows, LANES), F32),
            pltpu.VMEM((w // LANES, rows, LANES), F32),
            pltpu.VMEM((2, Q_BLOCK, 2 * Q_BLOCK), F32),
            pltpu.VMEM((3, Q_BLOCK, Q_BLOCK), F32),
        ],
        compiler_params=pltpu.CompilerParams(
            dimension_semantics=("parallel", "arbitrary"), vmem_limit_bytes=VMEM_LIMIT),
        name="dilated_attn",
    )(q, k, v)


def _softplus(x):
    return jnp.maximum(x, 0.0) + jnp.log1p(jnp.exp(-jnp.abs(x)))


def _silu(x):
    h = 0.5 * x
    return h + h * jnp.tanh(h)


def _ssd_kernel(act_ref, gate_ref, dtc_ref, dtr_ref, alog_row_ref, alog_col_ref, dskip_ref,
                nw_ref, y_ref, state):
    L = CHUNK

    @pl.when(pl.program_id(1) == 0)
    def _():
        state[...] = jnp.zeros(state.shape, F32)

    a_row = -jnp.exp(alog_row_ref[...]) * LOG2_E
    a_col = -jnp.exp(alog_col_ref[...]) * LOG2_E
    ri = lax.broadcasted_iota(jnp.int32, (L, L), 0)
    ci = lax.broadcasted_iota(jnp.int32, (L, L), 1)
    causal = ci <= ri
    tril = jnp.where(causal, 1.0, 0.0).astype(BF16)
    triu = jnp.where(ri <= ci, 1.0, 0.0).astype(BF16)
    gs = N_SSD_GROUPS * SSD_STATE
    gw = SSD_WIDTH // N_SSD_GROUPS

    n_pairs = SSD_WIDTH // LANES
    pairs_per_group = n_pairs // N_SSD_GROUPS
    head0 = lax.broadcasted_iota(jnp.int32, (L, LANES), 1) < SSD_HEAD_DIM
    head0_row = head0[:1, :]
    spread = (lax.broadcasted_iota(jnp.int32, (LANES, SSD_WIDTH), 0)
              == lax.broadcasted_iota(jnp.int32, (LANES, SSD_WIDTH), 1) // SSD_HEAD_DIM)
    spread = jnp.where(spread, 1.0, 0.0).astype(BF16)

    def prologue(chunk):
        rows = pl.ds(chunk * L, L)
        act = act_ref[rows, :]
        xs = act[:, :SSD_WIDTH]
        xs_b = xs.astype(BF16)
        bm = act[:, SSD_WIDTH:SSD_WIDTH + gs]
        cm = act[:, SSD_WIDTH + gs:SSD_WIDTH + 2 * gs]
        dt_c = dtc_ref[rows, :]
        dt_r = dtr_ref[:, chunk * L:(chunk + 1) * L]
        acs_c = _split_dot(tril, dt_c * a_row, 3, lhs_split=False)
        acs_r = _split_dot(dt_r * a_col, triu, 3)
        e_wide = _split_dot(jnp.exp2(acs_c), spread, 2)
        pairs = []
        for g in range(N_SSD_GROUPS):
            b_g = bm[:, g * SSD_STATE:(g + 1) * SSD_STATE]
            c_gb = cm[:, g * SSD_STATE:(g + 1) * SSD_STATE].astype(BF16)
            cb = _dot_nt(c_gb, b_g.astype(BF16))
            b_gt = b_g.T
            for q in range(pairs_per_group):
                p = g * pairs_per_group + q
                x_pair = xs_b[:, p * LANES:(p + 1) * LANES]
                zero = jnp.zeros_like(x_pair)
                x_split = jnp.concatenate([jnp.where(head0, x_pair, zero),
                                           jnp.where(head0, zero, x_pair)], axis=0)
                w_diag, w_state, last = [], [], []
                for h in (2 * p, 2 * p + 1):
                    acs_col = acs_c[:, h:h + 1]
                    acs_row = acs_r[h:h + 1, :]
                    dt_row = dt_r[h:h + 1, :]
                    seg = jnp.exp2(jnp.where(causal, acs_col - acs_row, NEG_BIG))
                    w_diag.append((cb * seg * dt_row).astype(BF16))
                    acs_last = acs_row[:, L - 1:L]
                    w_state.append((b_gt * (jnp.exp2(acs_last - acs_row) * dt_row)).astype(BF16))
                    last.append(jnp.exp2(acs_last))
                pairs.append((c_gb,
                              _dot(jnp.concatenate(w_diag, axis=1), x_split),
                              _dot(jnp.concatenate(w_state, axis=1), x_split),
                              jnp.where(head0_row, last[0], last[1])))
        return rows, xs, e_wide, pairs

    states = [state[p] for p in range(n_pairs)]
    n_chunks = act_ref.shape[0] // L
    queue = [prologue(c) for c in range(min(SSD_LOOKAHEAD, n_chunks))]
    for chunk in range(n_chunks):
        if chunk + SSD_LOOKAHEAD < n_chunks:
            queue.append(prologue(chunk + SSD_LOOKAHEAD))
        rows, xs, e_wide, pairs = queue.pop(0)
        ys = []
        for p, (c_gb, y_within, increment, decay) in enumerate(pairs):
            ys.append(y_within + e_wide[:, p * LANES:(p + 1) * LANES]
                      * _dot(c_gb, states[p].astype(BF16)))
            states[p] = decay * states[p] + increment
        y = (jnp.concatenate(ys, axis=1) + dskip_ref[...] * xs) * gate_ref[rows, :]

        outs = []
        for g in range(N_SSD_GROUPS):
            yg = y[:, g * gw:(g + 1) * gw]
            outs.append(yg * lax.rsqrt(jnp.mean(yg * yg, axis=-1, keepdims=True) + EPS))
        y_ref[rows, :] = (jnp.concatenate(outs, axis=1) * nw_ref[...]).astype(y_ref.dtype)

    for p in range(n_pairs):
        state[p] = states[p]


def _ssd(act, gate, dtc, dtr, alog_row, alog_col, dskip, nw):
    b, s, _ = act.shape
    rows = SSD_CHUNKS_PER_STEP * CHUNK
    tok = lambda width: pl.BlockSpec((None, rows, width), lambda i, c: (i, c, 0))
    return pl.pallas_call(
        _ssd_kernel,
        grid=(b, s // rows),
        in_specs=[
            tok(CONV_CHANNELS), tok(SSD_WIDTH), tok(LANES),
            pl.BlockSpec((None, DT_ROWS, rows), lambda i, c: (i, 0, c)),
            _resident(alog_row.shape), _resident(alog_col.shape),
            _resident(dskip.shape), _resident(nw.shape),
        ],
        out_specs=tok(SSD_WIDTH),
        out_shape=jax.ShapeDtypeStruct((b, s, SSD_WIDTH), BF16),
        scratch_shapes=[pltpu.VMEM((SSD_WIDTH // LANES, SSD_STATE, LANES), F32)],
        compiler_params=pltpu.CompilerParams(
            dimension_semantics=("parallel", "arbitrary"), vmem_limit_bytes=VMEM_LIMIT),
        name="ssd",
    )(act, gate, dtc, dtr, alog_row, alog_col, dskip, nw)


def _out_mlp_kernel(x_ref, oa_ref, ys_ref, woa_ref, wos_ref, nw_ref, wup_ref, wdn_ref, o_ref):
    x1 = x_ref[...] + _dot(oa_ref[...], woa_ref[...]) + _dot(ys_ref[...], wos_ref[...])
    hm = x1 * lax.rsqrt(jnp.mean(x1 * x1, axis=-1, keepdims=True) + EPS) * nw_ref[...]
    up = _dot(hm.astype(BF16), wup_ref[...])
    act = jnp.square(jnp.maximum(up, 0.0)).astype(BF16)
    o_ref[...] = x1 + _dot(act, wdn_ref[...])


def _out_mlp(x, o_attn, y_ssd, wo_a, wo_s, mlp_norm_w, w_up, w_down, tm):
    t, d = x.shape
    tok = lambda width: pl.BlockSpec((tm, width), lambda i: (i, 0))
    return pl.pallas_call(
        _out_mlp_kernel,
        grid=(t // tm,),
        in_specs=[
            tok(d), tok(ATTN_WIDTH), tok(SSD_WIDTH),
            _resident(wo_a.shape), _resident(wo_s.shape), _resident((1, d)),
            _resident(w_up.shape), _resident(w_down.shape),
        ],
        out_specs=tok(d),
        out_shape=jax.ShapeDtypeStruct((t, d), F32),
        compiler_params=pltpu.CompilerParams(
            dimension_semantics=("parallel",), vmem_limit_bytes=VMEM_LIMIT),
        name="out_mlp",
    )(x, o_attn, y_ssd, wo_a, wo_s, mlp_norm_w, w_up, w_down)


def _rope_tables(s):
    half = HEAD_DIM // 2
    inv_freq = ROPE_THETA ** (-jnp.arange(half, dtype=F32) / half)
    lane_freq = jnp.tile(inv_freq, LANES // half)
    sign = jnp.tile(jnp.repeat(jnp.array([-1.0, 1.0], F32), half), LANES // HEAD_DIM)
    ang = jnp.arange(s, dtype=F32)[:, None] * lane_freq[None, :]
    return jnp.cos(ang), jnp.sin(ang) * sign[None, :]


def _pad_lanes(row):
    return jnp.pad(row, ((0, 0), (0, LANES - row.shape[1])))


def _layer(x, attn_norm_w, w_in, q_norm_w, k_norm_w, conv_w, conv_b, dt_bias, a_log, d_skip,
           ssd_norm_w, w_out, mlp_norm_w, w_up, w_down):
    b, s, d = x.shape
    w_t = w_in.T
    w_main = w_t[:MAIN_WIDTH].astype(BF16)
    w_dt_c = jnp.pad(w_t[MAIN_WIDTH:], ((0, LANES - N_SSD_HEADS), (0, 0))).astype(BF16)
    head_of = jnp.arange(ATTN_WIDTH, dtype=jnp.int32) // HEAD_DIM
    group_mean = jnp.where(head_of[:, None] == head_of[None, :], 1.0 / HEAD_DIM, 0.0).astype(BF16)
    cos, sin = _rope_tables(s)
    q, k, v, gate, act, dtc, dtr = _in_proj(
        x, attn_norm_w[None, :], w_main, w_dt_c, group_mean,
        jnp.tile(q_norm_w, N_ATTN_HEADS)[None, :], jnp.tile(k_norm_w, N_ATTN_HEADS)[None, :],
        cos, sin, conv_w, conv_b[None, :], _pad_lanes(dt_bias[None, :]), tm=PROJ_TOKENS)

    o_attn = _attention(q, k, v)

    col = lambda p: jnp.pad(p[:, None], ((0, DT_ROWS - N_SSD_HEADS), (0, 0)))
    y_ssd = _ssd(act, gate, dtc, dtr, _pad_lanes(a_log[None, :]), col(a_log),
                 jnp.repeat(d_skip, SSD_HEAD_DIM)[None, :], ssd_norm_w[None, :])

    out = _out_mlp(x.reshape(b * s, d), o_attn.reshape(b * s, ATTN_WIDTH),
                   y_ssd.reshape(b * s, SSD_WIDTH),
                   w_out[:ATTN_WIDTH].astype(BF16), w_out[ATTN_WIDTH:].astype(BF16),
                   mlp_norm_w[None, :], w_up.astype(BF16), w_down.astype(BF16), tm=MLP_TOKENS)
    return out.reshape(b, s, d)


def kernel(x, attn_norm_w, w_in, q_norm_w, k_norm_w, conv_w, conv_b, dt_bias, a_log, d_skip,
           ssd_norm_w, w_out, mlp_norm_w, w_up, w_down):
    for i in range(attn_norm_w.shape[0]):
        x = _layer(x, attn_norm_w[i], w_in[i], q_norm_w[i], k_norm_w[i], conv_w[i], conv_b[i],
                   dt_bias[i], a_log[i], d_skip[i], ssd_norm_w[i], w_out[i], mlp_norm_w[i],
                   w_up[i], w_down[i])
    return x
```

```python
import functools

import jax
import jax.numpy as jnp
from jax import lax
from jax.experimental import pallas as pl
from jax.experimental.pallas import tpu as pltpu

F32 = jnp.float32
BF16 = jnp.bfloat16

HEAD_DIM = 64
N_ATTN_HEADS = 8
ATTN_WIDTH = N_ATTN_HEADS * HEAD_DIM
ROPE_THETA = 10000.0
Q_BLOCK = 128
DILATIONS = (1, 4, 16)
FAR_DIL = 4
FAR_RATIO = 4
ATTN_LOOKAHEAD = 5
ATTN_STEP_ROWS = 1024
SSD_HEAD_DIM = 64
N_SSD_HEADS = 8
SSD_WIDTH = N_SSD_HEADS * SSD_HEAD_DIM
N_SSD_GROUPS = 2
HEADS_PER_GROUP = N_SSD_HEADS // N_SSD_GROUPS
SSD_STATE = 128
CONV_WIDTH = 4
CHUNK = 128
SSD_CHUNKS_PER_STEP = 16
SSD_LOOKAHEAD = 1
CONV_CHANNELS = SSD_WIDTH + 2 * N_SSD_GROUPS * SSD_STATE
CONV_HISTORY = 16
ROPE_ROWS = 512
CONV_GROUP = 256
EPS = 1e-6
NEG_BIG = -1e30
LOG2_E = 1.4426950408889634
SCORE_SCALE = HEAD_DIM ** -0.5 * LOG2_E

LANES = 128
DT_ROWS = 16
MAIN_WIDTH = 3 * ATTN_WIDTH + SSD_WIDTH + CONV_CHANNELS
VMEM_LIMIT = 56 * 1024 * 1024
PROJ_TOKENS = 1024
MLP_TOKENS = 512


def _dot(a, b):
    return jnp.dot(a, b, preferred_element_type=F32)


def _dot_nt(a, b):
    return lax.dot_general(a, b, (((1,), (1,)), ((), ())), preferred_element_type=F32)


def _split_dot(a, b, parts, lhs_split=True):
    src = a if lhs_split else b
    acc = None
    rem = src
    for _ in range(parts):
        piece = rem.astype(BF16)
        rem = rem - piece.astype(F32)
        term = _dot(piece, b) if lhs_split else _dot(a, piece)
        acc = term if acc is None else acc + term
    return acc


def _resident(shape):
    zeros = (0,) * len(shape)
    return pl.BlockSpec(shape, lambda *_: zeros, pipeline_mode=pl.Buffered(1))


def _in_proj_kernel(x_ref, xprev_ref, nw_ref, w_ref, wdt_ref, gm_ref, qw_ref, kw_ref,
                    cos_ref, sin_ref, cw_ref, cb_ref, dtb_ref,
                    q_ref, k_ref, v_ref, gate_ref, act_ref, dtc_ref, dtr_ref, ubuf):
    tm = x_ref.shape[0]

    def normed(x):
        y = x * lax.rsqrt(jnp.mean(x * x, axis=-1, keepdims=True) + EPS) * nw_ref[...]
        return y.astype(BF16)

    hb = normed(x_ref[...])

    lane = lax.broadcasted_iota(jnp.int32, (ROPE_ROWS, ATTN_WIDTH), 1)
    first_half = (lane % HEAD_DIM) < (HEAD_DIM // 2)

    def head_norm_rope(t, w, out_ref, scale):
        for r in range(tm // ROPE_ROWS):
            rows = pl.ds(r * ROPE_ROWS, ROPE_ROWS)
            tr = t[r * ROPE_ROWS:(r + 1) * ROPE_ROWS, :]
            cos = jnp.concatenate([cos_ref[rows, :]] * (ATTN_WIDTH // LANES), axis=1)
            sin = jnp.concatenate([sin_ref[rows, :]] * (ATTN_WIDTH // LANES), axis=1)
            ms = _dot((tr * tr).astype(BF16), gm_ref[...])
            n = tr * lax.rsqrt(ms + EPS) * w
            partner = jnp.where(first_half,
                                pltpu.roll(n, ATTN_WIDTH - HEAD_DIM // 2, 1),
                                pltpu.roll(n, HEAD_DIM // 2, 1))
            out_ref[rows, :] = ((n * cos + partner * sin) * scale).astype(BF16)

    o = 0
    q = _dot_nt(hb, w_ref[o:o + ATTN_WIDTH, :]); o += ATTN_WIDTH
    head_norm_rope(q, qw_ref[...], q_ref, SCORE_SCALE)
    k = _dot_nt(hb, w_ref[o:o + ATTN_WIDTH, :]); o += ATTN_WIDTH
    head_norm_rope(k, kw_ref[...], k_ref, 1.0)
    v_ref[...] = _dot_nt(hb, w_ref[o:o + ATTN_WIDTH, :]).astype(BF16); o += ATTN_WIDTH
    gate_ref[...] = _silu(_dot_nt(hb, w_ref[o:o + SSD_WIDTH, :])); o += SSD_WIDTH

    hb_prev = normed(xprev_ref[...])
    for c in range(CONV_CHANNELS // CONV_GROUP):
        lanes = pl.ds(c * CONV_GROUP, CONV_GROUP)
        w_c = w_ref[o + c * CONV_GROUP:o + (c + 1) * CONV_GROUP, :]
        u = _dot_nt(hb, w_c)
        hist = _dot_nt(hb_prev, w_c)
        ubuf[0:CONV_HISTORY, lanes] = jnp.where(pl.program_id(0) == 0, 0.0, hist)
        ubuf[CONV_HISTORY:CONV_HISTORY + tm, lanes] = u
        conv = cb_ref[:, lanes] + cw_ref[CONV_WIDTH - 1:CONV_WIDTH, lanes] * u
        for tap in range(CONV_WIDTH - 1):
            first = CONV_HISTORY - (CONV_WIDTH - 1 - tap)
            conv = conv + cw_ref[tap:tap + 1, lanes] * ubuf[first:first + tm, lanes]
        act_ref[:, lanes] = _silu(conv)

    dtc = _softplus(_dot_nt(hb, wdt_ref[...]) + dtb_ref[...])
    dtc_ref[...] = dtc
    dtr_ref[...] = dtc.T[:DT_ROWS, :]


def _in_proj(x, attn_norm_w, w_main, w_dt, group_mean, qw, kw, cos, sin, conv_w, conv_b, dt_bias,
             tm):
    b, s, d = x.shape
    grid = (s // tm, b)
    tok = lambda width: pl.BlockSpec((None, tm, width), lambda j, i: (i, j, 0))
    prev = pl.BlockSpec((None, CONV_HISTORY, d),
                        lambda j, i: (i, jnp.maximum(j * (tm // CONV_HISTORY) - 1, 0), 0))
    out_shapes = (
        jax.ShapeDtypeStruct((b, s, ATTN_WIDTH), BF16),
        jax.ShapeDtypeStruct((b, s, ATTN_WIDTH), BF16),
        jax.ShapeDtypeStruct((b, s, ATTN_WIDTH), BF16),
        jax.ShapeDtypeStruct((b, s, SSD_WIDTH), F32),
        jax.ShapeDtypeStruct((b, s, CONV_CHANNELS), F32),
        jax.ShapeDtypeStruct((b, s, LANES), F32),
        jax.ShapeDtypeStruct((b, DT_ROWS, s), F32),
    )
    return pl.pallas_call(
        _in_proj_kernel,
        grid=grid,
        in_specs=[
            tok(d),
            prev,
            _resident((1, d)),
            _resident(w_main.shape),
            _resident(w_dt.shape),
            _resident(group_mean.shape),
            _resident((1, ATTN_WIDTH)),
            _resident((1, ATTN_WIDTH)),
            pl.BlockSpec((tm, LANES), lambda j, i: (j, 0)),
            pl.BlockSpec((tm, LANES), lambda j, i: (j, 0)),
            _resident(conv_w.shape), _resident(conv_b.shape), _resident(dt_bias.shape),
        ],
        out_specs=(
            tok(ATTN_WIDTH), tok(ATTN_WIDTH), tok(ATTN_WIDTH), tok(SSD_WIDTH),
            tok(CONV_CHANNELS), tok(LANES),
            pl.BlockSpec((None, DT_ROWS, tm), lambda j, i: (i, 0, j)),
        ),
        out_shape=out_shapes,
        scratch_shapes=[pltpu.VMEM((CONV_HISTORY + tm, CONV_CHANNELS), F32)],
        compiler_params=pltpu.CompilerParams(
            dimension_semantics=("parallel", "parallel"), vmem_limit_bytes=VMEM_LIMIT),
        name="in_proj",
    )(x, x, attn_norm_w, w_main, w_dt, group_mean, qw, kw, cos, sin, conv_w, conv_b, dt_bias)


def _attn_blocks(blocks):
    qb = Q_BLOCK
    n_pairs = ATTN_WIDTH // LANES
    head0 = lax.broadcasted_iota(jnp.int32, (qb, LANES), 1) < HEAD_DIM
    items = [(i, pair) for i in range(len(blocks)) for pair in range(n_pairs)]

    def scores(item):
        i, pair = item
        load_q, load_k, _, load_bias, _ = blocks[i]
        qt, kb, bias = load_q(pair), load_k(pair), load_bias()
        out = []
        for h in range(2):
            mine = head0 if h == 0 else jnp.logical_not(head0)
            qh = jnp.where(mine, qt, jnp.zeros_like(qt))
            out.append(_dot_nt(qh, kb) + bias)
        return out

    queue = [scores(item) for item in items[:ATTN_LOOKAHEAD]]
    parts = []
    for n, (i, pair) in enumerate(items):
        if n + ATTN_LOOKAHEAD < len(items):
            queue.append(scores(items[n + ATTN_LOOKAHEAD]))
        pending = queue.pop(0)
        vb = blocks[i][2](pair)
        v_ext = jnp.concatenate([vb, jnp.ones_like(vb)], axis=1)
        heads = []
        for sc in pending:
            m = jnp.max(sc, axis=1, keepdims=True)
            r = _dot(jnp.exp2(sc - m).astype(BF16), v_ext)
            heads.append((r[:, :LANES], r[:, LANES:], jnp.broadcast_to(m, (qb, LANES))))
        parts.append([jnp.where(head0, a, b) for a, b in zip(*heads)])
        if pair == n_pairs - 1:
            blocks[i][4](*[jnp.concatenate(t, axis=1) for t in zip(*parts)])
            parts = []


def _near_bias(first):
    qb = Q_BLOCK
    row = lax.broadcasted_iota(jnp.int32, (qb, 2 * qb), 0)
    col = lax.broadcasted_iota(jnp.int32, (qb, 2 * qb), 1)
    ok = (col <= row) if first else ((col >= row) & (col <= row + qb))
    return jnp.where(ok, 0.0, NEG_BIG)


def _far_bias(blocks_back):
    qb = Q_BLOCK
    row = lax.broadcasted_iota(jnp.int32, (qb, qb), 0)
    col = lax.broadcasted_iota(jnp.int32, (qb, qb), 1)
    dist = row - col + blocks_back * qb
    in_window = (dist >= 0) & (dist <= qb)
    on_stride = (dist >= 0) & ((dist & (FAR_RATIO - 1)) == 0)
    return jnp.where(in_window & on_stride, 1.0,
                     jnp.where(in_window | on_stride, 0.0, NEG_BIG))


def _attn_kernel(q_ref, k_ref, v_ref, o_ref,
                 tmp, q_cls, k_cls, v_cls, num_far, den_far, top_far, near_bias, far_bias):
    qb = Q_BLOCK
    step = pl.program_id(1)
    rows = q_ref.shape[0]
    s = k_ref.shape[0]
    n_steps = s // rows
    n_pairs = ATTN_WIDTH // LANES
    lanes = lambda pair: pl.ds(pair * LANES, LANES)
    far = (num_far, den_far, top_far)

    per_class = rows // FAR_DIL

    def to_class_order(src_rows, dst, at):
        for p in range(n_pairs):
            tmp[p] = src_rows[:, p * LANES:(p + 1) * LANES].astype(F32)
            for c in range(FAR_DIL):
                dst[c, at:at + per_class, lanes(p)] = (
                    tmp[p, pl.ds(c, per_class, stride=FAR_DIL), :].astype(BF16))

    @pl.when(step == 0)
    def _():
        for i in range(n_steps):
            to_class_order(k_ref[i * rows:(i + 1) * rows, :], k_cls, i * per_class)
            to_class_order(v_ref[i * rows:(i + 1) * rows, :], v_cls, i * per_class)

    to_class_order(q_ref[...], q_cls, 0)
    near_bias[0] = _near_bias(False)
    near_bias[1] = _near_bias(True)
    for d in range(3):
        far_bias[d] = _far_bias(d)

    def far_block(c, local, n_kb):
        token_rows = pl.ds(c + local * FAR_DIL * qb, qb, stride=FAR_DIL)

        def store(*stats):
            for dst, t in zip(far, stats):
                for p in range(n_pairs):
                    dst[p, token_rows, :] = t[:, p * LANES:(p + 1) * LANES]

        load_bias = lambda: jnp.concatenate(
            [far_bias[min(n_kb - 1 - kb, 2)] for kb in range(n_kb)], axis=1)
        return (lambda p: q_cls[c, local * qb:(local + 1) * qb, lanes(p)],
                lambda p: k_cls[c, 0:n_kb * qb, lanes(p)],
                lambda p: v_cls[c, 0:n_kb * qb, lanes(p)], load_bias, store)

    is_first = (step == 0).astype(jnp.int32)

    def near_block(j):
        q_rows = pl.ds(j * qb, qb)
        if j == 0:
            start = pl.multiple_of((step * (rows // qb) - 1 + is_first) * qb, qb)
            load_bias = lambda: near_bias[is_first]
        else:
            start = pl.multiple_of((step * (rows // qb) + j - 1) * qb, qb)
            load_bias = lambda: near_bias[0]
        kv_rows = pl.ds(start, 2 * qb)

        def store(num, den, top):
            num_f, den_f, top_f = [
                jnp.concatenate([t[p, q_rows, :] for p in range(n_pairs)], axis=1) for t in far]
            both = jnp.maximum(top, top_f)
            w_near, w_far = jnp.exp2(top - both), jnp.exp2(top_f - both)
            o_ref[q_rows, :] = ((w_near * num + w_far * num_f)
                                / (w_near * den + w_far * den_f)).astype(o_ref.dtype)

        return (lambda p: q_ref[q_rows, lanes(p)], lambda p: k_ref[kv_rows, lanes(p)],
                lambda p: v_ref[kv_rows, lanes(p)], load_bias, store)

    locals_per_step = per_class // qb
    for g in range(n_steps):
        @pl.when(step == g)
        def _(g=g):
            blocks = []
            for local in range(locals_per_step):
                n_kb = g * locals_per_step + local + 1
                blocks += [far_block(c, local, n_kb) for c in range(FAR_DIL)]
                blocks += [near_block(local * FAR_DIL + j) for j in range(FAR_DIL)]
            _attn_blocks(blocks)


def _attention(q, k, v):
    b, s, w = q.shape
    assert DILATIONS == (1, FAR_DIL, FAR_DIL * FAR_RATIO) and s == Q_BLOCK * DILATIONS[-1]
    rows = ATTN_STEP_ROWS
    q_blk = pl.BlockSpec((None, rows, w), lambda i, g: (i, g, 0))
    kv_blk = pl.BlockSpec((None, s, w), lambda i, g: (i, 0, 0))
    return pl.pallas_call(
        _attn_kernel,
        grid=(b, s // rows),
        in_specs=[q_blk, kv_blk, kv_blk],
        out_specs=q_blk,
        out_shape=jax.ShapeDtypeStruct((b, s, w), BF16),
        scratch_shapes=[
            pltpu.VMEM((w // LANES, rows, LANES), F32),
            pltpu.VMEM((FAR_DIL, rows // FAR_DIL, w), BF16),
            pltpu.VMEM((FAR_DIL, s // FAR_DIL, w), BF16),
            pltpu.VMEM((FAR_DIL, s // FAR_DIL, w), BF16),
            pltpu.VMEM((w // LANES, rows, LANES), F32),
            pltpu.VMEM((w // LANES, rows, LANES), F32),
            pltpu.VMEM((w // LANES, rows, LANES), F32),
            pltpu.VMEM((2, Q_BLOCK, 2 * Q_BLOCK), F32),
            pltpu.VMEM((3, Q_BLOCK, Q_BLOCK), F32),
        ],
        compiler_params=pltpu.CompilerParams(
            dimension_semantics=("parallel", "arbitrary"), vmem_limit_bytes=VMEM_LIMIT),
        name="dilated_attn",
    )(q, k, v)


def _softplus(x):
    return jnp.maximum(x, 0.0) + jnp.log1p(jnp.exp(-jnp.abs(x)))


def _silu(x):
    h = 0.5 * x
    return h + h * jnp.tanh(h)


def _ssd_kernel(act_ref, gate_ref, dtc_ref, dtr_ref, alog_row_ref, alog_col_ref, dskip_ref,
                nw_ref, y_ref, state):
    L = CHUNK

    @pl.when(pl.program_id(1) == 0)
    def _():
        state[...] = jnp.zeros(state.shape, F32)

    a_row = -jnp.exp(alog_row_ref[...]) * LOG2_E
    a_col = -jnp.exp(alog_col_ref[...]) * LOG2_E
    ri = lax.broadcasted_iota(jnp.int32, (L, L), 0)
    ci = lax.broadcasted_iota(jnp.int32, (L, L), 1)
    causal = ci <= ri
    tril = jnp.where(causal, 1.0, 0.0).astype(BF16)
    triu = jnp.where(ri <= ci, 1.0, 0.0).astype(BF16)
    gs = N_SSD_GROUPS * SSD_STATE
    gw = SSD_WIDTH // N_SSD_GROUPS

    n_pairs = SSD_WIDTH // LANES
    pairs_per_group = n_pairs // N_SSD_GROUPS
    head0 = lax.broadcasted_iota(jnp.int32, (L, LANES), 1) < SSD_HEAD_DIM
    head0_row = head0[:1, :]
    spread = (lax.broadcasted_iota(jnp.int32, (LANES, SSD_WIDTH), 0)
              == lax.broadcasted_iota(jnp.int32, (LANES, SSD_WIDTH), 1) // SSD_HEAD_DIM)
    spread = jnp.where(spread, 1.0, 0.0).astype(BF16)

    def prologue(chunk):
        rows = pl.ds(chunk * L, L)
        act = act_ref[rows, :]
        xs = act[:, :SSD_WIDTH]
        xs_b = xs.astype(BF16)
        bm = act[:, SSD_WIDTH:SSD_WIDTH + gs]
        cm = act[:, SSD_WIDTH + gs:SSD_WIDTH + 2 * gs]
        dt_c = dtc_ref[rows, :]
        dt_r = dtr_ref[:, chunk * L:(chunk + 1) * L]
        acs_c = _split_dot(tril, dt_c * a_row, 3, lhs_split=False)
        acs_r = _split_dot(dt_r * a_col, triu, 3)
        e_wide = _split_dot(jnp.exp2(acs_c), spread, 2)
        pairs = []
        for g in range(N_SSD_GROUPS):
            b_g = bm[:, g * SSD_STATE:(g + 1) * SSD_STATE]
            c_gb = cm[:, g * SSD_STATE:(g + 1) * SSD_STATE].astype(BF16)
            cb = _dot_nt(c_gb, b_g.astype(BF16))
            b_gt = b_g.T
            for q in range(pairs_per_group):
                p = g * pairs_per_group + q
                x_pair = xs_b[:, p * LANES:(p + 1) * LANES]
                zero = jnp.zeros_like(x_pair)
                x_split = jnp.concatenate([jnp.where(head0, x_pair, zero),
                                           jnp.where(head0, zero, x_pair)], axis=0)
                w_diag, w_state, last = [], [], []
                for h in (2 * p, 2 * p + 1):
                    acs_col = acs_c[:, h:h + 1]
                    acs_row = acs_r[h:h + 1, :]
                    dt_row = dt_r[h:h + 1, :]
                    seg = jnp.exp2(jnp.where(causal, acs_col - acs_row, NEG_BIG))
                    w_diag.append((cb * seg * dt_row).astype(BF16))
                    acs_last = acs_row[:, L - 1:L]
                    w_state.append((b_gt * (jnp.exp2(acs_last - acs_row) * dt_row)).astype(BF16))
                    last.append(jnp.exp2(acs_last))
                pairs.append((c_gb,
                              _dot(jnp.concatenate(w_diag, axis=1), x_split),
                              _dot(jnp.concatenate(w_state, axis=1), x_split),
                              jnp.where(head0_row, last[0], last[1])))
        return rows, xs, e_wide, pairs

    states = [state[p] for p in range(n_pairs)]
    n_chunks = act_ref.shape[0] // L
    queue = [prologue(c) for c in range(min(SSD_LOOKAHEAD, n_chunks))]
    for chunk in range(n_chunks):
        if chunk + SSD_LOOKAHEAD < n_chunks:
            queue.append(prologue(chunk + SSD_LOOKAHEAD))
        rows, xs, e_wide, pairs = queue.pop(0)
        ys = []
        for p, (c_gb, y_within, increment, decay) in enumerate(pairs):
            ys.append(y_within + e_wide[:, p * LANES:(p + 1) * LANES]
                      * _dot(c_gb, states[p].astype(BF16)))
            states[p] = decay * states[p] + increment
        y = (jnp.concatenate(ys, axis=1) + dskip_ref[...] * xs) * gate_ref[rows, :]

        outs = []
        for g in range(N_SSD_GROUPS):
            yg = y[:, g * gw:(g + 1) * gw]
            outs.append(yg * lax.rsqrt(jnp.mean(yg * yg, axis=-1, keepdims=True) + EPS))
        y_ref[rows, :] = (jnp.concatenate(outs, axis=1) * nw_ref[...]).astype(y_ref.dtype)

    for p in range(n_pairs):
        state[p] = states[p]


def _ssd(act, gate, dtc, dtr, alog_row, alog_col, dskip, nw):
    b, s, _ = act.shape
    rows = SSD_CHUNKS_PER_STEP * CHUNK
    tok = lambda width: pl.BlockSpec((None, rows, width), lambda i, c: (i, c, 0))
    return pl.pallas_call(
        _ssd_kernel,
        grid=(b, s // rows),
        in_specs=[
            tok(CONV_CHANNELS), tok(SSD_WIDTH), tok(LANES),
            pl.BlockSpec((None, DT_ROWS, rows), lambda i, c: (i, 0, c)),
            _resident(alog_row.shape), _resident(alog_col.shape),
            _resident(dskip.shape), _resident(nw.shape),
        ],
        out_specs=tok(SSD_WIDTH),
        out_shape=jax.ShapeDtypeStruct((b, s, SSD_WIDTH), BF16),
        scratch_shapes=[pltpu.VMEM((SSD_WIDTH // LANES, SSD_STATE, LANES), F32)],
        compiler_params=pltpu.CompilerParams(
            dimension_semantics=("parallel", "arbitrary"), vmem_limit_bytes=VMEM_LIMIT),
        name="ssd",
    )(act, gate, dtc, dtr, alog_row, alog_col, dskip, nw)


def _out_mlp_kernel(x_ref, oa_ref, ys_ref, woa_ref, wos_ref, nw_ref, wup_ref, wdn_ref, o_ref):
    x1 = x_ref[...] + _dot(oa_ref[...], woa_ref[...]) + _dot(ys_ref[...], wos_ref[...])
    hm = x1 * lax.rsqrt(jnp.mean(x1 * x1, axis=-1, keepdims=True) + EPS) * nw_ref[...]
    up = _dot(hm.astype(BF16), wup_ref[...])
    act = jnp.square(jnp.maximum(up, 0.0)).astype(BF16)
    o_ref[...] = x1 + _dot(act, wdn_ref[...])


def _out_mlp(x, o_attn, y_ssd, wo_a, wo_s, mlp_norm_w, w_up, w_down, tm):
    t, d = x.shape
    tok = lambda width: pl.BlockSpec((tm, width), lambda i: (i, 0))
    return pl.pallas_call(
        _out_mlp_kernel,
        grid=(t // tm,),
        in_specs=[
            tok(d), tok(ATTN_WIDTH), tok(SSD_WIDTH),
            _resident(wo_a.shape), _resident(wo_s.shape), _resident((1, d)),
            _resident(w_up.shape), _resident(w_down.shape),
        ],
        out_specs=tok(d),
        out_shape=jax.ShapeDtypeStruct((t, d), F32),
        compiler_params=pltpu.CompilerParams(
            dimension_semantics=("parallel",), vmem_limit_bytes=VMEM_LIMIT),
        name="out_mlp",
    )(x, o_attn, y_ssd, wo_a, wo_s, mlp_norm_w, w_up, w_down)


def _rope_tables(s):
    half = HEAD_DIM // 2
    inv_freq = ROPE_THETA ** (-jnp.arange(half, dtype=F32) / half)
    lane_freq = jnp.tile(inv_freq, LANES // half)
    sign = jnp.tile(jnp.repeat(jnp.array([-1.0, 1.0], F32), half), LANES // HEAD_DIM)
    ang = jnp.arange(s, dtype=F32)[:, None] * lane_freq[None, :]
    return jnp.cos(ang), jnp.sin(ang) * sign[None, :]


def _pad_lanes(row):
    return jnp.pad(row, ((0, 0), (0, LANES - row.shape[1])))


def _layer(x, attn_norm_w, w_in, q_norm_w, k_norm_w, conv_w, conv_b, dt_bias, a_log, d_skip,
           ssd_norm_w, w_out, mlp_norm_w, w_up, w_down):
    b, s, d = x.shape
    w_t = w_in.T
    w_main = w_t[:MAIN_WIDTH].astype(BF16)
    w_dt_c = jnp.pad(w_t[MAIN_WIDTH:], ((0, LANES - N_SSD_HEADS), (0, 0))).astype(BF16)
    head_of = jnp.arange(ATTN_WIDTH, dtype=jnp.int32) // HEAD_DIM
    group_mean = jnp.where(head_of[:, None] == head_of[None, :], 1.0 / HEAD_DIM, 0.0).astype(BF16)
    cos, sin = _rope_tables(s)
    q, k, v, gate, act, dtc, dtr = _in_proj(
        x, attn_norm_w[None, :], w_main, w_dt_c, group_mean,
        jnp.tile(q_norm_w, N_ATTN_HEADS)[None, :], jnp.tile(k_norm_w, N_ATTN_HEADS)[None, :],
        cos, sin, conv_w, conv_b[None, :], _pad_lanes(dt_bias[None, :]), tm=PROJ_TOKENS)

    o_attn = _attention(q, k, v)

    col = lambda p: jnp.pad(p[:, None], ((0, DT_ROWS - N_SSD_HEADS), (0, 0)))
    y_ssd = _ssd(act, gate, dtc, dtr, _pad_lanes(a_log[None, :]), col(a_log),
                 jnp.repeat(d_skip, SSD_HEAD_DIM)[None, :], ssd_norm_w[None, :])

    out = _out_mlp(x.reshape(b * s, d), o_attn.reshape(b * s, ATTN_WIDTH),
                   y_ssd.reshape(b * s, SSD_WIDTH),
                   w_out[:ATTN_WIDTH].astype(BF16), w_out[ATTN_WIDTH:].astype(BF16),
                   mlp_norm_w[None, :], w_up.astype(BF16), w_down.astype(BF16), tm=MLP_TOKENS)
    return out.reshape(b, s, d)


def kernel(x, attn_norm_w, w_in, q_norm_w, k_norm_w, conv_w, conv_b, dt_bias, a_log, d_skip,
           ssd_norm_w, w_out, mlp_norm_w, w_up, w_down):
    for i in range(attn_norm_w.shape[0]):
        x = _layer(x, attn_norm_w[i], w_in[i], q_norm_w[i], k_norm_w[i], conv_w[i], conv_b[i],
                   dt_bias[i], a_log[i], d_skip[i], ssd_norm_w[i], w_out[i], mlp_norm_w[i],
                   w_up[i], w_down[i])
    return x
```

```python
import functools

import jax
import jax.numpy as jnp
from jax import lax
from jax.experimental import pallas as pl
from jax.experimental.pallas import tpu as pltpu

F32 = jnp.float32
BF16 = jnp.bfloat16

HEAD_DIM = 64
N_ATTN_HEADS = 8
ATTN_WIDTH = N_ATTN_HEADS * HEAD_DIM
ROPE_THETA = 10000.0
Q_BLOCK = 128
DILATIONS = (1, 4, 16)
FAR_DIL = 4
FAR_RATIO = 4
ATTN_LOOKAHEAD = 5
ATTN_STEP_ROWS = 2048
SSD_HEAD_DIM = 64
N_SSD_HEADS = 8
SSD_WIDTH = N_SSD_HEADS * SSD_HEAD_DIM
N_SSD_GROUPS = 2
HEADS_PER_GROUP = N_SSD_HEADS // N_SSD_GROUPS
SSD_STATE = 128
CONV_WIDTH = 4
CHUNK = 128
SSD_CHUNKS_PER_STEP = 16
SSD_LOOKAHEAD = 1
CONV_CHANNELS = SSD_WIDTH + 2 * N_SSD_GROUPS * SSD_STATE
CONV_HISTORY = 16
ROPE_ROWS = 512
CONV_GROUP = 256
EPS = 1e-6
NEG_BIG = -1e30
LOG2_E = 1.4426950408889634
SCORE_SCALE = HEAD_DIM ** -0.5 * LOG2_E

LANES = 128
DT_ROWS = 16
MAIN_WIDTH = 3 * ATTN_WIDTH + SSD_WIDTH + CONV_CHANNELS
VMEM_LIMIT = 56 * 1024 * 1024
PROJ_TOKENS = 1024
MLP_TOKENS = 512


def _dot(a, b):
    return jnp.dot(a, b, preferred_element_type=F32)


def _dot_nt(a, b):
    return lax.dot_general(a, b, (((1,), (1,)), ((), ())), preferred_element_type=F32)


def _split_dot(a, b, parts, lhs_split=True):
    src = a if lhs_split else b
    acc = None
    rem = src
    for _ in range(parts):
        piece = rem.astype(BF16)
        rem = rem - piece.astype(F32)
        term = _dot(piece, b) if lhs_split else _dot(a, piece)
        acc = term if acc is None else acc + term
    return acc


def _resident(shape):
    zeros = (0,) * len(shape)
    return pl.BlockSpec(shape, lambda *_: zeros, pipeline_mode=pl.Buffered(1))


def _in_proj_kernel(x_ref, xprev_ref, nw_ref, w_ref, wdt_ref, gm_ref, qw_ref, kw_ref,
                    cos_ref, sin_ref, cw_ref, cb_ref, dtb_ref,
                    q_ref, k_ref, v_ref, gate_ref, act_ref, dtc_ref, dtr_ref, ubuf):
    tm = x_ref.shape[0]

    def normed(x):
        y = x * lax.rsqrt(jnp.mean(x * x, axis=-1, keepdims=True) + EPS) * nw_ref[...]
        return y.astype(BF16)

    hb = normed(x_ref[...])

    lane = lax.broadcasted_iota(jnp.int32, (ROPE_ROWS, ATTN_WIDTH), 1)
    first_half = (lane % HEAD_DIM) < (HEAD_DIM // 2)

    def head_norm_rope(t, w, out_ref, scale):
        for r in range(tm // ROPE_ROWS):
            rows = pl.ds(r * ROPE_ROWS, ROPE_ROWS)
            tr = t[r * ROPE_ROWS:(r + 1) * ROPE_ROWS, :]
            cos = jnp.concatenate([cos_ref[rows, :]] * (ATTN_WIDTH // LANES), axis=1)
            sin = jnp.concatenate([sin_ref[rows, :]] * (ATTN_WIDTH // LANES), axis=1)
            ms = _dot((tr * tr).astype(BF16), gm_ref[...])
            n = tr * lax.rsqrt(ms + EPS) * w
            partner = jnp.where(first_half,
                                pltpu.roll(n, ATTN_WIDTH - HEAD_DIM // 2, 1),
                                pltpu.roll(n, HEAD_DIM // 2, 1))
            out_ref[rows, :] = ((n * cos + partner * sin) * scale).astype(BF16)

    o = 0
    q = _dot_nt(hb, w_ref[o:o + ATTN_WIDTH, :]); o += ATTN_WIDTH
    head_norm_rope(q, qw_ref[...], q_ref, SCORE_SCALE)
    k = _dot_nt(hb, w_ref[o:o + ATTN_WIDTH, :]); o += ATTN_WIDTH
    head_norm_rope(k, kw_ref[...], k_ref, 1.0)
    v_ref[...] = _dot_nt(hb, w_ref[o:o + ATTN_WIDTH, :]).astype(BF16); o += ATTN_WIDTH
    gate_ref[...] = _silu(_dot_nt(hb, w_ref[o:o + SSD_WIDTH, :])); o += SSD_WIDTH

    hb_prev = normed(xprev_ref[...])
    for c in range(CONV_CHANNELS // CONV_GROUP):
        lanes = pl.ds(c * CONV_GROUP, CONV_GROUP)
        w_c = w_ref[o + c * CONV_GROUP:o + (c + 1) * CONV_GROUP, :]
        u = _dot_nt(hb, w_c)
        hist = _dot_nt(hb_prev, w_c)
        ubuf[0:CONV_HISTORY, lanes] = jnp.where(pl.program_id(0) == 0, 0.0, hist)
        ubuf[CONV_HISTORY:CONV_HISTORY + tm, lanes] = u
        conv = cb_ref[:, lanes] + cw_ref[CONV_WIDTH - 1:CONV_WIDTH, lanes] * u
        for tap in range(CONV_WIDTH - 1):
            first = CONV_HISTORY - (CONV_WIDTH - 1 - tap)
            conv = conv + cw_ref[tap:tap + 1, lanes] * ubuf[first:first + tm, lanes]
        act_ref[:, lanes] = _silu(conv)

    dtc = _softplus(_dot_nt(hb, wdt_ref[...]) + dtb_ref[...])
    dtc_ref[...] = dtc
    dtr_ref[...] = dtc.T[:DT_ROWS, :]


def _in_proj(x, attn_norm_w, w_main, w_dt, group_mean, qw, kw, cos, sin, conv_w, conv_b, dt_bias,
             tm):
    b, s, d = x.shape
    grid = (s // tm, b)
    tok = lambda width: pl.BlockSpec((None, tm, width), lambda j, i: (i, j, 0))
    prev = pl.BlockSpec((None, CONV_HISTORY, d),
                        lambda j, i: (i, jnp.maximum(j * (tm // CONV_HISTORY) - 1, 0), 0))
    out_shapes = (
        jax.ShapeDtypeStruct((b, s, ATTN_WIDTH), BF16),
        jax.ShapeDtypeStruct((b, s, ATTN_WIDTH), BF16),
        jax.ShapeDtypeStruct((b, s, ATTN_WIDTH), BF16),
        jax.ShapeDtypeStruct((b, s, SSD_WIDTH), F32),
        jax.ShapeDtypeStruct((b, s, CONV_CHANNELS), F32),
        jax.ShapeDtypeStruct((b, s, LANES), F32),
        jax.ShapeDtypeStruct((b, DT_ROWS, s), F32),
    )
    return pl.pallas_call(
        _in_proj_kernel,
        grid=grid,
        in_specs=[
            tok(d),
            prev,
            _resident((1, d)),
            _resident(w_main.shape),
            _resident(w_dt.shape),
            _resident(group_mean.shape),
            _resident((1, ATTN_WIDTH)),
            _resident((1, ATTN_WIDTH)),
            pl.BlockSpec((tm, LANES), lambda j, i: (j, 0)),
            pl.BlockSpec((tm, LANES), lambda j, i: (j, 0)),
            _resident(conv_w.shape), _resident(conv_b.shape), _resident(dt_bias.shape),
        ],
        out_specs=(
            tok(ATTN_WIDTH), tok(ATTN_WIDTH), tok(ATTN_WIDTH), tok(SSD_WIDTH),
            tok(CONV_CHANNELS), tok(LANES),
            pl.BlockSpec((None, DT_ROWS, tm), lambda j, i: (i, 0, j)),
        ),
        out_shape=out_shapes,
        scratch_shapes=[pltpu.VMEM((CONV_HISTORY + tm, CONV_CHANNELS), F32)],
        compiler_params=pltpu.CompilerParams(
            dimension_semantics=("parallel", "parallel"), vmem_limit_bytes=VMEM_LIMIT),
        name="in_proj",
    )(x, x, attn_norm_w, w_main, w_dt, group_mean, qw, kw, cos, sin, conv_w, conv_b, dt_bias)


def _attn_blocks(blocks):
    qb = Q_BLOCK
    n_pairs = ATTN_WIDTH // LANES
    head0 = lax.broadcasted_iota(jnp.int32, (qb, LANES), 1) < HEAD_DIM
    items = [(i, pair) for i in range(len(blocks)) for pair in range(n_pairs)]

    def scores(item):
        i, pair = item
        load_q, load_k, _, load_bias, _ = blocks[i]
        qt, kb, bias = load_q(pair), load_k(pair), load_bias()
        out = []
        for h in range(2):
            mine = head0 if h == 0 else jnp.logical_not(head0)
            qh = jnp.where(mine, qt, jnp.zeros_like(qt))
            out.append(_dot_nt(qh, kb) + bias)
        return out

    queue = [scores(item) for item in items[:ATTN_LOOKAHEAD]]
    parts = []
    for n, (i, pair) in enumerate(items):
        if n + ATTN_LOOKAHEAD < len(items):
            queue.append(scores(items[n + ATTN_LOOKAHEAD]))
        pending = queue.pop(0)
        vb = blocks[i][2](pair)
        v_ext = jnp.concatenate([vb, jnp.ones_like(vb)], axis=1)
        heads = []
        for sc in pending:
            m = jnp.max(sc, axis=1, keepdims=True)
            r = _dot(jnp.exp2(sc - m).astype(BF16), v_ext)
            heads.append((r[:, :LANES], r[:, LANES:], jnp.broadcast_to(m, (qb, LANES))))
        parts.append([jnp.where(head0, a, b) for a, b in zip(*heads)])
        if pair == n_pairs - 1:
            blocks[i][4](*[jnp.concatenate(t, axis=1) for t in zip(*parts)])
            parts = []


def _near_bias(first):
    qb = Q_BLOCK
    row = lax.broadcasted_iota(jnp.int32, (qb, 2 * qb), 0)
    col = lax.broadcasted_iota(jnp.int32, (qb, 2 * qb), 1)
    ok = (col <= row) if first else ((col >= row) & (col <= row + qb))
    return jnp.where(ok, 0.0, NEG_BIG)


def _far_bias(blocks_back):
    qb = Q_BLOCK
    row = lax.broadcasted_iota(jnp.int32, (qb, qb), 0)
    col = lax.broadcasted_iota(jnp.int32, (qb, qb), 1)
    dist = row - col + blocks_back * qb
    in_window = (dist >= 0) & (dist <= qb)
    on_stride = (dist >= 0) & ((dist & (FAR_RATIO - 1)) == 0)
    return jnp.where(in_window & on_stride, 1.0,
                     jnp.where(in_window | on_stride, 0.0, NEG_BIG))


def _attn_kernel(q_ref, k_ref, v_ref, o_ref,
                 tmp, q_cls, k_cls, v_cls, num_far, den_far, top_far, near_bias, far_bias):
    qb = Q_BLOCK
    step = pl.program_id(1)
    rows = q_ref.shape[0]
    s = k_ref.shape[0]
    n_steps = s // rows
    n_pairs = ATTN_WIDTH // LANES
    lanes = lambda pair: pl.ds(pair * LANES, LANES)
    far = (num_far, den_far, top_far)

    per_class = rows // FAR_DIL

    def to_class_order(src_rows, dst, at):
        for p in range(n_pairs):
            tmp[p] = src_rows[:, p * LANES:(p + 1) * LANES].astype(F32)
            for c in range(FAR_DIL):
                dst[c, at:at + per_class, lanes(p)] = (
                    tmp[p, pl.ds(c, per_class, stride=FAR_DIL), :].astype(BF16))

    @pl.when(step == 0)
    def _():
        for i in range(n_steps):
            to_class_order(k_ref[i * rows:(i + 1) * rows, :], k_cls, i * per_class)
            to_class_order(v_ref[i * rows:(i + 1) * rows, :], v_cls, i * per_class)

    to_class_order(q_ref[...], q_cls, 0)
    near_bias[0] = _near_bias(False)
    near_bias[1] = _near_bias(True)
    for d in range(3):
        far_bias[d] = _far_bias(d)

    def far_block(c, local, n_kb):
        token_rows = pl.ds(c + local * FAR_DIL * qb, qb, stride=FAR_DIL)

        def store(*stats):
            for dst, t in zip(far, stats):
                for p in range(n_pairs):
                    dst[p, token_rows, :] = t[:, p * LANES:(p + 1) * LANES]

        load_bias = lambda: jnp.concatenate(
            [far_bias[min(n_kb - 1 - kb, 2)] for kb in range(n_kb)], axis=1)
        return (lambda p: q_cls[c, local * qb:(local + 1) * qb, lanes(p)],
                lambda p: k_cls[c, 0:n_kb * qb, lanes(p)],
                lambda p: v_cls[c, 0:n_kb * qb, lanes(p)], load_bias, store)

    is_first = (step == 0).astype(jnp.int32)

    def near_block(j):
        q_rows = pl.ds(j * qb, qb)
        if j == 0:
            start = pl.multiple_of((step * (rows // qb) - 1 + is_first) * qb, qb)
            load_bias = lambda: near_bias[is_first]
        else:
            start = pl.multiple_of((step * (rows // qb) + j - 1) * qb, qb)
            load_bias = lambda: near_bias[0]
        kv_rows = pl.ds(start, 2 * qb)

        def store(num, den, top):
            num_f, den_f, top_f = [
                jnp.concatenate([t[p, q_rows, :] for p in range(n_pairs)], axis=1) for t in far]
            both = jnp.maximum(top, top_f)
            w_near, w_far = jnp.exp2(top - both), jnp.exp2(top_f - both)
            o_ref[q_rows, :] = ((w_near * num + w_far * num_f)
                                / (w_near * den + w_far * den_f)).astype(o_ref.dtype)

        return (lambda p: q_ref[q_rows, lanes(p)], lambda p: k_ref[kv_rows, lanes(p)],
                lambda p: v_ref[kv_rows, lanes(p)], load_bias, store)

    locals_per_step = per_class // qb
    for g in range(n_steps):
        @pl.when(step == g)
        def _(g=g):
            blocks = []
            for local in range(locals_per_step):
                n_kb = g * locals_per_step + local + 1
                blocks += [far_block(c, local, n_kb) for c in range(FAR_DIL)]
                blocks += [near_block(local * FAR_DIL + j) for j in range(FAR_DIL)]
            _attn_blocks(blocks)


def _attention(q, k, v):
    b, s, w = q.shape
    assert DILATIONS == (1, FAR_DIL, FAR_DIL * FAR_RATIO) and s == Q_BLOCK * DILATIONS[-1]
    rows = ATTN_STEP_ROWS
    q_blk = pl.BlockSpec((None, rows, w), lambda i, g: (i, g, 0))
    kv_blk = pl.BlockSpec((None, s, w), lambda i, g: (i, 0, 0))
    return pl.pallas_call(
        _attn_kernel,
        grid=(b, s // rows),
        in_specs=[q_blk, kv_blk, kv_blk],
        out_specs=q_blk,
        out_shape=jax.ShapeDtypeStruct((b, s, w), BF16),
        scratch_shapes=[
            pltpu.VMEM((w // LANES, rows, LANES), F32),
            pltpu.VMEM((FAR_DIL, rows // FAR_DIL, w), BF16),
            pltpu.VMEM((FAR_DIL, s // FAR_DIL, w), BF16),
            pltpu.VMEM((FAR_DIL, s // FAR_DIL, w), BF16),
            pltpu.VMEM((w // LANES, rows, LANES), F32),
            pltpu.VMEM((w // LANES, rows, LANES), F32),
            pltpu.VMEM((w // LANES, rows, LANES), F32),
            pltpu.VMEM((2, Q_BLOCK, 2 * Q_BLOCK), F32),
            pltpu.VMEM((3, Q_BLOCK, Q_BLOCK), F32),
        ],
        compiler_params=pltpu.CompilerParams(
            dimension_semantics=("parallel", "arbitrary"), vmem_limit_bytes=VMEM_LIMIT),
        name="dilated_attn",
    )(q, k, v)


def _softplus(x):
    return jnp.maximum(x, 0.0) + jnp.log1p(jnp.exp(-jnp.abs(x)))


def _silu(x):
    h = 0.5 * x
    return h + h * jnp.tanh(h)


def _ssd_kernel(act_ref, gate_ref, dtc_ref, dtr_ref, alog_row_ref, alog_col_ref, dskip_ref,
                nw_ref, y_ref, state):
    L = CHUNK

    @pl.when(pl.program_id(1) == 0)
    def _():
        state[...] = jnp.zeros(state.shape, F32)

    a_row = -jnp.exp(alog_row_ref[...]) * LOG2_E
    a_col = -jnp.exp(alog_col_ref[...]) * LOG2_E
    ri = lax.broadcasted_iota(jnp.int32, (L, L), 0)
    ci = lax.broadcasted_iota(jnp.int32, (L, L), 1)
    causal = ci <= ri
    tril = jnp.where(causal, 1.0, 0.0).astype(BF16)
    triu = jnp.where(ri <= ci, 1.0, 0.0).astype(BF16)
    gs = N_SSD_GROUPS * SSD_STATE
    gw = SSD_WIDTH // N_SSD_GROUPS

    n_pairs = SSD_WIDTH // LANES
    pairs_per_group = n_pairs // N_SSD_GROUPS
    head0 = lax.broadcasted_iota(jnp.int32, (L, LANES), 1) < SSD_HEAD_DIM
    head0_row = head0[:1, :]
    spread = (lax.broadcasted_iota(jnp.int32, (LANES, SSD_WIDTH), 0)
              == lax.broadcasted_iota(jnp.int32, (LANES, SSD_WIDTH), 1) // SSD_HEAD_DIM)
    spread = jnp.where(spread, 1.0, 0.0).astype(BF16)

    def prologue(chunk):
        rows = pl.ds(chunk * L, L)
        act = act_ref[rows, :]
        xs = act[:, :SSD_WIDTH]
        xs_b = xs.astype(BF16)
        bm = act[:, SSD_WIDTH:SSD_WIDTH + gs]
        cm = act[:, SSD_WIDTH + gs:SSD_WIDTH + 2 * gs]
        dt_c = dtc_ref[rows, :]
        dt_r = dtr_ref[:, chunk * L:(chunk + 1) * L]
        acs_c = _split_dot(tril, dt_c * a_row, 3, lhs_split=False)
        acs_r = _split_dot(dt_r * a_col, triu, 3)
        e_wide = _split_dot(jnp.exp2(acs_c), spread, 2)
        pairs = []
        for g in range(N_SSD_GROUPS):
            b_g = bm[:, g * SSD_STATE:(g + 1) * SSD_STATE]
            c_gb = cm[:, g * SSD_STATE:(g + 1) * SSD_STATE].astype(BF16)
            cb = _dot_nt(c_gb, b_g.astype(BF16))
            b_gt = b_g.T
            for q in range(pairs_per_group):
                p = g * pairs_per_group + q
                x_pair = xs_b[:, p * LANES:(p + 1) * LANES]
                zero = jnp.zeros_like(x_pair)
                x_split = jnp.concatenate([jnp.where(head0, x_pair, zero),
                                           jnp.where(head0, zero, x_pair)], axis=0)
                w_diag, w_state, last = [], [], []
                for h in (2 * p, 2 * p + 1):
                    acs_col = acs_c[:, h:h + 1]
                    acs_row = acs_r[h:h + 1, :]
                    dt_row = dt_r[h:h + 1, :]
                    seg = jnp.exp2(jnp.where(causal, acs_col - acs_row, NEG_BIG))
                    w_diag.append((cb * seg * dt_row).astype(BF16))
                    acs_last = acs_row[:, L - 1:L]
                    w_state.append((b_gt * (jnp.exp2(acs_last - acs_row) * dt_row)).astype(BF16))
                    last.append(jnp.exp2(acs_last))
                pairs.append((c_gb,
                              _dot(jnp.concatenate(w_diag, axis=1), x_split),
                              _dot(jnp.concatenate(w_state, axis=1), x_split),
                              jnp.where(head0_row, last[0], last[1])))
        return rows, xs, e_wide, pairs

    states = [state[p] for p in range(n_pairs)]
    n_chunks = act_ref.shape[0] // L
    queue = [prologue(c) for c in range(min(SSD_LOOKAHEAD, n_chunks))]
    for chunk in range(n_chunks):
        if chunk + SSD_LOOKAHEAD < n_chunks:
            queue.append(prologue(chunk + SSD_LOOKAHEAD))
        rows, xs, e_wide, pairs = queue.pop(0)
        ys = []
        for p, (c_gb, y_within, increment, decay) in enumerate(pairs):
            ys.append(y_within + e_wide[:, p * LANES:(p + 1) * LANES]
                      * _dot(c_gb, states[p].astype(BF16)))
            states[p] = decay * states[p] + increment
        y = (jnp.concatenate(ys, axis=1) + dskip_ref[...] * xs) * gate_ref[rows, :]

        outs = []
        for g in range(N_SSD_GROUPS):
            yg = y[:, g * gw:(g + 1) * gw]
            outs.append(yg * lax.rsqrt(jnp.mean(yg * yg, axis=-1, keepdims=True) + EPS))
        y_ref[rows, :] = (jnp.concatenate(outs, axis=1) * nw_ref[...]).astype(y_ref.dtype)

    for p in range(n_pairs):
        state[p] = states[p]


def _ssd(act, gate, dtc, dtr, alog_row, alog_col, dskip, nw):
    b, s, _ = act.shape
    rows = SSD_CHUNKS_PER_STEP * CHUNK
    tok = lambda width: pl.BlockSpec((None, rows, width), lambda i, c: (i, c, 0))
    return pl.pallas_call(
        _ssd_kernel,
        grid=(b, s // rows),
        in_specs=[
            tok(CONV_CHANNELS), tok(SSD_WIDTH), tok(LANES),
            pl.BlockSpec((None, DT_ROWS, rows), lambda i, c: (i, 0, c)),
            _resident(alog_row.shape), _resident(alog_col.shape),
            _resident(dskip.shape), _resident(nw.shape),
        ],
        out_specs=tok(SSD_WIDTH),
        out_shape=jax.ShapeDtypeStruct((b, s, SSD_WIDTH), BF16),
        scratch_shapes=[pltpu.VMEM((SSD_WIDTH // LANES, SSD_STATE, LANES), F32)],
        compiler_params=pltpu.CompilerParams(
            dimension_semantics=("parallel", "arbitrary"), vmem_limit_bytes=VMEM_LIMIT),
        name="ssd",
    )(act, gate, dtc, dtr, alog_row, alog_col, dskip, nw)


def _out_mlp_kernel(x_ref, oa_ref, ys_ref, woa_ref, wos_ref, nw_ref, wup_ref, wdn_ref, o_ref):
    x1 = x_ref[...] + _dot(oa_ref[...], woa_ref[...]) + _dot(ys_ref[...], wos_ref[...])
    hm = x1 * lax.rsqrt(jnp.mean(x1 * x1, axis=-1, keepdims=True) + EPS) * nw_ref[...]
    up = _dot(hm.astype(BF16), wup_ref[...])
    act = jnp.square(jnp.maximum(up, 0.0)).astype(BF16)
    o_ref[...] = x1 + _dot(act, wdn_ref[...])


def _out_mlp(x, o_attn, y_ssd, wo_a, wo_s, mlp_norm_w, w_up, w_down, tm):
    t, d = x.shape
    tok = lambda width: pl.BlockSpec((tm, width), lambda i: (i, 0))
    return pl.pallas_call(
        _out_mlp_kernel,
        grid=(t // tm,),
        in_specs=[
            tok(d), tok(ATTN_WIDTH), tok(SSD_WIDTH),
            _resident(wo_a.shape), _resident(wo_s.shape), _resident((1, d)),
            _resident(w_up.shape), _resident(w_down.shape),
        ],
        out_specs=tok(d),
        out_shape=jax.ShapeDtypeStruct((t, d), F32),
        compiler_params=pltpu.CompilerParams(
            dimension_semantics=("parallel",), vmem_limit_bytes=VMEM_LIMIT),
        name="out_mlp",
    )(x, o_attn, y_ssd, wo_a, wo_s, mlp_norm_w, w_up, w_down)


def _rope_tables(s):
    half = HEAD_DIM // 2
    inv_freq = ROPE_THETA ** (-jnp.arange(half, dtype=F32) / half)
    lane_freq = jnp.tile(inv_freq, LANES // half)
    sign = jnp.tile(jnp.repeat(jnp.array([-1.0, 1.0], F32), half), LANES // HEAD_DIM)
    ang = jnp.arange(s, dtype=F32)[:, None] * lane_freq[None, :]
    return jnp.cos(ang), jnp.sin(ang) * sign[None, :]


def _pad_lanes(row):
    return jnp.pad(row, ((0, 0), (0, LANES - row.shape[1])))


def _layer(x, attn_norm_w, w_in, q_norm_w, k_norm_w, conv_w, conv_b, dt_bias, a_log, d_skip,
           ssd_norm_w, w_out, mlp_norm_w, w_up, w_down):
    b, s, d = x.shape
    w_t = w_in.T
    w_main = w_t[:MAIN_WIDTH].astype(BF16)
    w_dt_c = jnp.pad(w_t[MAIN_WIDTH:], ((0, LANES - N_SSD_HEADS), (0, 0))).astype(BF16)
    head_of = jnp.arange(ATTN_WIDTH, dtype=jnp.int32) // HEAD_DIM
    group_mean = jnp.where(head_of[:, None] == head_of[None, :], 1.0 / HEAD_DIM, 0.0).astype(BF16)
    cos, sin = _rope_tables(s)
    q, k, v, gate, act, dtc, dtr = _in_proj(
        x, attn_norm_w[None, :], w_main, w_dt_c, group_mean,
        jnp.tile(q_norm_w, N_ATTN_HEADS)[None, :], jnp.tile(k_norm_w, N_ATTN_HEADS)[None, :],
        cos, sin, conv_w, conv_b[None, :], _pad_lanes(dt_bias[None, :]), tm=PROJ_TOKENS)

    o_attn = _attention(q, k, v)

    col = lambda p: jnp.pad(p[:, None], ((0, DT_ROWS - N_SSD_HEADS), (0, 0)))
    y_ssd = _ssd(act, gate, dtc, dtr, _pad_lanes(a_log[None, :]), col(a_log),
                 jnp.repeat(d_skip, SSD_HEAD_DIM)[None, :], ssd_norm_w[None, :])

    out = _out_mlp(x.reshape(b * s, d), o_attn.reshape(b * s, ATTN_WIDTH),
                   y_ssd.reshape(b * s, SSD_WIDTH),
                   w_out[:ATTN_WIDTH].astype(BF16), w_out[ATTN_WIDTH:].astype(BF16),
                   mlp_norm_w[None, :], w_up.astype(BF16), w_down.astype(BF16), tm=MLP_TOKENS)
    return out.reshape(b, s, d)


def kernel(x, attn_norm_w, w_in, q_norm_w, k_norm_w, conv_w, conv_b, dt_bias, a_log, d_skip,
           ssd_norm_w, w_out, mlp_norm_w, w_up, w_down):
    for i in range(attn_norm_w.shape[0]):
        x = _layer(x, attn_norm_w[i], w_in[i], q_norm_w[i], k_norm_w[i], conv_w[i], conv_b[i],
                   dt_bias[i], a_log[i], d_skip[i], ssd_norm_w[i], w_out[i], mlp_norm_w[i],
                   w_up[i], w_down[i])
    return x
```

```python
import functools

import jax
import jax.numpy as jnp
from jax import lax
from jax.experimental import pallas as pl
from jax.experimental.pallas import tpu as pltpu

F32 = jnp.float32
BF16 = jnp.bfloat16

HEAD_DIM = 64
N_ATTN_HEADS = 8
ATTN_WIDTH = N_ATTN_HEADS * HEAD_DIM
ROPE_THETA = 10000.0
Q_BLOCK = 128
DILATIONS = (1, 4, 16)
FAR_DIL = 4
FAR_RATIO = 4
ATTN_LOOKAHEAD = 5
ATTN_SIDE_JOB_SPACING = 2
SSD_HEAD_DIM = 64
N_SSD_HEADS = 8
SSD_WIDTH = N_SSD_HEADS * SSD_HEAD_DIM
N_SSD_GROUPS = 2
HEADS_PER_GROUP = N_SSD_HEADS // N_SSD_GROUPS
SSD_STATE = 128
CONV_WIDTH = 4
CHUNK = 128
SSD_CHUNKS_PER_STEP = 16
SSD_LOOKAHEAD = 1
CONV_CHANNELS = SSD_WIDTH + 2 * N_SSD_GROUPS * SSD_STATE
CONV_HISTORY = 16
ROPE_ROWS = 512
CONV_GROUP = 256
EPS = 1e-6
NEG_BIG = -1e30
LOG2_E = 1.4426950408889634
SCORE_SCALE = HEAD_DIM ** -0.5 * LOG2_E

LANES = 128
DT_ROWS = 16
MAIN_WIDTH = 3 * ATTN_WIDTH + SSD_WIDTH + CONV_CHANNELS
VMEM_LIMIT = 56 * 1024 * 1024
PROJ_TOKENS = 1024
MLP_TOKENS = 512


def _dot(a, b):
    return jnp.dot(a, b, preferred_element_type=F32)


def _dot_nt(a, b):
    return lax.dot_general(a, b, (((1,), (1,)), ((), ())), preferred_element_type=F32)


def _split_dot(a, b, parts, lhs_split=True):
    src = a if lhs_split else b
    acc = None
    rem = src
    for _ in range(parts):
        piece = rem.astype(BF16)
        rem = rem - piece.astype(F32)
        term = _dot(piece, b) if lhs_split else _dot(a, piece)
        acc = term if acc is None else acc + term
    return acc


def _resident(shape):
    zeros = (0,) * len(shape)
    return pl.BlockSpec(shape, lambda *_: zeros, pipeline_mode=pl.Buffered(1))


def _in_proj_kernel(x_ref, xprev_ref, nw_ref, w_ref, wdt_ref, gm_ref, qw_ref, kw_ref,
                    cos_ref, sin_ref, cw_ref, cb_ref, dtb_ref,
                    q_ref, k_ref, v_ref, gate_ref, act_ref, dtc_ref, dtr_ref, ubuf):
    tm = x_ref.shape[0]

    def normed(x):
        y = x * lax.rsqrt(jnp.mean(x * x, axis=-1, keepdims=True) + EPS) * nw_ref[...]
        return y.astype(BF16)

    hb = normed(x_ref[...])

    lane = lax.broadcasted_iota(jnp.int32, (ROPE_ROWS, ATTN_WIDTH), 1)
    first_half = (lane % HEAD_DIM) < (HEAD_DIM // 2)

    def head_norm_rope(t, w, out_ref, scale):
        for r in range(tm // ROPE_ROWS):
            rows = pl.ds(r * ROPE_ROWS, ROPE_ROWS)
            tr = t[r * ROPE_ROWS:(r + 1) * ROPE_ROWS, :]
            cos = jnp.concatenate([cos_ref[rows, :]] * (ATTN_WIDTH // LANES), axis=1)
            sin = jnp.concatenate([sin_ref[rows, :]] * (ATTN_WIDTH // LANES), axis=1)
            ms = _dot((tr * tr).astype(BF16), gm_ref[...])
            n = tr * lax.rsqrt(ms + EPS) * w
            partner = jnp.where(first_half,
                                pltpu.roll(n, ATTN_WIDTH - HEAD_DIM // 2, 1),
                                pltpu.roll(n, HEAD_DIM // 2, 1))
            out_ref[rows, :] = ((n * cos + partner * sin) * scale).astype(BF16)

    o = 0
    q = _dot_nt(hb, w_ref[o:o + ATTN_WIDTH, :]); o += ATTN_WIDTH
    head_norm_rope(q, qw_ref[...], q_ref, SCORE_SCALE)
    k = _dot_nt(hb, w_ref[o:o + ATTN_WIDTH, :]); o += ATTN_WIDTH
    head_norm_rope(k, kw_ref[...], k_ref, 1.0)
    v_ref[...] = _dot_nt(hb, w_ref[o:o + ATTN_WIDTH, :]).astype(BF16); o += ATTN_WIDTH
    gate_ref[...] = _silu(_dot_nt(hb, w_ref[o:o + SSD_WIDTH, :])); o += SSD_WIDTH

    hb_prev = normed(xprev_ref[...])
    for c in range(CONV_CHANNELS // CONV_GROUP):
        lanes = pl.ds(c * CONV_GROUP, CONV_GROUP)
        w_c = w_ref[o + c * CONV_GROUP:o + (c + 1) * CONV_GROUP, :]
        u = _dot_nt(hb, w_c)
        hist = _dot_nt(hb_prev, w_c)
        ubuf[0:CONV_HISTORY, lanes] = jnp.where(pl.program_id(0) == 0, 0.0, hist)
        ubuf[CONV_HISTORY:CONV_HISTORY + tm, lanes] = u
        conv = cb_ref[:, lanes] + cw_ref[CONV_WIDTH - 1:CONV_WIDTH, lanes] * u
        for tap in range(CONV_WIDTH - 1):
            first = CONV_HISTORY - (CONV_WIDTH - 1 - tap)
            conv = conv + cw_ref[tap:tap + 1, lanes] * ubuf[first:first + tm, lanes]
        act_ref[:, lanes] = _silu(conv)

    dtc = _softplus(_dot_nt(hb, wdt_ref[...]) + dtb_ref[...])
    dtc_ref[...] = dtc
    dtr_ref[...] = dtc.T[:DT_ROWS, :]


def _in_proj(x, attn_norm_w, w_main, w_dt, group_mean, qw, kw, cos, sin, conv_w, conv_b, dt_bias,
             tm):
    b, s, d = x.shape
    grid = (s // tm, b)
    tok = lambda width: pl.BlockSpec((None, tm, width), lambda j, i: (i, j, 0))
    prev = pl.BlockSpec((None, CONV_HISTORY, d),
                        lambda j, i: (i, jnp.maximum(j * (tm // CONV_HISTORY) - 1, 0), 0))
    out_shapes = (
        jax.ShapeDtypeStruct((b, s, ATTN_WIDTH), BF16),
        jax.ShapeDtypeStruct((b, s, ATTN_WIDTH), BF16),
        jax.ShapeDtypeStruct((b, s, ATTN_WIDTH), BF16),
        jax.ShapeDtypeStruct((b, s, SSD_WIDTH), F32),
        jax.ShapeDtypeStruct((b, s, CONV_CHANNELS), F32),
        jax.ShapeDtypeStruct((b, s, LANES), F32),
        jax.ShapeDtypeStruct((b, DT_ROWS, s), F32),
    )
    return pl.pallas_call(
        _in_proj_kernel,
        grid=grid,
        in_specs=[
            tok(d),
            prev,
            _resident((1, d)),
            _resident(w_main.shape),
            _resident(w_dt.shape),
            _resident(group_mean.shape),
            _resident((1, ATTN_WIDTH)),
            _resident((1, ATTN_WIDTH)),
            pl.BlockSpec((tm, LANES), lambda j, i: (j, 0)),
            pl.BlockSpec((tm, LANES), lambda j, i: (j, 0)),
            _resident(conv_w.shape), _resident(conv_b.shape), _resident(dt_bias.shape),
        ],
        out_specs=(
            tok(ATTN_WIDTH), tok(ATTN_WIDTH), tok(ATTN_WIDTH), tok(SSD_WIDTH),
            tok(CONV_CHANNELS), tok(LANES),
            pl.BlockSpec((None, DT_ROWS, tm), lambda j, i: (i, 0, j)),
        ),
        out_shape=out_shapes,
        scratch_shapes=[pltpu.VMEM((CONV_HISTORY + tm, CONV_CHANNELS), F32)],
        compiler_params=pltpu.CompilerParams(
            dimension_semantics=("parallel", "parallel"), vmem_limit_bytes=VMEM_LIMIT),
        name="in_proj",
    )(x, x, attn_norm_w, w_main, w_dt, group_mean, qw, kw, cos, sin, conv_w, conv_b, dt_bias)


def _attn_blocks(blocks, side_jobs):
    qb = Q_BLOCK
    n_pairs = ATTN_WIDTH // LANES
    head0 = lax.broadcasted_iota(jnp.int32, (qb, LANES), 1) < HEAD_DIM
    items = [(i, pair) for i in range(len(blocks)) for pair in range(n_pairs)]

    def scores(item):
        i, pair = item
        load_q, load_k, _, load_bias, _ = blocks[i]
        qt, kb, bias = load_q(pair), load_k(pair), load_bias()
        out = []
        for h in range(2):
            mine = head0 if h == 0 else jnp.logical_not(head0)
            qh = jnp.where(mine, qt, jnp.zeros_like(qt))
            out.append(_dot_nt(qh, kb) + bias)
        return out

    queue = [scores(item) for item in items[:ATTN_LOOKAHEAD]]
    parts = []
    for n, (i, pair) in enumerate(items):
        if n + ATTN_LOOKAHEAD < len(items):
            queue.append(scores(items[n + ATTN_LOOKAHEAD]))
        pending = queue.pop(0)
        vb = blocks[i][2](pair)
        v_ext = jnp.concatenate([vb, jnp.ones_like(vb)], axis=1)
        heads = []
        for sc in pending:
            m = jnp.max(sc, axis=1, keepdims=True)
            r = _dot(jnp.exp2(sc - m).astype(BF16), v_ext)
            heads.append((r[:, :LANES], r[:, LANES:], jnp.broadcast_to(m, (qb, LANES))))
        parts.append([jnp.where(head0, a, b) for a, b in zip(*heads)])
        if pair == n_pairs - 1:
            blocks[i][4](*[jnp.concatenate(t, axis=1) for t in zip(*parts)])
            parts = []
        if n in side_jobs:
            side_jobs[n]()


def _near_bias(first):
    qb = Q_BLOCK
    row = lax.broadcasted_iota(jnp.int32, (qb, 2 * qb), 0)
    col = lax.broadcasted_iota(jnp.int32, (qb, 2 * qb), 1)
    ok = (col <= row) if first else ((col >= row) & (col <= row + qb))
    return jnp.where(ok, 0.0, NEG_BIG)


def _far_bias(blocks_back):
    qb = Q_BLOCK
    row = lax.broadcasted_iota(jnp.int32, (qb, qb), 0)
    col = lax.broadcasted_iota(jnp.int32, (qb, qb), 1)
    dist = row - col + blocks_back * qb
    in_window = (dist >= 0) & (dist <= qb)
    on_stride = (dist >= 0) & ((dist & (FAR_RATIO - 1)) == 0)
    return jnp.where(in_window & on_stride, 1.0,
                     jnp.where(in_window | on_stride, 0.0, NEG_BIG))


def _attn_kernel(q_ref, k_ref, v_ref, o_ref,
                 tmp, q_cls, k_cls, v_cls, num_far, den_far, top_far, near_bias, far_bias):
    qb = Q_BLOCK
    s = q_ref.shape[0]
    group_rows = FAR_DIL * qb
    n_groups = s // group_rows
    n_pairs = ATTN_WIDTH // LANES
    lanes = lambda pair: pl.ds(pair * LANES, LANES)
    far = (num_far, den_far, top_far)

    def class_order_jobs(g):
        def job(t, src, dst, p):
            def run():
                tmp[t, p] = src[g * group_rows:(g + 1) * group_rows, lanes(p)].astype(F32)
                for c in range(FAR_DIL):
                    dst[c, g * qb:(g + 1) * qb, lanes(p)] = (
                        tmp[t, p, pl.ds(c, qb, stride=FAR_DIL), :].astype(BF16))
            return run
        return [job(t, src, dst, p)
                for t, (src, dst) in enumerate(((q_ref, q_cls), (k_ref, k_cls), (v_ref, v_cls)))
                for p in range(n_pairs)]

    for job in class_order_jobs(0):
        job()
    near_bias[0] = _near_bias(False)
    near_bias[1] = _near_bias(True)
    for d in range(3):
        far_bias[d] = _far_bias(d)

    def far_block(c, j):
        token_rows = pl.ds(j * group_rows + c, qb, stride=FAR_DIL)

        def store(*stats):
            for dst, t in zip(far, stats):
                for p in range(n_pairs):
                    dst[p, token_rows, :] = t[:, p * LANES:(p + 1) * LANES]

        load_bias = lambda: jnp.concatenate(
            [far_bias[min(j - kb, 2)] for kb in range(j + 1)], axis=1)
        return (lambda p: q_cls[c, j * qb:(j + 1) * qb, lanes(p)],
                lambda p: k_cls[c, 0:(j + 1) * qb, lanes(p)],
                lambda p: v_cls[c, 0:(j + 1) * qb, lanes(p)], load_bias, store)

    def near_block(j):
        q_rows = pl.ds(j * qb, qb)
        kv_rows = pl.ds(max(j - 1, 0) * qb, 2 * qb)
        load_bias = lambda: near_bias[1 if j == 0 else 0]

        def store(num, den, top):
            num_f, den_f, top_f = [
                jnp.concatenate([t[p, q_rows, :] for p in range(n_pairs)], axis=1) for t in far]
            both = jnp.maximum(top, top_f)
            w_near, w_far = jnp.exp2(top - both), jnp.exp2(top_f - both)
            o_ref[q_rows, :] = ((w_near * num + w_far * num_f)
                                / (w_near * den + w_far * den_f)).astype(o_ref.dtype)

        return (lambda p: q_ref[q_rows, lanes(p)], lambda p: k_ref[kv_rows, lanes(p)],
                lambda p: v_ref[kv_rows, lanes(p)], load_bias, store)

    blocks, side_jobs = [], {}
    for g in range(n_groups):
        if g + 1 < n_groups:
            first_item = len(blocks) * n_pairs
            for n, job in enumerate(class_order_jobs(g + 1)):
                side_jobs[first_item + ATTN_SIDE_JOB_SPACING * n] = job
        blocks += [far_block(c, g) for c in range(FAR_DIL)]
        blocks += [near_block(g * FAR_DIL + j) for j in range(FAR_DIL)]
    _attn_blocks(blocks, side_jobs)


def _attention(q, k, v):
    b, s, w = q.shape
    assert DILATIONS == (1, FAR_DIL, FAR_DIL * FAR_RATIO) and s == Q_BLOCK * DILATIONS[-1]
    row = pl.BlockSpec((None, s, w), lambda i: (i, 0, 0))
    return pl.pallas_call(
        _attn_kernel,
        grid=(b,),
        in_specs=[row, row, row],
        out_specs=row,
        out_shape=jax.ShapeDtypeStruct((b, s, w), BF16),
        scratch_shapes=[
            pltpu.VMEM((3, w // LANES, FAR_DIL * Q_BLOCK, LANES), F32),
            pltpu.VMEM((FAR_DIL, s // FAR_DIL, w), BF16),
            pltpu.VMEM((FAR_DIL, s // FAR_DIL, w), BF16),
            pltpu.VMEM((FAR_DIL, s // FAR_DIL, w), BF16),
            pltpu.VMEM((w // LANES, s, LANES), F32),
            pltpu.VMEM((w // LANES, s, LANES), F32),
            pltpu.VMEM((w // LANES, s, LANES), F32),
            pltpu.VMEM((2, Q_BLOCK, 2 * Q_BLOCK), F32),
            pltpu.VMEM((3, Q_BLOCK, Q_BLOCK), F32),
        ],
        compiler_params=pltpu.CompilerParams(
            dimension_semantics=("parallel",), vmem_limit_bytes=VMEM_LIMIT),
        name="dilated_attn",
    )(q, k, v)


def _softplus(x):
    return jnp.maximum(x, 0.0) + jnp.log1p(jnp.exp(-jnp.abs(x)))


def _silu(x):
    h = 0.5 * x
    return h + h * jnp.tanh(h)


def _ssd_kernel(act_ref, gate_ref, dtc_ref, dtr_ref, alog_row_ref, alog_col_ref, dskip_ref,
                nw_ref, y_ref, state):
    L = CHUNK

    @pl.when(pl.program_id(1) == 0)
    def _():
        state[...] = jnp.zeros(state.shape, F32)

    a_row = -jnp.exp(alog_row_ref[...]) * LOG2_E
    a_col = -jnp.exp(alog_col_ref[...]) * LOG2_E
    ri = lax.broadcasted_iota(jnp.int32, (L, L), 0)
    ci = lax.broadcasted_iota(jnp.int32, (L, L), 1)
    causal = ci <= ri
    tril = jnp.where(causal, 1.0, 0.0).astype(BF16)
    triu = jnp.where(ri <= ci, 1.0, 0.0).astype(BF16)
    gs = N_SSD_GROUPS * SSD_STATE
    gw = SSD_WIDTH // N_SSD_GROUPS

    n_pairs = SSD_WIDTH // LANES
    pairs_per_group = n_pairs // N_SSD_GROUPS
    head0 = lax.broadcasted_iota(jnp.int32, (L, LANES), 1) < SSD_HEAD_DIM
    head0_row = head0[:1, :]
    spread = (lax.broadcasted_iota(jnp.int32, (LANES, SSD_WIDTH), 0)
              == lax.broadcasted_iota(jnp.int32, (LANES, SSD_WIDTH), 1) // SSD_HEAD_DIM)
    spread = jnp.where(spread, 1.0, 0.0).astype(BF16)

    def prologue(chunk):
        rows = pl.ds(chunk * L, L)
        act = act_ref[rows, :]
        xs = act[:, :SSD_WIDTH]
        xs_b = xs.astype(BF16)
        bm = act[:, SSD_WIDTH:SSD_WIDTH + gs]
        cm = act[:, SSD_WIDTH + gs:SSD_WIDTH + 2 * gs]
        dt_c = dtc_ref[rows, :]
        dt_r = dtr_ref[:, chunk * L:(chunk + 1) * L]
        acs_c = _split_dot(tril, dt_c * a_row, 3, lhs_split=False)
        acs_r = _split_dot(dt_r * a_col, triu, 3)
        e_wide = _split_dot(jnp.exp2(acs_c), spread, 2)
        pairs = []
        for g in range(N_SSD_GROUPS):
            b_g = bm[:, g * SSD_STATE:(g + 1) * SSD_STATE]
            c_gb = cm[:, g * SSD_STATE:(g + 1) * SSD_STATE].astype(BF16)
            cb = _dot_nt(c_gb, b_g.astype(BF16))
            b_gt = b_g.T
            for q in range(pairs_per_group):
                p = g * pairs_per_group + q
                x_pair = xs_b[:, p * LANES:(p + 1) * LANES]
                zero = jnp.zeros_like(x_pair)
                x_split = jnp.concatenate([jnp.where(head0, x_pair, zero),
                                           jnp.where(head0, zero, x_pair)], axis=0)
                w_diag, w_state, last = [], [], []
                for h in (2 * p, 2 * p + 1):
                    acs_col = acs_c[:, h:h + 1]
                    acs_row = acs_r[h:h + 1, :]
                    dt_row = dt_r[h:h + 1, :]
                    seg = jnp.exp2(jnp.where(causal, acs_col - acs_row, NEG_BIG))
                    w_diag.append((cb * seg * dt_row).astype(BF16))
                    acs_last = acs_row[:, L - 1:L]
                    w_state.append((b_gt * (jnp.exp2(acs_last - acs_row) * dt_row)).astype(BF16))
                    last.append(jnp.exp2(acs_last))
                pairs.append((c_gb,
                              _dot(jnp.concatenate(w_diag, axis=1), x_split),
                              _dot(jnp.concatenate(w_state, axis=1), x_split),
                              jnp.where(head0_row, last[0], last[1])))
        return rows, xs, e_wide, pairs

    states = [state[p] for p in range(n_pairs)]
    n_chunks = act_ref.shape[0] // L
    queue = [prologue(c) for c in range(min(SSD_LOOKAHEAD, n_chunks))]
    for chunk in range(n_chunks):
        if chunk + SSD_LOOKAHEAD < n_chunks:
            queue.append(prologue(chunk + SSD_LOOKAHEAD))
        rows, xs, e_wide, pairs = queue.pop(0)
        ys = []
        for p, (c_gb, y_within, increment, decay) in enumerate(pairs):
            ys.append(y_within + e_wide[:, p * LANES:(p + 1) * LANES]
                      * _dot(c_gb, states[p].astype(BF16)))
            states[p] = decay * states[p] + increment
        y = (jnp.concatenate(ys, axis=1) + dskip_ref[...] * xs) * gate_ref[rows, :]

        outs = []
        for g in range(N_SSD_GROUPS):
            yg = y[:, g * gw:(g + 1) * gw]
            outs.append(yg * lax.rsqrt(jnp.mean(yg * yg, axis=-1, keepdims=True) + EPS))
        y_ref[rows, :] = (jnp.concatenate(outs, axis=1) * nw_ref[...]).astype(y_ref.dtype)

    for p in range(n_pairs):
        state[p] = states[p]


def _ssd(act, gate, dtc, dtr, alog_row, alog_col, dskip, nw):
    b, s, _ = act.shape
    rows = SSD_CHUNKS_PER_STEP * CHUNK
    tok = lambda width: pl.BlockSpec((None, rows, width), lambda i, c: (i, c, 0))
    return pl.pallas_call(
        _ssd_kernel,
        grid=(b, s // rows),
        in_specs=[
            tok(CONV_CHANNELS), tok(SSD_WIDTH), tok(LANES),
            pl.BlockSpec((None, DT_ROWS, rows), lambda i, c: (i, 0, c)),
            _resident(alog_row.shape), _resident(alog_col.shape),
            _resident(dskip.shape), _resident(nw.shape),
        ],
        out_specs=tok(SSD_WIDTH),
        out_shape=jax.ShapeDtypeStruct((b, s, SSD_WIDTH), BF16),
        scratch_shapes=[pltpu.VMEM((SSD_WIDTH // LANES, SSD_STATE, LANES), F32)],
        compiler_params=pltpu.CompilerParams(
            dimension_semantics=("parallel", "arbitrary"), vmem_limit_bytes=VMEM_LIMIT),
        name="ssd",
    )(act, gate, dtc, dtr, alog_row, alog_col, dskip, nw)


def _out_mlp_kernel(x_ref, oa_ref, ys_ref, woa_ref, wos_ref, nw_ref, wup_ref, wdn_ref, o_ref):
    x1 = x_ref[...] + _dot(oa_ref[...], woa_ref[...]) + _dot(ys_ref[...], wos_ref[...])
    hm = x1 * lax.rsqrt(jnp.mean(x1 * x1, axis=-1, keepdims=True) + EPS) * nw_ref[...]
    up = _dot(hm.astype(BF16), wup_ref[...])
    act = jnp.square(jnp.maximum(up, 0.0)).astype(BF16)
    o_ref[...] = x1 + _dot(act, wdn_ref[...])


def _out_mlp(x, o_attn, y_ssd, wo_a, wo_s, mlp_norm_w, w_up, w_down, tm):
    t, d = x.shape
    tok = lambda width: pl.BlockSpec((tm, width), lambda i: (i, 0))
    return pl.pallas_call(
        _out_mlp_kernel,
        grid=(t // tm,),
        in_specs=[
            tok(d), tok(ATTN_WIDTH), tok(SSD_WIDTH),
            _resident(wo_a.shape), _resident(wo_s.shape), _resident((1, d)),
            _resident(w_up.shape), _resident(w_down.shape),
        ],
        out_specs=tok(d),
        out_shape=jax.ShapeDtypeStruct((t, d), F32),
        compiler_params=pltpu.CompilerParams(
            dimension_semantics=("parallel",), vmem_limit_bytes=VMEM_LIMIT),
        name="out_mlp",
    )(x, o_attn, y_ssd, wo_a, wo_s, mlp_norm_w, w_up, w_down)


def _rope_tables(s):
    half = HEAD_DIM // 2
    inv_freq = ROPE_THETA ** (-jnp.arange(half, dtype=F32) / half)
    lane_freq = jnp.tile(inv_freq, LANES // half)
    sign = jnp.tile(jnp.repeat(jnp.array([-1.0, 1.0], F32), half), LANES // HEAD_DIM)
    ang = jnp.arange(s, dtype=F32)[:, None] * lane_freq[None, :]
    return jnp.cos(ang), jnp.sin(ang) * sign[None, :]


def _pad_lanes(row):
    return jnp.pad(row, ((0, 0), (0, LANES - row.shape[1])))


def _layer(x, attn_norm_w, w_in, q_norm_w, k_norm_w, conv_w, conv_b, dt_bias, a_log, d_skip,
           ssd_norm_w, w_out, mlp_norm_w, w_up, w_down):
    b, s, d = x.shape
    w_t = w_in.T
    w_main = w_t.astype(BF16)
    w_dt_c = jnp.pad(w_t[MAIN_WIDTH:], ((0, LANES - N_SSD_HEADS), (0, 0))).astype(BF16)
    head_of = jnp.arange(ATTN_WIDTH, dtype=jnp.int32) // HEAD_DIM
    group_mean = jnp.where(head_of[:, None] == head_of[None, :], 1.0 / HEAD_DIM, 0.0).astype(BF16)
    cos, sin = _rope_tables(s)
    q, k, v, gate, act, dtc, dtr = _in_proj(
        x, attn_norm_w[None, :], w_main, w_dt_c, group_mean,
        jnp.tile(q_norm_w, N_ATTN_HEADS)[None, :], jnp.tile(k_norm_w, N_ATTN_HEADS)[None, :],
        cos, sin, conv_w, conv_b[None, :], _pad_lanes(dt_bias[None, :]), tm=PROJ_TOKENS)

    o_attn = _attention(q, k, v)

    col = lambda p: jnp.pad(p[:, None], ((0, DT_ROWS - N_SSD_HEADS), (0, 0)))
    y_ssd = _ssd(act, gate, dtc, dtr, _pad_lanes(a_log[None, :]), col(a_log),
                 jnp.repeat(d_skip, SSD_HEAD_DIM)[None, :], ssd_norm_w[None, :])

    out = _out_mlp(x.reshape(b * s, d), o_attn.reshape(b * s, ATTN_WIDTH),
                   y_ssd.reshape(b * s, SSD_WIDTH),
                   w_out[:ATTN_WIDTH].astype(BF16), w_out[ATTN_WIDTH:].astype(BF16),
                   mlp_norm_w[None, :], w_up.astype(BF16), w_down.astype(BF16), tm=MLP_TOKENS)
    return out.reshape(b, s, d)


def kernel(x, attn_norm_w, w_in, q_norm_w, k_norm_w, conv_w, conv_b, dt_bias, a_log, d_skip,
           ssd_norm_w, w_out, mlp_norm_w, w_up, w_down):
    for i in range(attn_norm_w.shape[0]):
        x = _layer(x, attn_norm_w[i], w_in[i], q_norm_w[i], k_norm_w[i], conv_w[i], conv_b[i],
                   dt_bias[i], a_log[i], d_skip[i], ssd_norm_w[i], w_out[i], mlp_norm_w[i],
                   w_up[i], w_down[i])
    return x
```

```python
import functools

import jax
import jax.numpy as jnp
from jax import lax
from jax.experimental import pallas as pl
from jax.experimental.pallas import tpu as pltpu

F32 = jnp.float32
BF16 = jnp.bfloat16

HEAD_DIM = 64
N_ATTN_HEADS = 8
ATTN_WIDTH = N_ATTN_HEADS * HEAD_DIM
ROPE_THETA = 10000.0
Q_BLOCK = 128
DILATIONS = (1, 4, 16)
FAR_DIL = 4
FAR_RATIO = 4
ATTN_LOOKAHEAD = 5
ATTN_SIDE_JOB_SPACING = 2
SSD_HEAD_DIM = 64
N_SSD_HEADS = 8
SSD_WIDTH = N_SSD_HEADS * SSD_HEAD_DIM
N_SSD_GROUPS = 2
HEADS_PER_GROUP = N_SSD_HEADS // N_SSD_GROUPS
SSD_STATE = 128
CONV_WIDTH = 4
CHUNK = 128
SSD_CHUNKS_PER_STEP = 16
SSD_LOOKAHEAD = 1
CONV_CHANNELS = SSD_WIDTH + 2 * N_SSD_GROUPS * SSD_STATE
CONV_HISTORY = 16
ROPE_ROWS = 512
CONV_GROUP = 256
EPS = 1e-6
NEG_BIG = -1e30
LOG2_E = 1.4426950408889634
SCORE_SCALE = HEAD_DIM ** -0.5 * LOG2_E

LANES = 128
DT_ROWS = 16
MAIN_WIDTH = 3 * ATTN_WIDTH + SSD_WIDTH + CONV_CHANNELS
VMEM_LIMIT = 56 * 1024 * 1024
PROJ_TOKENS = 1024
MLP_TOKENS = 512


def _dot(a, b):
    return jnp.dot(a, b, preferred_element_type=F32)


def _dot_nt(a, b):
    return lax.dot_general(a, b, (((1,), (1,)), ((), ())), preferred_element_type=F32)


def _split_dot(a, b, parts, lhs_split=True):
    src = a if lhs_split else b
    acc = None
    rem = src
    for _ in range(parts):
        piece = rem.astype(BF16)
        rem = rem - piece.astype(F32)
        term = _dot(piece, b) if lhs_split else _dot(a, piece)
        acc = term if acc is None else acc + term
    return acc


def _resident(shape):
    zeros = (0,) * len(shape)
    return pl.BlockSpec(shape, lambda *_: zeros, pipeline_mode=pl.Buffered(1))


def _in_proj_kernel(x_ref, xprev_ref, nw_ref, w_ref, wdt_ref, gm_ref, qw_ref, kw_ref,
                    cos_ref, sin_ref, cw_ref, cb_ref, dtb_ref,
                    q_ref, k_ref, v_ref, gate_ref, act_ref, dtc_ref, dtr_ref, ubuf):
    tm = x_ref.shape[0]

    def normed(x):
        y = x * lax.rsqrt(jnp.mean(x * x, axis=-1, keepdims=True) + EPS) * nw_ref[...]
        return y.astype(BF16)

    hb = normed(x_ref[...])

    lane = lax.broadcasted_iota(jnp.int32, (ROPE_ROWS, ATTN_WIDTH), 1)
    first_half = (lane % HEAD_DIM) < (HEAD_DIM // 2)

    def head_norm_rope(t, w, out_ref, scale):
        for r in range(tm // ROPE_ROWS):
            rows = pl.ds(r * ROPE_ROWS, ROPE_ROWS)
            tr = t[r * ROPE_ROWS:(r + 1) * ROPE_ROWS, :]
            cos = jnp.concatenate([cos_ref[rows, :]] * (ATTN_WIDTH // LANES), axis=1)
            sin = jnp.concatenate([sin_ref[rows, :]] * (ATTN_WIDTH // LANES), axis=1)
            ms = _dot((tr * tr).astype(BF16), gm_ref[...])
            n = tr * lax.rsqrt(ms + EPS) * w
            partner = jnp.where(first_half,
                                pltpu.roll(n, ATTN_WIDTH - HEAD_DIM // 2, 1),
                                pltpu.roll(n, HEAD_DIM // 2, 1))
            out_ref[rows, :] = ((n * cos + partner * sin) * scale).astype(BF16)

    o = 0
    q = _dot_nt(hb, w_ref[o:o + ATTN_WIDTH, :]); o += ATTN_WIDTH
    head_norm_rope(q, qw_ref[...], q_ref, SCORE_SCALE)
    k = _dot_nt(hb, w_ref[o:o + ATTN_WIDTH, :]); o += ATTN_WIDTH
    head_norm_rope(k, kw_ref[...], k_ref, 1.0)
    v_ref[...] = _dot_nt(hb, w_ref[o:o + ATTN_WIDTH, :]).astype(BF16); o += ATTN_WIDTH
    gate_ref[...] = _silu(_dot_nt(hb, w_ref[o:o + SSD_WIDTH, :])); o += SSD_WIDTH

    hb_prev = normed(xprev_ref[...])
    for c in range(CONV_CHANNELS // CONV_GROUP):
        lanes = pl.ds(c * CONV_GROUP, CONV_GROUP)
        w_c = w_ref[o + c * CONV_GROUP:o + (c + 1) * CONV_GROUP, :]
        u = _dot_nt(hb, w_c)
        hist = _dot_nt(hb_prev, w_c)
        ubuf[0:CONV_HISTORY, lanes] = jnp.where(pl.program_id(0) == 0, 0.0, hist)
        ubuf[CONV_HISTORY:CONV_HISTORY + tm, lanes] = u
        conv = cb_ref[:, lanes] + cw_ref[CONV_WIDTH - 1:CONV_WIDTH, lanes] * u
        for tap in range(CONV_WIDTH - 1):
            first = CONV_HISTORY - (CONV_WIDTH - 1 - tap)
            conv = conv + cw_ref[tap:tap + 1, lanes] * ubuf[first:first + tm, lanes]
        act_ref[:, lanes] = _silu(conv)

    dtc = _softplus(_dot_nt(hb, wdt_ref[...]) + dtb_ref[...])
    dtc_ref[...] = dtc
    dtr_ref[...] = dtc.T[:DT_ROWS, :]


def _in_proj(x, attn_norm_w, w_main, w_dt, group_mean, qw, kw, cos, sin, conv_w, conv_b, dt_bias,
             tm):
    b, s, d = x.shape
    grid = (s // tm, b)
    tok = lambda width: pl.BlockSpec((None, tm, width), lambda j, i: (i, j, 0))
    prev = pl.BlockSpec((None, CONV_HISTORY, d),
                        lambda j, i: (i, jnp.maximum(j * (tm // CONV_HISTORY) - 1, 0), 0))
    out_shapes = (
        jax.ShapeDtypeStruct((b, s, ATTN_WIDTH), BF16),
        jax.ShapeDtypeStruct((b, s, ATTN_WIDTH), BF16),
        jax.ShapeDtypeStruct((b, s, ATTN_WIDTH), BF16),
        jax.ShapeDtypeStruct((b, s, SSD_WIDTH), F32),
        jax.ShapeDtypeStruct((b, s, CONV_CHANNELS), F32),
        jax.ShapeDtypeStruct((b, s, LANES), F32),
        jax.ShapeDtypeStruct((b, DT_ROWS, s), F32),
    )
    return pl.pallas_call(
        _in_proj_kernel,
        grid=grid,
        in_specs=[
            tok(d),
            prev,
            _resident((1, d)),
            _resident(w_main.shape),
            _resident(w_dt.shape),
            _resident(group_mean.shape),
            _resident((1, ATTN_WIDTH)),
            _resident((1, ATTN_WIDTH)),
            pl.BlockSpec((tm, LANES), lambda j, i: (j, 0)),
            pl.BlockSpec((tm, LANES), lambda j, i: (j, 0)),
            _resident(conv_w.shape), _resident(conv_b.shape), _resident(dt_bias.shape),
        ],
        out_specs=(
            tok(ATTN_WIDTH), tok(ATTN_WIDTH), tok(ATTN_WIDTH), tok(SSD_WIDTH),
            tok(CONV_CHANNELS), tok(LANES),
            pl.BlockSpec((None, DT_ROWS, tm), lambda j, i: (i, 0, j)),
        ),
        out_shape=out_shapes,
        scratch_shapes=[pltpu.VMEM((CONV_HISTORY + tm, CONV_CHANNELS), F32)],
        compiler_params=pltpu.CompilerParams(
            dimension_semantics=("parallel", "parallel"), vmem_limit_bytes=VMEM_LIMIT),
        name="in_proj",
    )(x, x, attn_norm_w, w_main, w_dt, group_mean, qw, kw, cos, sin, conv_w, conv_b, dt_bias)


def _attn_blocks(blocks, side_jobs):
    qb = Q_BLOCK
    n_pairs = ATTN_WIDTH // LANES
    head0 = lax.broadcasted_iota(jnp.int32, (qb, LANES), 1) < HEAD_DIM
    items = [(i, pair) for i in range(len(blocks)) for pair in range(n_pairs)]

    def scores(item):
        i, pair = item
        load_q, load_k, _, load_bias, _ = blocks[i]
        qt, kb, bias = load_q(pair), load_k(pair), load_bias()
        out = []
        for h in range(2):
            mine = head0 if h == 0 else jnp.logical_not(head0)
            qh = jnp.where(mine, qt, jnp.zeros_like(qt))
            out.append(_dot_nt(qh, kb) + bias)
        return out

    queue = [scores(item) for item in items[:ATTN_LOOKAHEAD]]
    parts = []
    for n, (i, pair) in enumerate(items):
        if n + ATTN_LOOKAHEAD < len(items):
            queue.append(scores(items[n + ATTN_LOOKAHEAD]))
        pending = queue.pop(0)
        vb = blocks[i][2](pair)
        v_ext = jnp.concatenate([vb, jnp.ones_like(vb)], axis=1)
        heads = []
        for sc in pending:
            m = jnp.max(sc, axis=1, keepdims=True)
            r = _dot(jnp.exp2(sc - m).astype(BF16), v_ext)
            heads.append((r[:, :LANES], r[:, LANES:], jnp.broadcast_to(m, (qb, LANES))))
        parts.append([jnp.where(head0, a, b) for a, b in zip(*heads)])
        if pair == n_pairs - 1:
            blocks[i][4](*[jnp.concatenate(t, axis=1) for t in zip(*parts)])
            parts = []
        if n in side_jobs:
            side_jobs[n]()


def _near_bias(first):
    qb = Q_BLOCK
    row = lax.broadcasted_iota(jnp.int32, (qb, 2 * qb), 0)
    col = lax.broadcasted_iota(jnp.int32, (qb, 2 * qb), 1)
    ok = (col <= row) if first else ((col >= row) & (col <= row + qb))
    return jnp.where(ok, 0.0, NEG_BIG)


def _far_bias(blocks_back):
    qb = Q_BLOCK
    row = lax.broadcasted_iota(jnp.int32, (qb, qb), 0)
    col = lax.broadcasted_iota(jnp.int32, (qb, qb), 1)
    dist = row - col + blocks_back * qb
    in_window = (dist >= 0) & (dist <= qb)
    on_stride = (dist >= 0) & ((dist & (FAR_RATIO - 1)) == 0)
    return jnp.where(in_window & on_stride, 1.0,
                     jnp.where(in_window | on_stride, 0.0, NEG_BIG))


def _attn_kernel(q_ref, k_ref, v_ref, o_ref,
                 tmp, q_cls, k_cls, v_cls, num_far, den_far, top_far, near_bias, far_bias):
    qb = Q_BLOCK
    s = q_ref.shape[0]
    group_rows = FAR_DIL * qb
    n_groups = s // group_rows
    n_pairs = ATTN_WIDTH // LANES
    lanes = lambda pair: pl.ds(pair * LANES, LANES)
    far = (num_far, den_far, top_far)

    def class_order_jobs(g):
        def job(t, src, dst, p):
            def run():
                tmp[t, p] = src[g * group_rows:(g + 1) * group_rows, lanes(p)].astype(F32)
                for c in range(FAR_DIL):
                    dst[c, g * qb:(g + 1) * qb, lanes(p)] = (
                        tmp[t, p, pl.ds(c, qb, stride=FAR_DIL), :].astype(BF16))
            return run
        return [job(t, src, dst, p)
                for t, (src, dst) in enumerate(((q_ref, q_cls), (k_ref, k_cls), (v_ref, v_cls)))
                for p in range(n_pairs)]

    for job in class_order_jobs(0):
        job()
    near_bias[0] = _near_bias(False)
    near_bias[1] = _near_bias(True)
    for d in range(3):
        far_bias[d] = _far_bias(d)

    def far_block(c, j):
        token_rows = pl.ds(j * group_rows + c, qb, stride=FAR_DIL)

        def store(*stats):
            for dst, t in zip(far, stats):
                for p in range(n_pairs):
                    dst[p, token_rows, :] = t[:, p * LANES:(p + 1) * LANES]

        load_bias = lambda: jnp.concatenate(
            [far_bias[min(j - kb, 2)] for kb in range(j + 1)], axis=1)
        return (lambda p: q_cls[c, j * qb:(j + 1) * qb, lanes(p)],
                lambda p: k_cls[c, 0:(j + 1) * qb, lanes(p)],
                lambda p: v_cls[c, 0:(j + 1) * qb, lanes(p)], load_bias, store)

    def near_block(j):
        q_rows = pl.ds(j * qb, qb)
        kv_rows = pl.ds(max(j - 1, 0) * qb, 2 * qb)
        load_bias = lambda: near_bias[1 if j == 0 else 0]

        def store(num, den, top):
            num_f, den_f, top_f = [
                jnp.concatenate([t[p, q_rows, :] for p in range(n_pairs)], axis=1) for t in far]
            both = jnp.maximum(top, top_f)
            w_near, w_far = jnp.exp2(top - both), jnp.exp2(top_f - both)
            o_ref[q_rows, :] = ((w_near * num + w_far * num_f)
                                / (w_near * den + w_far * den_f)).astype(o_ref.dtype)

        return (lambda p: q_ref[q_rows, lanes(p)], lambda p: k_ref[kv_rows, lanes(p)],
                lambda p: v_ref[kv_rows, lanes(p)], load_bias, store)

    blocks, side_jobs = [], {}
    for g in range(n_groups):
        if g + 1 < n_groups:
            first_item = len(blocks) * n_pairs
            for n, job in enumerate(class_order_jobs(g + 1)):
                side_jobs[first_item + ATTN_SIDE_JOB_SPACING * n] = job
        blocks += [far_block(c, g) for c in range(FAR_DIL)]
        blocks += [near_block(g * FAR_DIL + j) for j in range(FAR_DIL)]
    _attn_blocks(blocks, side_jobs)


def _attention(q, k, v):
    b, s, w = q.shape
    assert DILATIONS == (1, FAR_DIL, FAR_DIL * FAR_RATIO) and s == Q_BLOCK * DILATIONS[-1]
    row = pl.BlockSpec((None, s, w), lambda i: (i, 0, 0))
    return pl.pallas_call(
        _attn_kernel,
        grid=(b,),
        in_specs=[row, row, row],
        out_specs=row,
        out_shape=jax.ShapeDtypeStruct((b, s, w), BF16),
        scratch_shapes=[
            pltpu.VMEM((3, w // LANES, FAR_DIL * Q_BLOCK, LANES), F32),
            pltpu.VMEM((FAR_DIL, s // FAR_DIL, w), BF16),
            pltpu.VMEM((FAR_DIL, s // FAR_DIL, w), BF16),
            pltpu.VMEM((FAR_DIL, s // FAR_DIL, w), BF16),
            pltpu.VMEM((w // LANES, s, LANES), F32),
            pltpu.VMEM((w // LANES, s, LANES), F32),
            pltpu.VMEM((w // LANES, s, LANES), F32),
            pltpu.VMEM((2, Q_BLOCK, 2 * Q_BLOCK), F32),
            pltpu.VMEM((3, Q_BLOCK, Q_BLOCK), F32),
        ],
        compiler_params=pltpu.CompilerParams(
            dimension_semantics=("parallel",), vmem_limit_bytes=VMEM_LIMIT),
        name="dilated_attn",
    )(q, k, v)


def _softplus(x):
    return jnp.maximum(x, 0.0) + jnp.log1p(jnp.exp(-jnp.abs(x)))


def _silu(x):
    h = 0.5 * x
    return h + h * jnp.tanh(h)


def _ssd_kernel(act_ref, gate_ref, dtc_ref, dtr_ref, alog_row_ref, alog_col_ref, dskip_ref,
                nw_ref, y_ref, state):
    L = CHUNK

    @pl.when(pl.program_id(1) == 0)
    def _():
        state[...] = jnp.zeros(state.shape, F32)

    a_row = -jnp.exp(alog_row_ref[...]) * LOG2_E
    a_col = -jnp.exp(alog_col_ref[...]) * LOG2_E
    ri = lax.broadcasted_iota(jnp.int32, (L, L), 0)
    ci = lax.broadcasted_iota(jnp.int32, (L, L), 1)
    causal = ci <= ri
    tril = jnp.where(causal, 1.0, 0.0).astype(BF16)
    triu = jnp.where(ri <= ci, 1.0, 0.0).astype(BF16)
    gs = N_SSD_GROUPS * SSD_STATE
    gw = SSD_WIDTH // N_SSD_GROUPS

    n_pairs = SSD_WIDTH // LANES
    pairs_per_group = n_pairs // N_SSD_GROUPS
    head0 = lax.broadcasted_iota(jnp.int32, (L, LANES), 1) < SSD_HEAD_DIM
    head0_row = head0[:1, :]
    spread = (lax.broadcasted_iota(jnp.int32, (LANES, SSD_WIDTH), 0)
              == lax.broadcasted_iota(jnp.int32, (LANES, SSD_WIDTH), 1) // SSD_HEAD_DIM)
    spread = jnp.where(spread, 1.0, 0.0).astype(BF16)

    def prologue(chunk):
        rows = pl.ds(chunk * L, L)
        act = act_ref[rows, :]
        xs = act[:, :SSD_WIDTH]
        xs_b = xs.astype(BF16)
        bm = act[:, SSD_WIDTH:SSD_WIDTH + gs]
        cm = act[:, SSD_WIDTH + gs:SSD_WIDTH + 2 * gs]
        dt_c = dtc_ref[rows, :]
        dt_r = dtr_ref[:, chunk * L:(chunk + 1) * L]
        acs_c = _split_dot(tril, dt_c * a_row, 3, lhs_split=False)
        acs_r = _split_dot(dt_r * a_col, triu, 3)
        e_wide = _split_dot(jnp.exp2(acs_c), spread, 2)
        pairs = []
        for g in range(N_SSD_GROUPS):
            b_g = bm[:, g * SSD_STATE:(g + 1) * SSD_STATE]
            c_gb = cm[:, g * SSD_STATE:(g + 1) * SSD_STATE].astype(BF16)
            cb = _dot_nt(c_gb, b_g.astype(BF16))
            b_gt = b_g.T
            for q in range(pairs_per_group):
                p = g * pairs_per_group + q
                x_pair = xs_b[:, p * LANES:(p + 1) * LANES]
                zero = jnp.zeros_like(x_pair)
                x_split = jnp.concatenate([jnp.where(head0, x_pair, zero),
                                           jnp.where(head0, zero, x_pair)], axis=0)
                w_diag, w_state, last = [], [], []
                for h in (2 * p, 2 * p + 1):
                    acs_col = acs_c[:, h:h + 1]
                    acs_row = acs_r[h:h + 1, :]
                    dt_row = dt_r[h:h + 1, :]
                    seg = jnp.exp2(jnp.where(causal, acs_col - acs_row, NEG_BIG))
                    w_diag.append((cb * seg * dt_row).astype(BF16))
                    acs_last = acs_row[:, L - 1:L]
                    w_state.append((b_gt * (jnp.exp2(acs_last - acs_row) * dt_row)).astype(BF16))
                    last.append(jnp.exp2(acs_last))
                pairs.append((c_gb,
                              _dot(jnp.concatenate(w_diag, axis=1), x_split),
                              _dot(jnp.concatenate(w_state, axis=1), x_split),
                              jnp.where(head0_row, last[0], last[1])))
        return rows, xs, e_wide, pairs

    states = [state[p] for p in range(n_pairs)]
    n_chunks = act_ref.shape[0] // L
    queue = [prologue(c) for c in range(min(SSD_LOOKAHEAD, n_chunks))]
    for chunk in range(n_chunks):
        if chunk + SSD_LOOKAHEAD < n_chunks:
            queue.append(prologue(chunk + SSD_LOOKAHEAD))
        rows, xs, e_wide, pairs = queue.pop(0)
        ys = []
        for p, (c_gb, y_within, increment, decay) in enumerate(pairs):
            ys.append(y_within + e_wide[:, p * LANES:(p + 1) * LANES]
                      * _dot(c_gb, states[p].astype(BF16)))
            states[p] = decay * states[p] + increment
        y = (jnp.concatenate(ys, axis=1) + dskip_ref[...] * xs) * gate_ref[rows, :]

        outs = []
        for g in range(N_SSD_GROUPS):
            yg = y[:, g * gw:(g + 1) * gw]
            outs.append(yg * lax.rsqrt(jnp.mean(yg * yg, axis=-1, keepdims=True) + EPS))
        y_ref[rows, :] = (jnp.concatenate(outs, axis=1) * nw_ref[...]).astype(y_ref.dtype)

    for p in range(n_pairs):
        state[p] = states[p]


def _ssd(act, gate, dtc, dtr, alog_row, alog_col, dskip, nw):
    b, s, _ = act.shape
    rows = SSD_CHUNKS_PER_STEP * CHUNK
    tok = lambda width: pl.BlockSpec((None, rows, width), lambda i, c: (i, c, 0))
    return pl.pallas_call(
        _ssd_kernel,
        grid=(b, s // rows),
        in_specs=[
            tok(CONV_CHANNELS), tok(SSD_WIDTH), tok(LANES),
            pl.BlockSpec((None, DT_ROWS, rows), lambda i, c: (i, 0, c)),
            _resident(alog_row.shape), _resident(alog_col.shape),
            _resident(dskip.shape), _resident(nw.shape),
        ],
        out_specs=tok(SSD_WIDTH),
        out_shape=jax.ShapeDtypeStruct((b, s, SSD_WIDTH), BF16),
        scratch_shapes=[pltpu.VMEM((SSD_WIDTH // LANES, SSD_STATE, LANES), F32)],
        compiler_params=pltpu.CompilerParams(
            dimension_semantics=("parallel", "arbitrary"), vmem_limit_bytes=VMEM_LIMIT),
        name="ssd",
    )(act, gate, dtc, dtr, alog_row, alog_col, dskip, nw)


def _out_mlp_kernel(x_ref, oa_ref, ys_ref, woa_ref, wos_ref, nw_ref, wup_ref, wdn_ref, o_ref):
    mixed = jnp.concatenate([oa_ref[...], ys_ref[...]], axis=-1)
    w_out = jnp.concatenate([woa_ref[...], wos_ref[...]], axis=0)
    x1 = x_ref[...] + _dot(mixed, w_out)
    hm = x1 * lax.rsqrt(jnp.mean(x1 * x1, axis=-1, keepdims=True) + EPS) * nw_ref[...]
    up = _dot(hm.astype(BF16), wup_ref[...])
    act = jnp.square(jnp.maximum(up, 0.0)).astype(BF16)
    o_ref[...] = x1 + _dot(act, wdn_ref[...])


def _out_mlp(x, o_attn, y_ssd, wo_a, wo_s, mlp_norm_w, w_up, w_down, tm):
    t, d = x.shape
    tok = lambda width: pl.BlockSpec((tm, width), lambda i: (i, 0))
    return pl.pallas_call(
        _out_mlp_kernel,
        grid=(t // tm,),
        in_specs=[
            tok(d), tok(ATTN_WIDTH), tok(SSD_WIDTH),
            _resident(wo_a.shape), _resident(wo_s.shape), _resident((1, d)),
            _resident(w_up.shape), _resident(w_down.shape),
        ],
        out_specs=tok(d),
        out_shape=jax.ShapeDtypeStruct((t, d), F32),
        compiler_params=pltpu.CompilerParams(
            dimension_semantics=("parallel",), vmem_limit_bytes=VMEM_LIMIT),
        name="out_mlp",
    )(x, o_attn, y_ssd, wo_a, wo_s, mlp_norm_w, w_up, w_down)


def _rope_tables(s):
    half = HEAD_DIM // 2
    inv_freq = ROPE_THETA ** (-jnp.arange(half, dtype=F32) / half)
    lane_freq = jnp.tile(inv_freq, LANES // half)
    sign = jnp.tile(jnp.repeat(jnp.array([-1.0, 1.0], F32), half), LANES // HEAD_DIM)
    ang = jnp.arange(s, dtype=F32)[:, None] * lane_freq[None, :]
    return jnp.cos(ang), jnp.sin(ang) * sign[None, :]


def _pad_lanes(row):
    return jnp.pad(row, ((0, 0), (0, LANES - row.shape[1])))


def _layer(x, attn_norm_w, w_in, q_norm_w, k_norm_w, conv_w, conv_b, dt_bias, a_log, d_skip,
           ssd_norm_w, w_out, mlp_norm_w, w_up, w_down):
    b, s, d = x.shape
    w_t = w_in.T
    w_main = w_t.astype(BF16)
    w_dt_c = jnp.pad(w_t[MAIN_WIDTH:], ((0, LANES - N_SSD_HEADS), (0, 0))).astype(BF16)
    head_of = jnp.arange(ATTN_WIDTH, dtype=jnp.int32) // HEAD_DIM
    group_mean = jnp.where(head_of[:, None] == head_of[None, :], 1.0 / HEAD_DIM, 0.0).astype(BF16)
    cos, sin = _rope_tables(s)
    q, k, v, gate, act, dtc, dtr = _in_proj(
        x, attn_norm_w[None, :], w_main, w_dt_c, group_mean,
        jnp.tile(q_norm_w, N_ATTN_HEADS)[None, :], jnp.tile(k_norm_w, N_ATTN_HEADS)[None, :],
        cos, sin, conv_w, conv_b[None, :], _pad_lanes(dt_bias[None, :]), tm=PROJ_TOKENS)

    o_attn = _attention(q, k, v)

    col = lambda p: jnp.pad(p[:, None], ((0, DT_ROWS - N_SSD_HEADS), (0, 0)))
    y_ssd = _ssd(act, gate, dtc, dtr, _pad_lanes(a_log[None, :]), col(a_log),
                 jnp.repeat(d_skip, SSD_HEAD_DIM)[None, :], ssd_norm_w[None, :])

    out = _out_mlp(x.reshape(b * s, d), o_attn.reshape(b * s, ATTN_WIDTH),
                   y_ssd.reshape(b * s, SSD_WIDTH),
                   w_out[:ATTN_WIDTH].astype(BF16), w_out[ATTN_WIDTH:].astype(BF16),
                   mlp_norm_w[None, :], w_up.astype(BF16), w_down.astype(BF16), tm=MLP_TOKENS)
    return out.reshape(b, s, d)


def kernel(x, attn_norm_w, w_in, q_norm_w, k_norm_w, conv_w, conv_b, dt_bias, a_log, d_skip,
           ssd_norm_w, w_out, mlp_norm_w, w_up, w_down):
    for i in range(attn_norm_w.shape[0]):
        x = _layer(x, attn_norm_w[i], w_in[i], q_norm_w[i], k_norm_w[i], conv_w[i], conv_b[i],
                   dt_bias[i], a_log[i], d_skip[i], ssd_norm_w[i], w_out[i], mlp_norm_w[i],
                   w_up[i], w_down[i])
    return x
```
